```python
import math
import jax
import jax.numpy as jnp
from jax import lax
import numpy as np


D_MODEL = 2048
BATCH = 2
SEQ = 16384
DEPTH = 2

GRID_W = 64
CTX_LEN = 256
N_MIXERS = 4
W_GROUP = D_MODEL // N_MIXERS
S5_CH = 16
S5_GROUPS = W_GROUP // S5_CH
S5_STATE = 64
S5_DT_MIN = 0.001
S5_DT_MAX = 0.1
LRU_BLOCKS = 8
LRU_BLOCK = W_GROUP // LRU_BLOCKS
LRU_CONV = 4
LRU_C = 8.0
DIFF_HEADS = 4
DIFF_D = W_GROUP // (2 * DIFF_HEADS)
ROPE_AX = DIFF_D // 4
ROPE_BASE = 10000.0
Q_BLOCK = 128
RWKV_N = 64
RWKV_HEADS = W_GROUP // RWKV_N
RANK_W = 32
RANK_A = 32
RANK_G = 64
RWKV_COLS = 3 * W_GROUP + RANK_G + RANK_W + RANK_A
IN_SIZES = (W_GROUP,) * 6 + (RWKV_COLS,)
D_IN = 6 * W_GROUP + RWKV_COLS
N_EGROUPS = 4
EXP_PER_GROUP = 8
N_EXPERTS = N_EGROUPS * EXP_PER_GROUP
D_EXPERT = D_MODEL // 2
TOP_K = 2
MOE_BLOCK = 128
NORM_EPS = 1e-6
GN_EPS = 64e-5

kernel_name = 'hybrid_headgroup_diffusion_block'


def _split(z, sizes):
    idx = [int(i) for i in np.cumsum(sizes)[:-1]]
    return jnp.split(z, idx, axis=-1)


def _rms(z, gain):
    zf = z.astype(jnp.float32)
    zf = zf * lax.rsqrt(jnp.mean(zf * zf, axis=-1, keepdims=True) + NORM_EPS)
    return (zf * gain.astype(jnp.float32)).astype(z.dtype)


def _centred_shift(z):
    zp = jnp.pad(z, ((0, 0), (1, 1), (0, 0)))
    return 0.5 * (zp[:, :-2] + zp[:, 2:]) - z


def _dwconv_centred(z, w, b):
    k = w.shape[0]
    left = k // 2
    y = lax.conv_general_dilated(z, w[:, None, :].astype(z.dtype), window_strides=(1,),
                                 padding=[(left, k - 1 - left)],
                                 dimension_numbers=('NWC', 'WIO', 'NWC'),
                                 feature_group_count=z.shape[-1])
    return y + b.astype(z.dtype)


def _lin_comb(e1, e2):
    a1, b1 = e1
    a2, b2 = e2
    return a1 * a2, a2 * b1 + b2


def _cplx_comb(e1, e2):
    a1r, a1i, b1r, b1i = e1
    a2r, a2i, b2r, b2i = e2
    return (a1r * a2r - a1i * a2i, a1r * a2i + a1i * a2r,
            a2r * b1r - a2i * b1i + b2r, a2r * b1i + a2i * b1r + b2i)


def _s5_discretise(lam_re, lam_im, log_step, b_re, b_im):
    f32 = jnp.float32
    lr, li = lam_re.astype(f32), lam_im.astype(f32)
    dt = jnp.exp(log_step.astype(f32))[:, None]
    mag = jnp.exp(lr * dt)
    ar, ai = mag * jnp.cos(li * dt), mag * jnp.sin(li * dt)
    den = lr * lr + li * li
    cr = ((ar - 1.0) * lr + ai * li) / den
    ci = (ai * lr - (ar - 1.0) * li) / den
    br, bi = b_re.astype(f32), b_im.astype(f32)
    bbr = cr[..., None] * br - ci[..., None] * bi
    bbi = cr[..., None] * bi + ci[..., None] * br
    return (ar, ai), (bbr, bbi)


def _s5_states(ug, lam_bar, b_bar, h0, reverse):
    ar, ai = lam_bar
    bbr, bbi = b_bar
    bu_r = jnp.einsum('blgh,gph->blgp', ug, bbr)
    bu_i = jnp.einsum('blgh,gph->blgp', ug, bbi)
    n = ug.shape[1]
    a_r = jnp.broadcast_to(ar, (1, n) + ar.shape)
    a_i = jnp.broadcast_to(ai, (1, n) + ai.shape)
    pr, pi, sr, si = lax.associative_scan(_cplx_comb, (a_r, a_i, bu_r, bu_i), reverse=reverse, axis=1)
    if h0 is not None:
        h0r, h0i = h0[0][:, None], h0[1][:, None]
        sr, si = sr + pr * h0r - pi * h0i, si + pr * h0i + pi * h0r
    end = 0 if reverse else -1
    return sr, si, (sr[:, end], si[:, end])


def _s5_readout(sr, si, c_re, c_im):
    return jnp.einsum('blgp,ghp->blgh', sr, c_re.astype(jnp.float32)) - jnp.einsum('blgp,ghp->blgh', si, c_im.astype(jnp.float32))


def _s5_mixer(u, uc, lam_re, lam_im, log_step, b_re, b_im, c_re, c_im, d_skip, w_glu, b_glu, need_ctx):
    def grp(t):
        return t.reshape(t.shape[0], t.shape[1], S5_GROUPS, S5_CH).astype(jnp.float32)
    ug, ucg = grp(u), grp(uc)
    y_lat, y_ctx = 0.0, 0.0
    for d in range(2):
        rev = d == 1
        lam_bar, b_bar = _s5_discretise(lam_re[d], lam_im[d], log_step[d], b_re[d], b_im[d])
        scr, sci, fin = _s5_states(ucg, lam_bar, b_bar, None, rev)
        slr, sli, _ = _s5_states(ug, lam_bar, b_bar, fin, rev)
        y_lat = y_lat + _s5_readout(slr, sli, c_re[d], c_im[d])
        if need_ctx:
            y_ctx = y_ctx + _s5_readout(scr, sci, c_re[d], c_im[d])
    d_g = d_skip.reshape(S5_GROUPS, S5_CH)

    def glu(y, ugrp):
        y = jax.nn.gelu((y + d_g * ugrp).reshape(ugrp.shape[0], ugrp.shape[1], W_GROUP))
        return y * jax.nn.sigmoid(y @ w_glu + b_glu)
    return glu(y_lat, ug), (glu(y_ctx, ucg) if need_ctx else None)


def _rglru_dir(v, lam, wa, ba, wx, bx, h0, reverse):
    bsz, n, _ = v.shape
    vb = v.reshape(bsz, n, LRU_BLOCKS, LRU_BLOCK)
    r = jax.nn.sigmoid(jnp.einsum('blnc,ncd->blnd', vb, wa).reshape(bsz, n, W_GROUP) + ba)
    i = jax.nn.sigmoid(jnp.einsum('blnc,ncd->blnd', vb, wx).reshape(bsz, n, W_GROUP) + bx)
    log_a = -LRU_C * r * jax.nn.softplus(-lam)
    a = jnp.exp(log_a)
    b = jnp.sqrt(-jnp.expm1(2.0 * log_a)) * (i * v)
    a_cum, h = lax.associative_scan(_lin_comb, (a, b), reverse=reverse, axis=1)
    if h0 is not None:
        h = h + a_cum * h0[:, None]
    return h, h[:, 0 if reverse else -1]


def _rglru_mixer(xl, gl, xcx, gcx, conv_w, conv_b, lam, wa, ba, wx, bx, need_ctx):
    f32 = jnp.float32
    vl = _dwconv_centred(xl, conv_w, conv_b).astype(f32)
    vc = _dwconv_centred(xcx, conv_w, conv_b).astype(f32)
    hl, hc = 0.0, 0.0
    for d in range(2):
        rev = d == 1
        seq_c, fin = _rglru_dir(vc, lam[d], wa[d], ba[d], wx[d], bx[d], None, rev)
        seq_l, _ = _rglru_dir(vl, lam[d], wa[d], ba[d], wx[d], bx[d], fin, rev)
        hl = hl + seq_l
        if need_ctx:
            hc = hc + seq_c
    out_l = hl * jax.nn.gelu(gl.astype(f32))
    out_c = hc * jax.nn.gelu(gcx.astype(f32)) if need_ctx else None
    return out_l, out_c


def _rope_half(t, cos, sin):
    t1, t2 = jnp.split(t, 2, axis=-1)
    return jnp.concatenate([t1 * cos - t2 * sin, t1 * sin + t2 * cos], axis=-1)


def _axial_rope(t, rope):
    cr, sr, cc, scl = rope
    t_row, t_col = jnp.split(t.astype(jnp.float32), 2, axis=-1)
    return jnp.concatenate([_rope_half(t_row, cr, sr), _rope_half(t_col, cc, scl)], axis=-1).astype(t.dtype)


def _diff_attn(q, k, v, qc, kc, vc, rope, gq, gk, lq1, lk1, lq2, lk2, subln, lam_init, need_ctx):
    f32 = jnp.float32
    H, d = DIFF_HEADS, DIFF_D
    scale = d ** -0.5

    def qk_heads(t, g):
        return _rms(t.reshape(t.shape[0], t.shape[1], H, 2, d), g).transpose(0, 2, 1, 3, 4)

    def v_heads(t):
        return t.reshape(t.shape[0], t.shape[1], H, 2 * d).transpose(0, 2, 1, 3)

    qh = _axial_rope(qk_heads(q, gq), rope)
    kh = _axial_rope(qk_heads(k, gk), rope)
    qch, kch, vch = qk_heads(qc, gq), qk_heads(kc, gk), v_heads(vc)
    k_all = jnp.concatenate([kh, kch], axis=2)
    v_all = jnp.concatenate([v_heads(v), vch], axis=2)
    lam = (jnp.exp(jnp.sum(lq1 * lk1)) - jnp.exp(jnp.sum(lq2 * lk2)) + lam_init).astype(f32)

    def attend(qb, kk, vv):
        s = jnp.einsum('bhqmd,bhkmd->bhmqk', qb, kk).astype(f32) * scale
        p = jax.nn.softmax(s, axis=-1)
        pd = p[:, :, 0] - lam * p[:, :, 1]
        return jnp.einsum('bhqk,bhkc->bhqc', pd.astype(vv.dtype), vv)

    bsz, _, n = qh.shape[:3]
    nb = n // Q_BLOCK
    q_blocks = jnp.moveaxis(qh.reshape(bsz, H, nb, Q_BLOCK, 2, d), 2, 0)
    o = lax.map(lambda blk: attend(blk, k_all, v_all), q_blocks)
    o = jnp.moveaxis(o, 0, 2).reshape(bsz, H, n, 2 * d)

    def finish(o_):
        o_ = _rms(o_, subln) * (1.0 - lam_init)
        return o_.transpose(0, 2, 1, 3).reshape(o_.shape[0], o_.shape[2], H * 2 * d)
    out_l = finish(o)
    out_c = finish(attend(qch, kch, vch)) if need_ctx else None
    return out_l, out_c


def _rwkv7_scan(r, w, k, v, a, b, s0, reverse):
    xs = tuple(jnp.swapaxes(t, 0, 1) for t in (r, w, k, v, a, b))

    def step(s, inp):
        rt, wt, kt, vt, at, bt = inp
        sa = jnp.einsum('bhvk,bhk->bhv', s, at)
        s = s * wt[:, :, None, :] + sa[..., None] * bt[:, :, None, :] + vt[..., None] * kt[:, :, None, :]
        return s, jnp.einsum('bhvk,bhk->bhv', s, rt)
    s_fin, ys = lax.scan(step, s0, xs, reverse=reverse)
    return jnp.swapaxes(ys, 0, 1), s_fin


def _rwkv7_seq(z, s0, mu, w0, w2, a0, a2, g2, k_k, k_a, r_k, lnx_g, lnx_b, need_out):
    f32 = jnp.float32
    bsz, n, _ = z.shape
    H, N = RWKV_HEADS, RWKV_N
    z = (z + mu * _centred_shift(z)).astype(f32)
    r, k, v, gd, wd, ad = _split(z, (W_GROUP, W_GROUP, W_GROUP, RANK_G, RANK_W, RANK_A))

    def hd(t):
        return t.reshape(bsz, n, H, N)
    kk = hd(k * k_k)
    kk = kk / jnp.maximum(jnp.sqrt(jnp.sum(kk * kk, axis=-1, keepdims=True)), 1e-12)
    y, finals, kds = 0.0, [], []
    for d in range(2):
        wlog = -jax.nn.softplus(-(w0[d] + jnp.tanh(wd) @ w2[d])) - 0.5
        decay = jnp.exp(-jnp.exp(wlog))
        ag = jax.nn.sigmoid(a0[d] + ad @ a2[d])
        kd = hd(k * (1.0 + (ag - 1.0) * k_a))
        init = jnp.zeros((bsz, H, N, N), f32) if s0 is None else s0[d]
        yd, s_fin = _rwkv7_scan(hd(r), hd(decay), kd, hd(v), -kk, kk * hd(ag), init, d == 1)
        y = y + yd
        finals.append(s_fin)
        kds.append(kd)
    if not need_out:
        return None, finals
    mean = jnp.mean(y, axis=-1, keepdims=True)
    var = jnp.mean(jnp.square(y - mean), axis=-1, keepdims=True)
    y = ((y - mean) * lax.rsqrt(var + GN_EPS)).reshape(bsz, n, W_GROUP) * lnx_g + lnx_b
    rh, vh = hd(r), hd(v)
    bonus = sum(jnp.sum(rh * kd * r_k, axis=-1, keepdims=True) * vh for kd in kds)
    g = jax.nn.sigmoid(gd) @ g2
    return (y + bonus.reshape(bsz, n, W_GROUP)) * g, finals


def _hier_moe(t, w_rg, b_rg, w_re, b_re, w1, w3, w2):
    f32 = jnp.float32
    T, D = t.shape
    lg = (t @ w_rg).astype(f32) + b_rg
    gsel = jnp.argmax(lg, axis=-1)
    gate_g = jnp.max(jax.nn.softmax(lg, axis=-1), axis=-1)
    le = ((t @ w_re).astype(f32) + b_re).reshape(T, N_EGROUPS, EXP_PER_GROUP)
    le_sel = le[jnp.arange(T), gsel]
    top_p, top_i = lax.top_k(jax.nn.softmax(le_sel, axis=-1), TOP_K)
    wts = gate_g[:, None] * top_p / jnp.sum(top_p, axis=-1, keepdims=True)
    eid = gsel[:, None] * EXP_PER_GROUP + top_i
    A = T * TOP_K
    e_flat = eid.reshape(A).astype(jnp.int32)
    tok_flat = jnp.repeat(jnp.arange(T, dtype=jnp.int32), TOP_K)
    order = jnp.argsort(e_flat)
    se, stok, sw = e_flat[order], tok_flat[order], wts.reshape(A)[order]
    counts = jnp.bincount(e_flat, length=N_EXPERTS)
    start = jnp.cumsum(counts) - counts
    pcounts = ((counts + MOE_BLOCK - 1) // MOE_BLOCK) * MOE_BLOCK
    pend = jnp.cumsum(pcounts)
    pstart = pend - pcounts
    dest = pstart[se] + (jnp.arange(A, dtype=jnp.int32) - start[se])
    n_blocks = -(-A // MOE_BLOCK) + N_EXPERTS
    P = n_blocks * MOE_BLOCK
    slot_tok = jnp.full((P,), T, dtype=jnp.int32).at[dest].set(stok)
    t_pad = jnp.concatenate([t, jnp.zeros((1, D), t.dtype)], axis=0)
    xb = t_pad[slot_tok].reshape(n_blocks, MOE_BLOCK, D)
    block_e = jnp.minimum(jnp.searchsorted(pend, jnp.arange(n_blocks) * MOE_BLOCK, side='right'), N_EXPERTS - 1)

    def run(args):
        xblk, e = args
        hid = jax.nn.silu(xblk @ w1[e]) * (xblk @ w3[e])
        return hid @ w2[e]
    yb = lax.map(run, (xb, block_e)).reshape(P, D)
    y_assign = yb[dest] * sw[:, None].astype(yb.dtype)
    return jnp.zeros((T, D), t.dtype).at[stok].add(y_assign.astype(t.dtype))


def setup_inputs(seed: int = 0) -> dict:
    key = jax.random.key(seed)
    ks = iter(jax.random.split(key, 64))
    f32 = jnp.float32

    def nrm(shape, scale):
        return jax.random.normal(next(ks), shape, f32) * scale

    def unif(shape, lo, hi):
        return jax.random.uniform(next(ks), shape, f32, lo, hi)
    L, D, W = DEPTH, D_MODEL, W_GROUP
    G, P, HC = S5_GROUPS, S5_STATE, S5_CH
    inp = {}
    inp['x'] = nrm((BATCH, SEQ, D), 1.0)
    inp['c'] = nrm((BATCH, D), 1.0)
    inp['ctx'] = nrm((BATCH, CTX_LEN, D), 1.0)
    inp['c_ctx'] = nrm((D,), 1.0)
    inp['w_ada'] = nrm((L, D, 6 * D), 0.5 * D ** -0.5)
    inp['b_ada'] = nrm((L, 6 * D), 0.02)
    inp['g_norm1'] = 1.0 + nrm((L, D), 0.02)
    inp['g_norm2'] = 1.0 + nrm((L, D), 0.02)
    inp['w_in'] = nrm((L, D, D_IN), D ** -0.5)
    inp['w_out'] = nrm((L, D, D), D ** -0.5)
    n_idx = jnp.arange(P, dtype=f32)
    inp['s5_lam_re'] = -0.5 + nrm((L, 2, G, P), 0.01)
    inp['s5_lam_im'] = math.pi * n_idx + nrm((L, 2, G, P), 0.01)
    inp['s5_log_step'] = unif((L, 2, G), math.log(S5_DT_MIN), math.log(S5_DT_MAX))
    inp['s5_b_re'] = nrm((L, 2, G, P, HC), HC ** -0.5)
    inp['s5_b_im'] = nrm((L, 2, G, P, HC), HC ** -0.5)
    inp['s5_c_re'] = nrm((L, 2, G, HC, P), P ** -0.5)
    inp['s5_c_im'] = nrm((L, 2, G, HC, P), P ** -0.5)
    inp['s5_d'] = nrm((L, W), 0.5)
    inp['s5_w_glu'] = nrm((L, W, W), W ** -0.5)
    inp['s5_b_glu'] = nrm((L, W), 0.02)
    inp['lru_conv_w'] = nrm((L, LRU_CONV, W), LRU_CONV ** -0.5)
    inp['lru_conv_b'] = nrm((L, W), 0.02)
    a_c = unif((L, 2, W), 0.9, 0.999)
    s_a = a_c ** (1.0 / LRU_C)
    inp['lru_lam'] = jnp.log(s_a) - jnp.log1p(-s_a)
    inp['lru_wa'] = nrm((L, 2, LRU_BLOCKS, LRU_BLOCK, LRU_BLOCK), LRU_BLOCK ** -0.5)
    inp['lru_ba'] = nrm((L, 2, W), 0.02)
    inp['lru_wx'] = nrm((L, 2, LRU_BLOCKS, LRU_BLOCK, LRU_BLOCK), LRU_BLOCK ** -0.5)
    inp['lru_bx'] = nrm((L, 2, W), 0.02)
    inp['diff_gq'] = 1.0 + nrm((L, DIFF_D), 0.02)
    inp['diff_gk'] = 1.0 + nrm((L, DIFF_D), 0.02)
    inp['diff_lq1'] = nrm((L, DIFF_D), 0.1)
    inp['diff_lk1'] = nrm((L, DIFF_D), 0.1)
    inp['diff_lq2'] = nrm((L, DIFF_D), 0.1)
    inp['diff_lk2'] = nrm((L, DIFF_D), 0.1)
    inp['diff_subln'] = 1.0 + nrm((L, 2 * DIFF_D), 0.02)
    inp['rw_mu'] = unif((L, RWKV_COLS), 0.0, 1.0)
    inp['rw_w0'] = unif((L, 2, W), -5.0, 0.5)
    inp['rw_w2'] = nrm((L, 2, RANK_W, W), 0.1 * RANK_W ** -0.5)
    inp['rw_a0'] = nrm((L, 2, W), 0.1)
    inp['rw_a2'] = nrm((L, 2, RANK_A, W), RANK_A ** -0.5)
    inp['rw_g2'] = nrm((L, RANK_G, W), RANK_G ** -0.5)
    inp['rw_kk'] = 0.85 + nrm((L, W), 0.02)
    inp['rw_ka'] = 1.0 + nrm((L, W), 0.02)
    inp['rw_rk'] = nrm((L, RWKV_HEADS, RWKV_N), 0.1)
    inp['rw_lnx_g'] = 1.0 + nrm((L, W), 0.02)
    inp['rw_lnx_b'] = nrm((L, W), 0.02)
    inp['moe_w_rg'] = nrm((L, D, N_EGROUPS), D ** -0.5)
    inp['moe_b_rg'] = nrm((L, N_EGROUPS), 0.01)
    inp['moe_w_re'] = nrm((L, D, N_EXPERTS), D ** -0.5)
    inp['moe_b_re'] = nrm((L, N_EXPERTS), 0.01)
    inp['moe_w1'] = nrm((L, N_EXPERTS, D, D_EXPERT), D ** -0.5)
    inp['moe_w3'] = nrm((L, N_EXPERTS, D, D_EXPERT), D ** -0.5)
    inp['moe_w2'] = nrm((L, N_EXPERTS, D_EXPERT, D), D_EXPERT ** -0.5)
    return inp


def reference(x, c, ctx, c_ctx, w_ada, b_ada, g_norm1, g_norm2, w_in, w_out,
              s5_lam_re, s5_lam_im, s5_log_step, s5_b_re, s5_b_im, s5_c_re, s5_c_im, s5_d, s5_w_glu, s5_b_glu,
              lru_conv_w, lru_conv_b, lru_lam, lru_wa, lru_ba, lru_wx, lru_bx,
              diff_gq, diff_gk, diff_lq1, diff_lk1, diff_lq2, diff_lk2, diff_subln,
              rw_mu, rw_w0, rw_w2, rw_a0, rw_a2, rw_g2, rw_kk, rw_ka, rw_rk, rw_lnx_g, rw_lnx_b,
              moe_w_rg, moe_b_rg, moe_w_re, moe_b_re, moe_w1, moe_w3, moe_w2):
    f32 = jnp.float32
    bsz, n_tok, _ = x.shape
    rows = n_tok // GRID_W
    pos = jnp.arange(rows * GRID_W)
    row = (pos // GRID_W).astype(f32)
    col = (pos % GRID_W).astype(f32)
    inv_freq = ROPE_BASE ** (-jnp.arange(ROPE_AX, dtype=f32) / ROPE_AX)
    ang_r = row[:, None] * inv_freq
    ang_c = col[:, None] * inv_freq
    rope = (jnp.cos(ang_r)[:, None], jnp.sin(ang_r)[:, None],
            jnp.cos(ang_c)[:, None], jnp.sin(ang_c)[:, None])
    silu_c = jax.nn.silu(c.astype(f32))
    silu_cc = jax.nn.silu(c_ctx.astype(f32))
    xc = ctx
    for l in range(DEPTH):
        need_ctx = l < DEPTH - 1
        lam_init = 0.8 - 0.6 * math.exp(-0.3 * l)
        sh1, sc1, gt1, sh2, sc2, gt2 = [m[:, None] for m in jnp.split(silu_c @ w_ada[l] + b_ada[l], 6, axis=-1)]
        sh1c, sc1c, gt1c, sh2c, sc2c, gt2c = jnp.split(silu_cc @ w_ada[l] + b_ada[l], 6, axis=-1)
        h = _rms(x, g_norm1[l]) * (1.0 + sc1) + sh1
        hc = _rms(xc, g_norm1[l]) * (1.0 + sc1c) + sh1c
        z = _split(h @ w_in[l], IN_SIZES)
        zc = _split(hc @ w_in[l], IN_SIZES)
        ya, yac = _s5_mixer(z[0], zc[0], s5_lam_re[l], s5_lam_im[l], s5_log_step[l], s5_b_re[l], s5_b_im[l],
                            s5_c_re[l], s5_c_im[l], s5_d[l], s5_w_glu[l], s5_b_glu[l], need_ctx)
        yb, ybc = _rglru_mixer(z[1], z[2], zc[1], zc[2], lru_conv_w[l], lru_conv_b[l], lru_lam[l],
                               lru_wa[l], lru_ba[l], lru_wx[l], lru_bx[l], need_ctx)
        ydf, ydfc = _diff_attn(z[3], z[4], z[5], zc[3], zc[4], zc[5], rope, diff_gq[l], diff_gk[l],
                               diff_lq1[l], diff_lk1[l], diff_lq2[l], diff_lk2[l], diff_subln[l], lam_init, need_ctx)
        rw_args = (rw_mu[l], rw_w0[l], rw_w2[l], rw_a0[l], rw_a2[l], rw_g2[l], rw_kk[l], rw_ka[l],
                   rw_rk[l], rw_lnx_g[l], rw_lnx_b[l])
        yrc, ctx_states = _rwkv7_seq(zc[6], None, *rw_args, need_ctx)
        yr, _ = _rwkv7_seq(z[6], ctx_states, *rw_args, True)
        mix = jnp.concatenate([ya, yb, ydf, yr], axis=-1) @ w_out[l]
        x = x + (gt1 * mix).astype(x.dtype)
        if need_ctx:
            mixc = jnp.concatenate([yac, ybc, ydfc, yrc], axis=-1) @ w_out[l]
            xc = xc + (gt1c * mixc).astype(xc.dtype)
        h2 = _rms(x, g_norm2[l]) * (1.0 + sc2) + sh2
        moe_args = (moe_w_rg[l], moe_b_rg[l], moe_w_re[l], moe_b_re[l], moe_w1[l], moe_w3[l], moe_w2[l])
        n_lat = bsz * n_tok
        if need_ctx:
            h2c = _rms(xc, g_norm2[l]) * (1.0 + sc2c) + sh2c
            toks = jnp.concatenate([h2.reshape(n_lat, D_MODEL), h2c.reshape(-1, D_MODEL)], axis=0)
            mo = _hier_moe(toks, *moe_args)
            x = x + (gt2 * mo[:n_lat].reshape(bsz, n_tok, D_MODEL)).astype(x.dtype)
            xc = xc + (gt2c * mo[n_lat:].reshape(xc.shape)).astype(xc.dtype)
        else:
            mo = _hier_moe(h2.reshape(n_lat, D_MODEL), *moe_args)
            x = x + (gt2 * mo.reshape(bsz, n_tok, D_MODEL)).astype(x.dtype)
    return x
```

```python
import functools
import math

import jax
import jax.numpy as jnp
from jax import lax
from jax.experimental import pallas as pl
from jax.experimental.pallas import tpu as pltpu

F32 = jnp.float32
BF16 = jnp.bfloat16

D_MODEL = 2048
W_GROUP = 512
S5_CH, S5_GROUPS, S5_STATE = 16, 32, 64
S5_NS = S5_GROUPS * S5_STATE
LRU_C = 8.0
DIFF_HEADS, DIFF_D = 4, 64
ROPE_AX = 16
ROPE_BASE = 10000.0
RWKV_N, RWKV_HEADS = 64, 8
RANK_G, RANK_W, RANK_A = 64, 32, 32
RWKV_COLS = 3 * W_GROUP + RANK_G + RANK_W + RANK_A
N_EGROUPS, EXP_PER_GROUP, N_EXPERTS = 4, 8, 32
D_EXPERT = 1024
NORM_EPS = 1e-6
GN_EPS = 64e-5
GRID_W = 64

SUBLANES = 8
LANES = 128
ROW_TILE = 512
SEQ_TILE = 256
RW_CHUNK = 64
ATT_TQ = 256
ATT_TK = 512
MOE_BM = 256
OUT_TILE = 256
IN_TN = 1024
D_IN = 6 * W_GROUP + RWKV_COLS
D_IN_PAD = -(-D_IN // IN_TN) * IN_TN
COL_R, COL_K, COL_V = 6, 7, 8
COL_LOW = (9 * W_GROUP) // LANES
VMEM_LIMIT = 56 * 1024 * 1024


def _cparams(sem):
    return pltpu.CompilerParams(dimension_semantics=sem, vmem_limit_bytes=VMEM_LIMIT)


def _split2(x):
    hi = x.astype(BF16)
    lo = (x - hi.astype(F32)).astype(BF16)
    return hi, lo


def _dot(a, b):
    return jnp.dot(a, b, preferred_element_type=F32)


def _dot_nt(a, b):
    return lax.dot_general(a, b, (((1,), (1,)), ((), ())), preferred_element_type=F32)


def _dot_tn(a, b):
    return lax.dot_general(a, b, (((0,), (0,)), ((), ())), preferred_element_type=F32)


def _dot3(a, b_hi, b_lo):
    a_hi, a_lo = _split2(a)
    return _dot(a_hi, b_hi) + _dot(a_hi, b_lo) + _dot(a_lo, b_hi)


def _segsum(x, e_ref):
    hi, lo = _split2(x)
    e = e_ref[...]
    return _dot(hi, e) + _dot(lo, e)


def _seq_block(rev, b, i, n_c, n_l, n_b):
    if rev:
        ctx = n_b * n_l + b * n_c + (n_c - 1 - i)
        lat = b * n_l + (n_l - 1 - (i - n_c))
    else:
        ctx = n_b * n_l + b * n_c + i
        lat = b * n_l + (i - n_c)
    return jnp.where(i < n_c, ctx, lat)


def _seg_pos(t, n_b, n_l, n_c):
    is_lat = t < n_b * n_l
    p = jnp.where(is_lat, t % n_l, (t - n_b * n_l) % n_c)
    n = jnp.where(is_lat, n_l, n_c)
    return p, n


def _mod_body(c_ref, w_ref, b_ref, o_ref):
    c = c_ref[...]
    s = c * jax.nn.sigmoid(c)
    w_hi, w_lo = _split2(w_ref[...])
    o_ref[...] = _dot3(s, w_hi, w_lo) + b_ref[...]


def _modulation(cvec, w_ada, b_ada):
    d6 = w_ada.shape[1]
    tn = 1536
    return pl.pallas_call(
        _mod_body,
        grid=(d6 // tn,),
        in_specs=[pl.BlockSpec((SUBLANES, D_MODEL), lambda j: (0, 0)),
                  pl.BlockSpec((D_MODEL, tn), lambda j: (0, j)),
                  pl.BlockSpec((1, tn), lambda j: (0, j))],
        out_specs=pl.BlockSpec((SUBLANES, tn), lambda j: (0, j)),
        out_shape=jax.ShapeDtypeStruct((SUBLANES, d6), F32),
        compiler_params=_cparams(("arbitrary",)),
        name="adaln_mod",
    )(cvec, w_ada, b_ada.reshape(1, d6))


def _mod_row(i, tile, n_b, n_seq):
    return jnp.where(i < n_b * n_seq // tile, 1 + i // (n_seq // tile), 0)


def _inproj_body(x_ref, mod_ref, g_ref, w_ref, z_ref, h_ref):
    @pl.when(pl.program_id(1) == 0)
    def _():
        x = x_ref[...]
        ms = jnp.mean(x * x, axis=-1, keepdims=True)
        xn = x * lax.rsqrt(ms + NORM_EPS) * g_ref[...]
        h_ref[...] = (xn * (1.0 + mod_ref[0, 1:2, :]) + mod_ref[0, 0:1, :]).astype(BF16)

    z_ref[...] = _dot(h_ref[...], w_ref[...])


def _inproj(x, mod3, g1, w, n_b, n_seq):
    rows = x.shape[0]
    tn = IN_TN
    return pl.pallas_call(
        _inproj_body,
        grid=(rows // ROW_TILE, w.shape[1] // tn),
        in_specs=[pl.BlockSpec((ROW_TILE, D_MODEL), lambda i, j: (i, 0)),
                  pl.BlockSpec((1, 6, D_MODEL), lambda i, j: (_mod_row(i, ROW_TILE, n_b, n_seq), 0, 0)),
                  pl.BlockSpec((1, D_MODEL), lambda i, j: (0, 0)),
                  pl.BlockSpec((D_MODEL, tn), lambda i, j: (0, j))],
        out_specs=pl.BlockSpec((ROW_TILE, tn), lambda i, j: (i, j)),
        out_shape=jax.ShapeDtypeStruct((rows, w.shape[1]), F32),
        scratch_shapes=[pltpu.VMEM((ROW_TILE, D_MODEL), BF16)],
        compiler_params=_cparams(("arbitrary", "arbitrary")),
        name="norm_inproj",
    )(x, mod3, g1.reshape(1, D_MODEL), w)


def _s5_body(rev, u_ref, bre_ref, bim_ref, t1r_ref, t1i_ref, p2r_ref, p2i_ref, cre_ref, cim_ref,
             y_ref, sr_ref, si_ref, car_ref):
    g_n = SEQ_TILE // SUBLANES

    @pl.when(pl.program_id(1) == 0)
    def _():
        car_ref[...] = jnp.zeros_like(car_ref)

    u = u_ref[...].astype(BF16)
    xr = _dot(u, bre_ref[0]).reshape(g_n, SUBLANES, S5_NS)
    xi = _dot(u, bim_ref[0]).reshape(g_n, SUBLANES, S5_NS)
    for lvl, s in enumerate((1, 2, 4)):
        sh = (SUBLANES - s) if rev else s
        pr = pltpu.roll(xr, sh, axis=1)
        pi = pltpu.roll(xi, sh, axis=1)
        ar = t1r_ref[0, lvl][None]
        ai = t1i_ref[0, lvl][None]
        xr, xi = xr + ar * pr - ai * pi, xi + ar * pi + ai * pr
    sr_ref[...] = xr
    si_ref[...] = xi
    p2r = p2r_ref[0]
    p2i = p2i_ref[0]
    row = 0 if rev else SUBLANES - 1

    def step(g, _):
        gg = (g_n - 1 - g) if rev else g
        cr = car_ref[0]
        ci = car_ref[1]
        nr = sr_ref[gg] + p2r * cr - p2i * ci
        ni = si_ref[gg] + p2r * ci + p2i * cr
        sr_ref[gg] = nr
        si_ref[gg] = ni
        car_ref[0] = jnp.broadcast_to(nr[row:row + 1, :], (SUBLANES, S5_NS))
        car_ref[1] = jnp.broadcast_to(ni[row:row + 1, :], (SUBLANES, S5_NS))
        return 0

    lax.fori_loop(0, g_n, step, 0)
    s_r = sr_ref[...].reshape(SEQ_TILE, S5_NS).astype(BF16)
    s_i = si_ref[...].reshape(SEQ_TILE, S5_NS).astype(BF16)
    y_ref[...] = _dot(s_r, cre_ref[0]) - _dot(s_i, cim_ref[0])


def _s5_scan(za, p, d, n_b, n_l, n_c):
    rev = d == 1
    rows = za.shape[0]
    seq = lambda b, i: (_seq_block(rev, b, i, n_c, n_l, n_b), 0)
    cst3 = lambda b, i: (d, 0, 0)
    cst4 = lambda b, i: (d, 0, 0, 0)
    return pl.pallas_call(
        functools.partial(_s5_body, rev),
        grid=(n_b, n_c + n_l),
        in_specs=[pl.BlockSpec((SEQ_TILE, W_GROUP), seq),
                  pl.BlockSpec((1, W_GROUP, S5_NS), cst3),
                  pl.BlockSpec((1, W_GROUP, S5_NS), cst3),
                  pl.BlockSpec((1, 3, SUBLANES, S5_NS), cst4),
                  pl.BlockSpec((1, 3, SUBLANES, S5_NS), cst4),
                  pl.BlockSpec((1, SUBLANES, S5_NS), cst3),
                  pl.BlockSpec((1, SUBLANES, S5_NS), cst3),
                  pl.BlockSpec((1, S5_NS, W_GROUP), cst3),
                  pl.BlockSpec((1, S5_NS, W_GROUP), cst3)],
        out_specs=pl.BlockSpec((SEQ_TILE, W_GROUP), seq),
        out_shape=jax.ShapeDtypeStruct((rows, W_GROUP), F32),
        scratch_shapes=[pltpu.VMEM((SEQ_TILE // SUBLANES, SUBLANES, S5_NS), F32),
                        pltpu.VMEM((SEQ_TILE // SUBLANES, SUBLANES, S5_NS), F32),
                        pltpu.VMEM((2, SUBLANES, S5_NS), F32)],
        compiler_params=_cparams(("arbitrary", "arbitrary")),
        name="s5_scan_rev" if rev else "s5_scan_fwd",
    )(za, p["bre"], p["bim"], p["t1r"], p["t1i"], p["p2r"], p["p2i"], p["cre"], p["cim"])


def _s5_prepare(lam_re, lam_im, log_step, b_re, b_im, c_re, c_im):
    lr, li = lam_re.astype(F32), lam_im.astype(F32)
    dt = jnp.exp(log_step.astype(F32))[..., None]
    mag = jnp.exp(lr * dt)
    ar, ai = mag * jnp.cos(li * dt), mag * jnp.sin(li * dt)
    den = lr * lr + li * li
    cr = ((ar - 1.0) * lr + ai * li) / den
    ci = (ai * lr - (ar - 1.0) * li) / den
    bbr = cr[..., None] * b_re - ci[..., None] * b_im
    bbi = cr[..., None] * b_im + ci[..., None] * b_re
    eye = jnp.eye(S5_GROUPS, dtype=F32)
    bd = lambda t: jnp.einsum('dgph,gk->dghkp', t, eye).reshape(2, W_GROUP, S5_NS).astype(BF16)
    cd = lambda t: jnp.einsum('dghp,gk->dgpkh', t.astype(F32), eye).reshape(2, S5_NS, W_GROUP).astype(BF16)
    ar = ar.reshape(2, S5_NS)
    ai = ai.reshape(2, S5_NS)
    pw_r, pw_i = [ar], [ai]
    for _ in range(SUBLANES - 1):
        pw_r.append(pw_r[-1] * ar - pw_i[-1] * ai)
        pw_i.append(pw_r[-2] * ai + pw_i[-1] * ar)
    k = jnp.arange(SUBLANES)[None, :, None]
    t1r, t1i, p2r, p2i = [], [], [], []
    for d in range(2):
        lr_, li_ = [], []
        for s in (1, 2, 4):
            m = (k + s <= SUBLANES - 1) if d == 1 else (k >= s)
            lr_.append(jnp.where(m, pw_r[s - 1][d][None, None, :], 0.0)[0])
            li_.append(jnp.where(m, pw_i[s - 1][d][None, None, :], 0.0)[0])
        t1r.append(jnp.stack(lr_))
        t1i.append(jnp.stack(li_))
        order = range(SUBLANES - 1, -1, -1) if d == 1 else range(SUBLANES)
        p2r.append(jnp.stack([pw_r[j][d] for j in order]))
        p2i.append(jnp.stack([pw_i[j][d] for j in order]))
    return dict(bre=bd(bbr), bim=bd(bbi), cre=cd(c_re), cim=cd(c_im),
                t1r=jnp.stack(t1r), t1i=jnp.stack(t1i), p2r=jnp.stack(p2r), p2i=jnp.stack(p2i))


def _s5_finish_body(yf_ref, yb_ref, u_ref, d_ref, w_ref, b_ref, o_ref):
    y = jax.nn.gelu(yf_ref[...] + yb_ref[...] + d_ref[...] * u_ref[...])
    gate = jax.nn.sigmoid(_dot(y.astype(BF16), w_ref[...]) + b_ref[...])
    o_ref[...] = (y * gate).astype(BF16)


def _s5_finish(yf, yb, za, d_skip, w_glu, b_glu):
    rows = yf.shape[0]
    blk = pl.BlockSpec((ROW_TILE, W_GROUP), lambda i: (i, 0))
    vec = pl.BlockSpec((1, W_GROUP), lambda i: (0, 0))
    return pl.pallas_call(
        _s5_finish_body,
        grid=(rows // ROW_TILE,),
        in_specs=[blk, blk, blk, vec, pl.BlockSpec((W_GROUP, W_GROUP), lambda i: (0, 0)), vec],
        out_specs=blk,
        out_shape=jax.ShapeDtypeStruct((rows, W_GROUP), BF16),
        compiler_params=_cparams(("arbitrary",)),
        name="s5_finish",
    )(yf, yb, za, d_skip.reshape(1, W_GROUP), w_glu.astype(BF16), b_glu.reshape(1, W_GROUP))


HALO = SUBLANES


def _halo_maps(tile_fn, rows, col):
    per = SEQ_TILE // HALO
    last = rows // HALO - 1
    prv = lambda *ids: (jnp.maximum(tile_fn(*ids) * per - 1, 0), col)
    nxt = lambda *ids: (jnp.minimum((tile_fn(*ids) + 1) * per, last), col)
    return prv, nxt


def _lru_body(rev, n_b, n_l, n_c, xp_ref, xc_ref, xn_ref, cw_ref, cb_ref, wa_ref, ba_ref, wx_ref, bx_ref,
              sp_ref, h_ref, buf_ref, a_ref, b_ref, car_ref):
    g_n = SEQ_TILE // SUBLANES
    b = pl.program_id(0)
    i = pl.program_id(1)

    @pl.when(i == 0)
    def _():
        car_ref[...] = jnp.zeros_like(car_ref)

    p, n = _seg_pos(_seq_block(rev, b, i, n_c, n_l, n_b), n_b, n_l, n_c)
    prev_ok = (p > 0).astype(F32)
    next_ok = (p < n - 1).astype(F32)
    buf_ref[0:HALO, :] = xp_ref[...] * prev_ok
    buf_ref[HALO:HALO + SEQ_TILE, :] = xc_ref[...]
    buf_ref[HALO + SEQ_TILE:, :] = xn_ref[...] * next_ok
    v = cb_ref[...] + cw_ref[2:3, :] * xc_ref[...]
    v = v + cw_ref[0:1, :] * buf_ref[HALO - 2:HALO - 2 + SEQ_TILE, :]
    v = v + cw_ref[1:2, :] * buf_ref[HALO - 1:HALO - 1 + SEQ_TILE, :]
    v = v + cw_ref[3:4, :] * buf_ref[HALO + 1:HALO + 1 + SEQ_TILE, :]
    vb = v.astype(BF16)
    r = jax.nn.sigmoid(_dot(vb, wa_ref[0]) + ba_ref[0])
    ig = jax.nn.sigmoid(_dot(vb, wx_ref[0]) + bx_ref[0])
    a = jnp.exp(-LRU_C * r * sp_ref[0])
    bb = jnp.sqrt(1.0 - a * a) * (ig * v)
    a3 = a.reshape(g_n, SUBLANES, W_GROUP)
    b3 = bb.reshape(g_n, SUBLANES, W_GROUP)
    k = lax.broadcasted_iota(jnp.int32, (g_n, SUBLANES, W_GROUP), 1)
    for s in (1, 2, 4):
        sh = (SUBLANES - s) if rev else s
        m = (k + s <= SUBLANES - 1) if rev else (k >= s)
        a_s = pltpu.roll(a3, sh, axis=1)
        b_s = pltpu.roll(b3, sh, axis=1)
        b3 = jnp.where(m, a3 * b_s + b3, b3)
        a3 = jnp.where(m, a3 * a_s, a3)
    a_ref[...] = a3
    b_ref[...] = b3
    row = 0 if rev else SUBLANES - 1

    def step(g, _):
        gg = (g_n - 1 - g) if rev else g
        hh = b_ref[gg] + a_ref[gg] * car_ref[...]
        b_ref[gg] = hh
        car_ref[...] = jnp.broadcast_to(hh[row:row + 1, :], (SUBLANES, W_GROUP))
        return 0

    lax.fori_loop(0, g_n, step, 0)
    h_ref[...] = b_ref[...].reshape(SEQ_TILE, W_GROUP)


def _lru_scan(za, p, d, n_b, n_l, n_c):
    rev = d == 1
    rows = za.shape[0]
    tile = lambda b, i: _seq_block(rev, b, i, n_c, n_l, n_b)
    cur = lambda b, i: (tile(b, i), 1)
    prv, nxt = _halo_maps(tile, rows, 1)
    out = lambda b, i: (tile(b, i), 0)
    vec = pl.BlockSpec((1, W_GROUP), lambda b, i: (0, 0))
    dvec = pl.BlockSpec((1, 1, W_GROUP), lambda b, i: (d, 0, 0))
    dmat = pl.BlockSpec((1, W_GROUP, W_GROUP), lambda b, i: (d, 0, 0))
    blk = lambda f: pl.BlockSpec((SEQ_TILE, W_GROUP), f)
    halo = lambda f: pl.BlockSpec((HALO, W_GROUP), f)
    g_n = SEQ_TILE // SUBLANES
    return pl.pallas_call(
        functools.partial(_lru_body, rev, n_b, n_l, n_c),
        grid=(n_b, n_c + n_l),
        in_specs=[halo(prv), blk(cur), halo(nxt),
                  pl.BlockSpec((4, W_GROUP), lambda b, i: (0, 0)), vec,
                  dmat, dvec, dmat, dvec, dvec],
        out_specs=blk(out),
        out_shape=jax.ShapeDtypeStruct((rows, W_GROUP), F32),
        scratch_shapes=[pltpu.VMEM((SEQ_TILE + 2 * HALO, W_GROUP), F32),
                        pltpu.VMEM((g_n, SUBLANES, W_GROUP), F32),
                        pltpu.VMEM((g_n, SUBLANES, W_GROUP), F32),
                        pltpu.VMEM((SUBLANES, W_GROUP), F32)],
        compiler_params=_cparams(("arbitrary", "arbitrary")),
        name="lru_scan_rev" if rev else "lru_scan_fwd",
    )(za, za, za, p["cw"], p["cb"], p["wa"], p["ba"], p["wx"], p["bx"], p["sp"])


def _blockdiag(w):
    nb, c = w.shape[1], w.shape[2]
    eye = jnp.eye(nb, dtype=F32)
    return jnp.einsum('dncf,nm->dncmf', w.astype(F32), eye).reshape(2, nb * c, nb * c)


def _lru_finish_body(hf_ref, hb_ref, g_ref, o_ref):
    o_ref[...] = ((hf_ref[...] + hb_ref[...]) * jax.nn.gelu(g_ref[...])).astype(BF16)


def _lru_finish(hf, hb, za):
    rows = hf.shape[0]
    blk = pl.BlockSpec((ROW_TILE, W_GROUP), lambda i: (i, 0))
    return pl.pallas_call(
        _lru_finish_body,
        grid=(rows // ROW_TILE,),
        in_specs=[blk, blk, pl.BlockSpec((ROW_TILE, W_GROUP), lambda i: (i, 2))],
        out_specs=blk,
        out_shape=jax.ShapeDtypeStruct((rows, W_GROUP), BF16),
        compiler_params=_cparams(("arbitrary",)),
        name="lru_finish",
    )(hf, hb, za)


def _qkv_prep_body(q_ref, k_ref, v_ref, cos_ref, s1_ref, s2_ref, gq_ref, gk_ref, e_ref, qo_ref, ko_ref, vo_ref):
    reps = W_GROUP // LANES
    cos = jnp.concatenate([cos_ref[...]] * reps, axis=1)
    s1 = jnp.concatenate([s1_ref[...]] * reps, axis=1)
    s2 = jnp.concatenate([s2_ref[...]] * reps, axis=1)

    def prep(x, g):
        ms = _segsum(x * x, e_ref) * (1.0 / DIFF_D)
        x = x * lax.rsqrt(ms + NORM_EPS) * g
        return x * cos + pltpu.roll(x, ROPE_AX, axis=1) * s1 + pltpu.roll(x, W_GROUP - ROPE_AX, axis=1) * s2

    qo_ref[...] = (prep(q_ref[...], gq_ref[...]) * (DIFF_D ** -0.5)).astype(BF16)
    ko_ref[...] = prep(k_ref[...], gk_ref[...]).astype(BF16)
    vo_ref[...] = v_ref[...].astype(BF16)


def _qkv_prep(za, rope, gq, gk, e_seg):
    rows = za.shape[0]
    col = lambda c: pl.BlockSpec((ROW_TILE, W_GROUP), lambda i: (i, c))
    tab = pl.BlockSpec((ROW_TILE, LANES), lambda i: (i, 0))
    vec = pl.BlockSpec((1, W_GROUP), lambda i: (0, 0))
    out = pl.BlockSpec((ROW_TILE, W_GROUP), lambda i: (i, 0))
    reps = W_GROUP // DIFF_D
    return pl.pallas_call(
        _qkv_prep_body,
        grid=(rows // ROW_TILE,),
        in_specs=[col(3), col(4), col(5), tab, tab, tab, vec, vec,
                  pl.BlockSpec((W_GROUP, W_GROUP), lambda i: (0, 0))],
        out_specs=[out, out, out],
        out_shape=[jax.ShapeDtypeStruct((rows, W_GROUP), BF16)] * 3,
        compiler_params=_cparams(("arbitrary",)),
        name="attn_qkv_prep",
    )(za, za, za, rope[0], rope[1], rope[2], jnp.tile(gq, reps).reshape(1, W_GROUP),
      jnp.tile(gk, reps).reshape(1, W_GROUP), e_seg)


def _attn_body(seg_lens, out_scale, lam_ref, sub_ref, q_ref, *rest):
    o_ref, m_ref, l_ref, acc_ref = rest[-4:]
    kv = rest[:-4]
    tq = q_ref.shape[0]
    q = q_ref[...]
    lane = lax.broadcasted_iota(jnp.int32, (tq, LANES), 1)
    zero = jnp.zeros_like(q)
    q2 = jnp.concatenate([jnp.where(lane < DIFF_D, q, zero), jnp.where(lane < DIFF_D, zero, q)], axis=0)
    m_ref[...] = jnp.full_like(m_ref, -jnp.inf)
    l_ref[...] = jnp.zeros_like(l_ref)
    acc_ref[...] = jnp.zeros_like(acc_ref)

    def update(kc, vc):
        s = _dot_nt(q2, kc)
        m_old = m_ref[...]
        m_new = jnp.maximum(m_old, jnp.max(s, axis=-1, keepdims=True))
        alpha = jnp.exp(m_old - m_new)
        pm = jnp.exp(s - m_new)
        l_ref[...] = alpha * l_ref[...] + jnp.sum(pm, axis=-1, keepdims=True)
        acc_ref[...] = alpha * acc_ref[...] + _dot(pm.astype(BF16), vc)
        m_ref[...] = m_new

    for si, n_k in enumerate(seg_lens):
        k_ref, v_ref = kv[2 * si], kv[2 * si + 1]
        tk = min(ATT_TK, n_k)

        def body(j, _, k_ref=k_ref, v_ref=v_ref, tk=tk):
            off = pl.multiple_of(j * tk, tk)
            update(k_ref[pl.ds(off, tk), :], v_ref[pl.ds(off, tk), :])
            return 0

        lax.fori_loop(0, n_k // tk, body, 0)

    o = acc_ref[...] / l_ref[...]
    o = o[:tq] - lam_ref[...] * o[tq:]
    ms = jnp.mean(o * o, axis=-1, keepdims=True)
    o_ref[...] = (o * lax.rsqrt(ms + NORM_EPS) * sub_ref[...] * out_scale).astype(BF16)


def _attention(qn, kn, vn, lam_vec, subln, lam_init, n_b, n_seq, n_ctx):
    vec = lambda nd: pl.BlockSpec((1, LANES), (lambda b, h, i: (0, 0)) if nd == 3 else (lambda b, h: (0, 0)))
    scratch = lambda tq: [pltpu.VMEM((2 * tq, 1), F32), pltpu.VMEM((2 * tq, 1), F32),
                          pltpu.VMEM((2 * tq, LANES), F32)]
    sub = subln.reshape(1, LANES)
    ctx_blk0 = n_b * n_seq // n_ctx
    qpb = n_seq // ATT_TQ
    lat_kv = pl.BlockSpec((n_seq, LANES), lambda b, h, i: (b, h))
    ctx_kv = pl.BlockSpec((n_ctx, LANES), lambda b, h, i: (ctx_blk0 + b, h))
    o_lat = pl.pallas_call(
        functools.partial(_attn_body, (n_seq, n_ctx), 1.0 - lam_init),
        grid=(n_b, DIFF_HEADS, qpb),
        in_specs=[vec(3), vec(3), pl.BlockSpec((ATT_TQ, LANES), lambda b, h, i: (b * qpb + i, h)),
                  lat_kv, lat_kv, ctx_kv, ctx_kv],
        out_specs=pl.BlockSpec((ATT_TQ, LANES), lambda b, h, i: (b * qpb + i, h)),
        out_shape=jax.ShapeDtypeStruct((n_b * n_seq, W_GROUP), BF16),
        scratch_shapes=scratch(ATT_TQ),
        compiler_params=_cparams(("arbitrary", "arbitrary", "arbitrary")),
        name="diff_attn_latent",
    )(lam_vec, sub, qn, kn, vn, kn, vn)
    ckv = pl.BlockSpec((n_ctx, LANES), lambda b, h: (ctx_blk0 + b, h))
    o_ctx = pl.pallas_call(
        functools.partial(_attn_body, (n_ctx,), 1.0 - lam_init),
        grid=(n_b, DIFF_HEADS),
        in_specs=[vec(2), vec(2), ckv, ckv, ckv],
        out_specs=pl.BlockSpec((n_ctx, LANES), lambda b, h: (b, h)),
        out_shape=jax.ShapeDtypeStruct((n_b * n_ctx, W_GROUP), BF16),
        scratch_shapes=scratch(n_ctx),
        compiler_params=_cparams(("arbitrary", "arbitrary")),
        name="diff_attn_context",
    )(lam_vec, sub, qn, kn, vn)
    return o_lat, o_ctx


def _rope_tables(n_b, n_seq, n_ctx):
    pos = jnp.arange(n_seq)
    row = (pos // GRID_W).astype(F32)
    col = (pos % GRID_W).astype(F32)
    inv_freq = ROPE_BASE ** (-jnp.arange(ROPE_AX, dtype=F32) / ROPE_AX)
    ang_r, ang_c = row[:, None] * inv_freq, col[:, None] * inv_freq
    z = jnp.zeros_like(ang_r)
    cos = jnp.concatenate([jnp.cos(ang_r)] * 2 + [jnp.cos(ang_c)] * 2, axis=1)
    s1 = jnp.concatenate([z, jnp.sin(ang_r), z, jnp.sin(ang_c)], axis=1)
    s2 = jnp.concatenate([-jnp.sin(ang_r), z, -jnp.sin(ang_c), z], axis=1)

    def full(t, ctx_val):
        t = jnp.tile(jnp.tile(t, (1, LANES // DIFF_D)), (n_b, 1))
        return jnp.concatenate([t, jnp.full((n_b * n_ctx, LANES), ctx_val, F32)], axis=0)
    return full(cos, 1.0), full(s1, 0.0), full(s2, 0.0)


def _rwkv_prep_body(n_b, n_l, n_c,
                    rp_ref, rc_ref, rn_ref, kp_ref, kc_ref, kn_ref, vp_ref, vc_ref, vn_ref, lp_ref, lc_ref, ln_ref,
                    mu_ref, w2_ref, a2_ref, g2_ref, w0_ref, a0_ref, kk_w_ref, ka_ref, rk_ref, e_ref,
                    r_ref, v_ref, kk_ref, lw0_ref, kd0_ref, b0_ref, lw1_ref, kd1_ref, b1_ref, bon_ref, g_ref,
                    buf_ref, bufl_ref):
    t = pl.program_id(0)
    p, n = _seg_pos(t, n_b, n_l, n_c)
    prev_ok = (p > 0).astype(F32)
    next_ok = (p < n - 1).astype(F32)

    def shifted(zp_ref, zc_ref, zn_ref, buf, mu):
        buf[0:HALO, :] = zp_ref[...] * prev_ok
        buf[HALO:HALO + SEQ_TILE, :] = zc_ref[...]
        buf[HALO + SEQ_TILE:, :] = zn_ref[...] * next_ok
        z = zc_ref[...]
        return z + mu * (0.5 * (buf[HALO - 1:HALO - 1 + SEQ_TILE, :] + buf[HALO + 1:HALO + 1 + SEQ_TILE, :]) - z)

    r = shifted(rp_ref, rc_ref, rn_ref, buf_ref, mu_ref[:, 0:W_GROUP])
    k = shifted(kp_ref, kc_ref, kn_ref, buf_ref, mu_ref[:, W_GROUP:2 * W_GROUP])
    v = shifted(vp_ref, vc_ref, vn_ref, buf_ref, mu_ref[:, 2 * W_GROUP:3 * W_GROUP])
    low = shifted(lp_ref, lc_ref, ln_ref, bufl_ref, mu_ref[:, 3 * W_GROUP:])
    kk = k * kk_w_ref[...]
    ss = _segsum(kk * kk, e_ref)
    kk = kk / jnp.maximum(jnp.sqrt(ss), 1e-12)
    tw = jnp.tanh(low).astype(BF16)
    lb = low.astype(BF16)
    r_ref[...] = r
    v_ref[...] = v
    kk_ref[...] = kk
    ksum = jnp.zeros_like(k)
    for d, (lw_ref, kd_ref, b_ref) in enumerate(((lw0_ref, kd0_ref, b0_ref), (lw1_ref, kd1_ref, b1_ref))):
        y = w0_ref[d:d + 1, :] + _dot(tw, w2_ref[d])
        y = -y
        softplus = jnp.maximum(y, 0.0) + jnp.log(1.0 + jnp.exp(-jnp.abs(y)))
        lw_ref[...] = -jnp.exp(-softplus - 0.5)
        ag = jax.nn.sigmoid(a0_ref[d:d + 1, :] + _dot(lb, a2_ref[d]))
        kd = k * (1.0 + (ag - 1.0) * ka_ref[...])
        kd_ref[...] = kd
        b_ref[...] = kk * ag
        ksum = ksum + kd
    bon_ref[...] = _segsum(r * ksum * rk_ref[...], e_ref) * v
    g_ref[...] = _dot(jax.nn.sigmoid(low).astype(BF16), g2_ref[...])


def _rwkv_prep(z, p, e_seg, n_b, n_l, n_c):
    rows = z.shape[0]
    vec = pl.BlockSpec((1, W_GROUP), lambda t: (0, 0))
    two = pl.BlockSpec((2, W_GROUP), lambda t: (0, 0))
    lowm = pl.BlockSpec((2, LANES, W_GROUP), lambda t: (0, 0, 0))
    out = pl.BlockSpec((SEQ_TILE, W_GROUP), lambda t: (t, 0))
    z_specs = []
    for col, width in ((COL_R, W_GROUP), (COL_K, W_GROUP), (COL_V, W_GROUP), (COL_LOW, LANES)):
        prv, nxt = _halo_maps(lambda t: t, rows, col)
        z_specs += [pl.BlockSpec((HALO, width), prv),
                    pl.BlockSpec((SEQ_TILE, width), lambda t, col=col: (t, col)),
                    pl.BlockSpec((HALO, width), nxt)]
    return pl.pallas_call(
        functools.partial(_rwkv_prep_body, n_b, n_l, n_c),
        grid=(rows // SEQ_TILE,),
        in_specs=z_specs + [pl.BlockSpec((1, RWKV_COLS), lambda t: (0, 0)), lowm, lowm,
                            pl.BlockSpec((LANES, W_GROUP), lambda t: (0, 0)), two, two, vec, vec, vec,
                            pl.BlockSpec((W_GROUP, W_GROUP), lambda t: (0, 0))],
        out_specs=[out] * 11,
        out_shape=[jax.ShapeDtypeStruct((rows, W_GROUP), F32)] * 11,
        scratch_shapes=[pltpu.VMEM((SEQ_TILE + 2 * HALO, W_GROUP), F32),
                        pltpu.VMEM((SEQ_TILE + 2 * HALO, LANES), F32)],
        compiler_params=_cparams(("arbitrary",)),
        name="rwkv_prep",
    )(*([z] * 12), p["mu"], p["w2"], p["a2"], p["g2"], p["w0"], p["a0"], p["kk"], p["ka"], p["rk"], e_seg)


def _rwkv_scan_body(rev, r_ref, v_ref, kk_ref, lw_ref, kd_ref, b_ref, y_ref, st_ref):
    c = RW_CHUNK

    @pl.when(pl.program_id(1) == 0)
    def _():
        st_ref[...] = jnp.zeros_like(st_ref)

    rt = lax.broadcasted_iota(jnp.int32, (c, c), 0)
    ct = lax.broadcasted_iota(jnp.int32, (c, c), 1)
    tri = ((rt <= ct) if rev else (rt >= ct)).astype(BF16)
    lw = lw_ref[...]
    hi = lw.astype(BF16)
    r1 = lw - hi.astype(F32)
    mid = r1.astype(BF16)
    lo = (r1 - mid.astype(F32)).astype(BF16)
    cum = _dot(tri, hi) + _dot(tri, mid) + _dot(tri, lo)
    last = 0 if rev else c - 1
    tot = cum[last:last + 1, :]
    g_end = jnp.exp(tot - cum)
    g_inv = jnp.exp(-cum)
    a_t = -kk_ref[...] * jnp.exp(cum - lw)
    r_t = r_ref[...] * jnp.exp(cum)
    k_t = kd_ref[...] * g_inv
    b_t = b_ref[...] * g_inv
    k_g = kd_ref[...] * g_end
    b_g = b_ref[...] * g_end
    g_tot = jnp.exp(tot)
    v = v_ref[...]

    lane = lax.broadcasted_iota(jnp.int32, (c, LANES), 1)
    head0 = lane < RWKV_N

    def stack(x):
        z = jnp.zeros_like(x)
        return jnp.concatenate([jnp.where(head0, x, z), jnp.where(head0, z, x)], axis=0).astype(BF16)

    rr = lax.broadcasted_iota(jnp.int32, (2 * c, 2 * c), 0)
    cc = lax.broadcasted_iota(jnp.int32, (2 * c, 2 * c), 1)
    same = (rr >= c) == (cc >= c)
    tt = rr & (c - 1)
    ss = cc & (c - 1)
    strict = same & ((tt < ss) if rev else (tt > ss))
    incl = same & ((tt <= ss) if rev else (tt >= ss))
    eye = (rr == cc).astype(F32)

    for p in range(RWKV_HEADS // 2):
        sl = slice(p * LANES, (p + 1) * LANES)
        am, rm, km, bm = stack(a_t[:, sl]), stack(r_t[:, sl]), stack(k_t[:, sl]), stack(b_t[:, sl])
        vm, kg, bg = stack(v[:, sl]), stack(k_g[:, sl]), stack(b_g[:, sl])
        low = jnp.where(strict, _dot_nt(am, bm), 0.0)
        a_k = jnp.where(strict, _dot_nt(am, km), 0.0)
        r_k = jnp.where(incl, _dot_nt(rm, km), 0.0)
        r_b = jnp.where(incl, _dot_nt(rm, bm), 0.0)
        st = st_ref[p]
        stb = st.astype(BF16)
        rhs = _dot_nt(am, stb) + _dot(a_k.astype(BF16), vm)
        x = eye + low
        pw = low
        for _ in range(int(math.log2(c)) - 1):
            pwb = pw.astype(BF16)
            pw = _dot(pwb, pwb)
            x = x + _dot(x.astype(BF16), pw.astype(BF16))
        u = _dot(x.astype(BF16), rhs.astype(BF16)).astype(BF16)
        y = _dot_nt(rm, stb) + _dot(r_k.astype(BF16), vm) + _dot(r_b.astype(BF16), u)
        y_ref[:, sl] = y[:c] + y[c:]
        st_ref[p] = st * g_tot[:, sl] + _dot_tn(vm, kg) + _dot_tn(u, bg)


def _rwkv_scan(pre, d, n_b, n_l, n_c):
    rev = d == 1
    r, v, kk = pre[0], pre[1], pre[2]
    lw, kd, bb = pre[3 + 3 * d], pre[4 + 3 * d], pre[5 + 3 * d]
    rows = r.shape[0]
    blk = pl.BlockSpec((RW_CHUNK, W_GROUP), lambda b, i: (_seq_block(rev, b, i, n_c, n_l, n_b), 0))
    return pl.pallas_call(
        functools.partial(_rwkv_scan_body, rev),
        grid=(n_b, n_c + n_l),
        in_specs=[blk] * 6,
        out_specs=blk,
        out_shape=jax.ShapeDtypeStruct((rows, W_GROUP), F32),
        scratch_shapes=[pltpu.VMEM((RWKV_HEADS // 2, LANES, LANES), F32)],
        compiler_params=_cparams(("arbitrary", "arbitrary")),
        name="rwkv_scan_rev" if rev else "rwkv_scan_fwd",
    )(r, v, kk, lw, kd, bb)


def _rwkv_finish_body(yf_ref, yb_ref, bon_ref, g_ref, lg_ref, lb_ref, e_ref, o_ref):
    y = yf_ref[...] + yb_ref[...]
    dlt = y - _segsum(y, e_ref) * (1.0 / RWKV_N)
    var = _segsum(dlt * dlt, e_ref) * (1.0 / RWKV_N)
    yn = dlt * lax.rsqrt(var + GN_EPS) * lg_ref[...] + lb_ref[...]
    o_ref[...] = ((yn + bon_ref[...]) * g_ref[...]).astype(BF16)


def _rwkv_finish(yf, yb, bonus, gate, lnx_g, lnx_b, e_seg):
    rows = yf.shape[0]
    blk = pl.BlockSpec((ROW_TILE, W_GROUP), lambda i: (i, 0))
    vec = pl.BlockSpec((1, W_GROUP), lambda i: (0, 0))
    return pl.pallas_call(
        _rwkv_finish_body,
        grid=(rows // ROW_TILE,),
        in_specs=[blk, blk, blk, blk, vec, vec, pl.BlockSpec((W_GROUP, W_GROUP), lambda i: (0, 0))],
        out_specs=blk,
        out_shape=jax.ShapeDtypeStruct((rows, W_GROUP), BF16),
        compiler_params=_cparams(("arbitrary",)),
        name="rwkv_finish",
    )(yf, yb, bonus, gate, lnx_g.reshape(1, W_GROUP), lnx_b.reshape(1, W_GROUP), e_seg)


def _outproj_body(x_ref, ya_ref, yb_ref, yc_ref, yd_ref, w_ref, mod_ref, g2_ref, wrh_ref, wrl_ref, br_ref,
                  xo_ref, h2_ref, lg_ref):
    mix = _dot(ya_ref[...], w_ref[0:W_GROUP, :])
    mix = mix + _dot(yb_ref[...], w_ref[W_GROUP:2 * W_GROUP, :])
    mix = mix + _dot(yc_ref[...], w_ref[2 * W_GROUP:3 * W_GROUP, :])
    mix = mix + _dot(yd_ref[...], w_ref[3 * W_GROUP:, :])
    x = x_ref[...] + mod_ref[0, 2:3, :] * mix
    xo_ref[...] = x
    ms = jnp.mean(x * x, axis=-1, keepdims=True)
    h2 = x * lax.rsqrt(ms + NORM_EPS) * g2_ref[...]
    h2 = h2 * (1.0 + mod_ref[0, 4:5, :]) + mod_ref[0, 3:4, :]
    h2_ref[...] = h2.astype(BF16)
    lg_ref[...] = _dot3(h2, wrh_ref[...], wrl_ref[...]) + br_ref[...]


def _outproj(x, ys, w_out, mod3, g2, wr_hi, wr_lo, br, n_b, n_seq):
    rows = x.shape[0]
    big = pl.BlockSpec((OUT_TILE, D_MODEL), lambda i: (i, 0))
    yb = pl.BlockSpec((OUT_TILE, W_GROUP), lambda i: (i, 0))
    cst = lambda shape: pl.BlockSpec(shape, lambda i: (0, 0))
    return pl.pallas_call(
        _outproj_body,
        grid=(rows // OUT_TILE,),
        in_specs=[big, yb, yb, yb, yb, cst((D_MODEL, D_MODEL)),
                  pl.BlockSpec((1, 6, D_MODEL), lambda i: (_mod_row(i, OUT_TILE, n_b, n_seq), 0, 0)),
                  cst((1, D_MODEL)), cst((D_MODEL, LANES)), cst((D_MODEL, LANES)), cst((1, LANES))],
        out_specs=[big, big, pl.BlockSpec((OUT_TILE, LANES), lambda i: (i, 0))],
        out_shape=[jax.ShapeDtypeStruct((rows, D_MODEL), F32),
                   jax.ShapeDtypeStruct((rows, D_MODEL), BF16),
                   jax.ShapeDtypeStruct((rows, LANES), F32)],
        compiler_params=_cparams(("arbitrary",)),
        name="outproj_norm_router",
    )(x, ys[0], ys[1], ys[2], ys[3], w_out, mod3, g2.reshape(1, D_MODEL), wr_hi, wr_lo, br)


def _expert_body(be_ref, nv_ref, x_ref, w1_ref, w3_ref, w2_ref, o_ref):
    i = pl.program_id(0)

    @pl.when(i < nv_ref[0])
    def _():
        x = x_ref[...]
        h1 = _dot(x, w1_ref[0])
        hid = (h1 * jax.nn.sigmoid(h1)) * _dot(x, w3_ref[0])
        o_ref[...] = _dot(hid.astype(BF16), w2_ref[0])

    @pl.when(i >= nv_ref[0])
    def _():
        o_ref[...] = jnp.zeros_like(o_ref)


def _experts(xb, block_e, n_valid, w1, w3, w2):
    n_blocks = xb.shape[0] // MOE_BM
    return pl.pallas_call(
        _expert_body,
        grid_spec=pltpu.PrefetchScalarGridSpec(
            num_scalar_prefetch=2,
            grid=(n_blocks,),
            in_specs=[pl.BlockSpec((MOE_BM, D_MODEL), lambda i, be, nv: (i, 0)),
                      pl.BlockSpec((1, D_MODEL, D_EXPERT), lambda i, be, nv: (be[i], 0, 0)),
                      pl.BlockSpec((1, D_MODEL, D_EXPERT), lambda i, be, nv: (be[i], 0, 0)),
                      pl.BlockSpec((1, D_EXPERT, D_MODEL), lambda i, be, nv: (be[i], 0, 0))],
            out_specs=pl.BlockSpec((MOE_BM, D_MODEL), lambda i, be, nv: (i, 0))),
        out_shape=jax.ShapeDtypeStruct((xb.shape[0], D_MODEL), F32),
        compiler_params=_cparams(("arbitrary",)),
        name="moe_experts",
    )(block_e, n_valid, xb, w1, w3, w2)


def _combine_body(x_ref, y0_ref, y1_ref, w_ref, mod_ref, o_ref):
    w = w_ref[...]
    mo = w[:, 0:1] * y0_ref[...] + w[:, 1:2] * y1_ref[...]
    o_ref[...] = x_ref[...] + mod_ref[0, 5:6, :] * mo


def _combine(x, y0, y1, wts, mod3, n_b, n_seq):
    rows = x.shape[0]
    big = pl.BlockSpec((ROW_TILE, D_MODEL), lambda i: (i, 0))
    return pl.pallas_call(
        _combine_body,
        grid=(rows // ROW_TILE,),
        in_specs=[big, big, big, pl.BlockSpec((ROW_TILE, LANES), lambda i: (i, 0)),
                  pl.BlockSpec((1, 6, D_MODEL), lambda i: (_mod_row(i, ROW_TILE, n_b, n_seq), 0, 0))],
        out_specs=big,
        out_shape=jax.ShapeDtypeStruct((rows, D_MODEL), F32),
        compiler_params=_cparams(("arbitrary",)),
        name="moe_combine",
    )(x, y0, y1, wts, mod3)


def _moe(x, h2, logits, mod3, w1, w3, w2, n_b, n_seq):
    rows = x.shape[0]
    lg = logits[:, :N_EGROUPS]
    le = logits[:, N_EGROUPS:N_EGROUPS + N_EXPERTS].reshape(rows, N_EGROUPS, EXP_PER_GROUP)
    gsel = jnp.argmax(lg, axis=-1)
    gate_g = jnp.max(jax.nn.softmax(lg, axis=-1), axis=-1)
    le_sel = jnp.take_along_axis(le, gsel[:, None, None], axis=1)[:, 0]
    top_p, top_i = lax.top_k(jax.nn.softmax(le_sel, axis=-1), 2)
    wts = gate_g[:, None] * top_p / jnp.sum(top_p, axis=-1, keepdims=True)
    eid = (gsel[:, None] * EXP_PER_GROUP + top_i).astype(jnp.int32)
    n_asg = rows * 2
    e_flat = eid.reshape(n_asg)
    order = jnp.argsort(e_flat)
    se = e_flat[order]
    stok = (order // 2).astype(jnp.int32)
    counts = jnp.bincount(e_flat, length=N_EXPERTS)
    start = jnp.cumsum(counts) - counts
    pcounts = ((counts + MOE_BM - 1) // MOE_BM) * MOE_BM
    pend = jnp.cumsum(pcounts)
    pstart = pend - pcounts
    dest = (pstart[se] + (jnp.arange(n_asg) - start[se])).astype(jnp.int32)
    n_blocks = -(-n_asg // MOE_BM) + N_EXPERTS
    slot_tok = jnp.zeros((n_blocks * MOE_BM,), jnp.int32).at[dest].set(stok)
    xb = jnp.take(h2, slot_tok, axis=0)
    blk_start = jnp.arange(n_blocks) * MOE_BM
    block_e = jnp.minimum(jnp.searchsorted(pend, blk_start, side='right'), N_EXPERTS - 1).astype(jnp.int32)
    n_valid = (pend[-1] // MOE_BM).astype(jnp.int32).reshape(1)
    yb = _experts(xb, block_e, n_valid, w1, w3, w2)
    slot_of = jnp.zeros((n_asg,), jnp.int32).at[order].set(dest).reshape(rows, 2)
    y0 = jnp.take(yb, slot_of[:, 0], axis=0)
    y1 = jnp.take(yb, slot_of[:, 1], axis=0)
    wpad = jnp.zeros((rows, LANES), F32).at[:, :2].set(wts)
    return _combine(x, y0, y1, wpad, mod3, n_b, n_seq)


def _pad_rows(w, lo, n):
    pad = [(0, 0)] * (w.ndim - 2) + [(lo, n - lo - w.shape[-2]), (0, 0)]
    return jnp.pad(w.astype(F32), pad).astype(BF16)


def kernel(x, c, ctx, c_ctx, w_ada, b_ada, g_norm1, g_norm2, w_in, w_out, s5_lam_re, s5_lam_im, s5_log_step, s5_b_re, s5_b_im, s5_c_re, s5_c_im, s5_d, s5_w_glu, s5_b_glu, lru_conv_w, lru_conv_b, lru_lam, lru_wa, lru_ba, lru_wx, lru_bx, diff_gq, diff_gk, diff_lq1, diff_lk1, diff_lq2, diff_lk2, diff_subln, rw_mu, rw_w0, rw_w2, rw_a0, rw_a2, rw_g2, rw_kk, rw_ka, rw_rk, rw_lnx_g, rw_lnx_b, moe_w_rg, moe_b_rg, moe_w_re, moe_b_re, moe_w1, moe_w3, moe_w2):
    n_b, n_seq, _ = x.shape
    n_ctx = ctx.shape[1]
    depth = w_ada.shape[0]
    assert n_seq % SEQ_TILE == 0 and n_ctx % SEQ_TILE == 0 and n_seq % n_ctx == 0
    assert (n_b * n_ctx) % ROW_TILE == 0 and n_seq % ROW_TILE == 0 and n_seq % ATT_TK == 0
    xs = jnp.concatenate([x.reshape(n_b * n_seq, D_MODEL), ctx.reshape(n_b * n_ctx, D_MODEL)], axis=0)
    rows = xs.shape[0]
    cvec = jnp.zeros((SUBLANES, D_MODEL), F32).at[0].set(c_ctx).at[1:1 + n_b].set(c)
    rope = _rope_tables(n_b, n_seq, n_ctx)
    seg = jnp.arange(W_GROUP) // RWKV_N
    e_seg = (seg[:, None] == seg[None, :]).astype(BF16)
    nl_s, nc_s = n_seq // SEQ_TILE, n_ctx // SEQ_TILE
    nl_r, nc_r = n_seq // RW_CHUNK, n_ctx // RW_CHUNK

    for l in range(depth):
        lam_init = 0.8 - 0.6 * math.exp(-0.3 * l)
        mod = _modulation(cvec, w_ada[l], b_ada[l])
        mod3 = mod[:1 + n_b].reshape(1 + n_b, 6, D_MODEL)
        w_in_b = jnp.pad(w_in[l].astype(BF16), ((0, 0), (0, D_IN_PAD - D_IN)))
        za = _inproj(xs, mod3, g_norm1[l], w_in_b, n_b, n_seq)
        s5p = _s5_prepare(s5_lam_re[l], s5_lam_im[l], s5_log_step[l], s5_b_re[l], s5_b_im[l],
                          s5_c_re[l], s5_c_im[l])
        y5 = [_s5_scan(za, s5p, d, n_b, nl_s, nc_s) for d in range(2)]
        ya = _s5_finish(y5[0], y5[1], za, s5_d[l], s5_w_glu[l], s5_b_glu[l])
        lrp = dict(cw=lru_conv_w[l], cb=lru_conv_b[l].reshape(1, W_GROUP),
                   wa=_blockdiag(lru_wa[l]).astype(BF16), wx=_blockdiag(lru_wx[l]).astype(BF16),
                   ba=lru_ba[l].reshape(2, 1, W_GROUP), bx=lru_bx[l].reshape(2, 1, W_GROUP),
                   sp=jax.nn.softplus(-lru_lam[l]).reshape(2, 1, W_GROUP))
        hl = [_lru_scan(za, lrp, d, n_b, nl_s, nc_s) for d in range(2)]
        yb_ = _lru_finish(hl[0], hl[1], za)
        lam = jnp.exp(jnp.sum(diff_lq1[l] * diff_lk1[l])) - jnp.exp(jnp.sum(diff_lq2[l] * diff_lk2[l])) + lam_init
        lam_vec = jnp.full((1, LANES), lam, F32)
        qn, kn, vn = _qkv_prep(za, rope, diff_gq[l], diff_gk[l], e_seg)
        o_lat, o_ctx = _attention(qn, kn, vn, lam_vec, diff_subln[l], lam_init, n_b, n_seq, n_ctx)
        yc = jnp.concatenate([o_lat, o_ctx], axis=0)
        lo_w, lo_a = RANK_G, RANK_G + RANK_W
        rwp = dict(mu=rw_mu[l].reshape(1, RWKV_COLS),
                   w2=_pad_rows(rw_w2[l], lo_w, LANES), a2=_pad_rows(rw_a2[l], lo_a, LANES),
                   g2=_pad_rows(rw_g2[l], 0, LANES), w0=rw_w0[l], a0=rw_a0[l],
                   kk=rw_kk[l].reshape(1, W_GROUP), ka=rw_ka[l].reshape(1, W_GROUP),
                   rk=rw_rk[l].reshape(1, W_GROUP))
        pre = _rwkv_prep(za, rwp, e_seg, n_b, nl_s, nc_s)
        yr = [_rwkv_scan(pre, d, n_b, nl_r, nc_r) for d in range(2)]
        yd = _rwkv_finish(yr[0], yr[1], pre[9], pre[10], rw_lnx_g[l], rw_lnx_b[l], e_seg)
        wr = jnp.zeros((D_MODEL, LANES), F32).at[:, :N_EGROUPS].set(moe_w_rg[l])
        wr = wr.at[:, N_EGROUPS:N_EGROUPS + N_EXPERTS].set(moe_w_re[l])
        wr_hi, wr_lo = _split2(wr)
        br = jnp.zeros((1, LANES), F32).at[0, :N_EGROUPS].set(moe_b_rg[l])
        br = br.at[0, N_EGROUPS:N_EGROUPS + N_EXPERTS].set(moe_b_re[l])
        xs, h2, logits = _outproj(xs, (ya, yb_, yc, yd), w_out[l].astype(BF16), mod3, g_norm2[l],
                                  wr_hi, wr_lo, br, n_b, n_seq)
        xs = _moe(xs, h2, logits, mod3, moe_w1[l].astype(BF16), moe_w3[l].astype(BF16), moe_w2[l].astype(BF16),
                  n_b, n_seq)
    return xs[:n_b * n_seq].reshape(n_b, n_seq, D_MODEL)
```

```python
import functools
import math

import jax
import jax.numpy as jnp
from jax import lax
from jax.experimental import pallas as pl
from jax.experimental.pallas import tpu as pltpu

F32 = jnp.float32
BF16 = jnp.bfloat16

D_MODEL = 2048
W_GROUP = 512
S5_CH, S5_GROUPS, S5_STATE = 16, 32, 64
S5_NS = S5_GROUPS * S5_STATE
LRU_C = 8.0
DIFF_HEADS, DIFF_D = 4, 64
ROPE_AX = 16
ROPE_BASE = 10000.0
RWKV_N, RWKV_HEADS = 64, 8
RANK_G, RANK_W, RANK_A = 64, 32, 32
RWKV_COLS = 3 * W_GROUP + RANK_G + RANK_W + RANK_A
N_EGROUPS, EXP_PER_GROUP, N_EXPERTS = 4, 8, 32
D_EXPERT = 1024
NORM_EPS = 1e-6
GN_EPS = 64e-5
GRID_W = 64

SUBLANES = 8
LANES = 128
ROW_TILE = 512
SEQ_TILE = 256
RW_CHUNK = 64
ATT_TQ = 256
ATT_UNROLL = 4
ATT_TK = 512
ATT_RB = 128
MOE_BM = 256
OUT_TILE = 256
IN_TN = 1024
D_IN = 6 * W_GROUP + RWKV_COLS
D_IN_PAD = -(-D_IN // IN_TN) * IN_TN
COL_R, COL_K, COL_V = 6, 7, 8
COL_LOW = (9 * W_GROUP) // LANES
VMEM_LIMIT = 56 * 1024 * 1024


def _cparams(sem):
    return pltpu.CompilerParams(dimension_semantics=sem, vmem_limit_bytes=VMEM_LIMIT)


def _split2(x):
    hi = x.astype(BF16)
    lo = (x - hi.astype(F32)).astype(BF16)
    return hi, lo


def _dot(a, b):
    return jnp.dot(a, b, preferred_element_type=F32)


def _dot_nt(a, b):
    return lax.dot_general(a, b, (((1,), (1,)), ((), ())), preferred_element_type=F32)


def _dot_tn(a, b):
    return lax.dot_general(a, b, (((0,), (0,)), ((), ())), preferred_element_type=F32)


def _dot3(a, b_hi, b_lo):
    a_hi, a_lo = _split2(a)
    return _dot(a_hi, b_hi) + _dot(a_hi, b_lo) + _dot(a_lo, b_hi)


def _segsum(x, e_ref):
    hi, lo = _split2(x)
    e = e_ref[...]
    return _dot(hi, e) + _dot(lo, e)


def _seq_block(rev, b, i, n_c, n_l, n_b):
    if rev:
        ctx = n_b * n_l + b * n_c + (n_c - 1 - i)
        lat = b * n_l + (n_l - 1 - (i - n_c))
    else:
        ctx = n_b * n_l + b * n_c + i
        lat = b * n_l + (i - n_c)
    return jnp.where(i < n_c, ctx, lat)


def _seg_pos(t, n_b, n_l, n_c):
    is_lat = t < n_b * n_l
    p = jnp.where(is_lat, t % n_l, (t - n_b * n_l) % n_c)
    n = jnp.where(is_lat, n_l, n_c)
    return p, n


def _mod_body(c_ref, w_ref, b_ref, o_ref):
    c = c_ref[...]
    s = c * jax.nn.sigmoid(c)
    w_hi, w_lo = _split2(w_ref[...])
    o_ref[...] = _dot3(s, w_hi, w_lo) + b_ref[...]


def _modulation(cvec, w_ada, b_ada):
    d6 = w_ada.shape[1]
    tn = 1536
    return pl.pallas_call(
        _mod_body,
        grid=(d6 // tn,),
        in_specs=[pl.BlockSpec((SUBLANES, D_MODEL), lambda j: (0, 0)),
                  pl.BlockSpec((D_MODEL, tn), lambda j: (0, j)),
                  pl.BlockSpec((1, tn), lambda j: (0, j))],
        out_specs=pl.BlockSpec((SUBLANES, tn), lambda j: (0, j)),
        out_shape=jax.ShapeDtypeStruct((SUBLANES, d6), F32),
        compiler_params=_cparams(("arbitrary",)),
        name="adaln_mod",
    )(cvec, w_ada, b_ada.reshape(1, d6))


def _mod_row(i, tile, n_b, n_seq):
    return jnp.where(i < n_b * n_seq // tile, 1 + i // (n_seq // tile), 0)


def _inproj_body(x_ref, mod_ref, g_ref, w_ref, z_ref, h_ref):
    @pl.when(pl.program_id(1) == 0)
    def _():
        x = x_ref[...]
        ms = jnp.mean(x * x, axis=-1, keepdims=True)
        xn = x * lax.rsqrt(ms + NORM_EPS) * g_ref[...]
        h_ref[...] = (xn * (1.0 + mod_ref[0, 1:2, :]) + mod_ref[0, 0:1, :]).astype(BF16)

    z_ref[...] = _dot(h_ref[...], w_ref[...])


def _inproj(x, mod3, g1, w, n_b, n_seq):
    rows = x.shape[0]
    tn = IN_TN
    return pl.pallas_call(
        _inproj_body,
        grid=(rows // ROW_TILE, w.shape[1] // tn),
        in_specs=[pl.BlockSpec((ROW_TILE, D_MODEL), lambda i, j: (i, 0)),
                  pl.BlockSpec((1, 6, D_MODEL), lambda i, j: (_mod_row(i, ROW_TILE, n_b, n_seq), 0, 0)),
                  pl.BlockSpec((1, D_MODEL), lambda i, j: (0, 0)),
                  pl.BlockSpec((D_MODEL, tn), lambda i, j: (0, j))],
        out_specs=pl.BlockSpec((ROW_TILE, tn), lambda i, j: (i, j)),
        out_shape=jax.ShapeDtypeStruct((rows, w.shape[1]), F32),
        scratch_shapes=[pltpu.VMEM((ROW_TILE, D_MODEL), BF16)],
        compiler_params=_cparams(("arbitrary", "arbitrary")),
        name="norm_inproj",
    )(x, mod3, g1.reshape(1, D_MODEL), w)


def _s5_body(rev, u_ref, bre_ref, bim_ref, t1r_ref, t1i_ref, p2r_ref, p2i_ref, cre_ref, cim_ref,
             y_ref, sr_ref, si_ref, car_ref):
    g_n = SEQ_TILE // SUBLANES

    @pl.when(pl.program_id(1) == 0)
    def _():
        car_ref[...] = jnp.zeros_like(car_ref)

    u = u_ref[...].astype(BF16)
    xr = _dot(u, bre_ref[0]).reshape(g_n, SUBLANES, S5_NS)
    xi = _dot(u, bim_ref[0]).reshape(g_n, SUBLANES, S5_NS)
    for lvl, s in enumerate((1, 2, 4)):
        sh = (SUBLANES - s) if rev else s
        pr = pltpu.roll(xr, sh, axis=1)
        pi = pltpu.roll(xi, sh, axis=1)
        ar = t1r_ref[0, lvl][None]
        ai = t1i_ref[0, lvl][None]
        xr, xi = xr + ar * pr - ai * pi, xi + ar * pi + ai * pr
    sr_ref[...] = xr
    si_ref[...] = xi
    p2r = p2r_ref[0]
    p2i = p2i_ref[0]
    row = 0 if rev else SUBLANES - 1

    def step(g, _):
        gg = (g_n - 1 - g) if rev else g
        cr = car_ref[0]
        ci = car_ref[1]
        nr = sr_ref[gg] + p2r * cr - p2i * ci
        ni = si_ref[gg] + p2r * ci + p2i * cr
        sr_ref[gg] = nr
        si_ref[gg] = ni
        car_ref[0] = jnp.broadcast_to(nr[row:row + 1, :], (SUBLANES, S5_NS))
        car_ref[1] = jnp.broadcast_to(ni[row:row + 1, :], (SUBLANES, S5_NS))
        return 0

    lax.fori_loop(0, g_n, step, 0)
    s_r = sr_ref[...].reshape(SEQ_TILE, S5_NS).astype(BF16)
    s_i = si_ref[...].reshape(SEQ_TILE, S5_NS).astype(BF16)
    y_ref[...] = _dot(s_r, cre_ref[0]) - _dot(s_i, cim_ref[0])


def _s5_scan(za, p, d, n_b, n_l, n_c):
    rev = d == 1
    rows = za.shape[0]
    seq = lambda b, i: (_seq_block(rev, b, i, n_c, n_l, n_b), 0)
    cst3 = lambda b, i: (d, 0, 0)
    cst4 = lambda b, i: (d, 0, 0, 0)
    return pl.pallas_call(
        functools.partial(_s5_body, rev),
        grid=(n_b, n_c + n_l),
        in_specs=[pl.BlockSpec((SEQ_TILE, W_GROUP), seq),
                  pl.BlockSpec((1, W_GROUP, S5_NS), cst3),
                  pl.BlockSpec((1, W_GROUP, S5_NS), cst3),
                  pl.BlockSpec((1, 3, SUBLANES, S5_NS), cst4),
                  pl.BlockSpec((1, 3, SUBLANES, S5_NS), cst4),
                  pl.BlockSpec((1, SUBLANES, S5_NS), cst3),
                  pl.BlockSpec((1, SUBLANES, S5_NS), cst3),
                  pl.BlockSpec((1, S5_NS, W_GROUP), cst3),
                  pl.BlockSpec((1, S5_NS, W_GROUP), cst3)],
        out_specs=pl.BlockSpec((SEQ_TILE, W_GROUP), seq),
        out_shape=jax.ShapeDtypeStruct((rows, W_GROUP), F32),
        scratch_shapes=[pltpu.VMEM((SEQ_TILE // SUBLANES, SUBLANES, S5_NS), F32),
                        pltpu.VMEM((SEQ_TILE // SUBLANES, SUBLANES, S5_NS), F32),
                        pltpu.VMEM((2, SUBLANES, S5_NS), F32)],
        compiler_params=_cparams(("arbitrary", "arbitrary")),
        name="s5_scan_rev" if rev else "s5_scan_fwd",
    )(za, p["bre"], p["bim"], p["t1r"], p["t1i"], p["p2r"], p["p2i"], p["cre"], p["cim"])


def _s5_prepare(lam_re, lam_im, log_step, b_re, b_im, c_re, c_im):
    lr, li = lam_re.astype(F32), lam_im.astype(F32)
    dt = jnp.exp(log_step.astype(F32))[..., None]
    mag = jnp.exp(lr * dt)
    ar, ai = mag * jnp.cos(li * dt), mag * jnp.sin(li * dt)
    den = lr * lr + li * li
    cr = ((ar - 1.0) * lr + ai * li) / den
    ci = (ai * lr - (ar - 1.0) * li) / den
    bbr = cr[..., None] * b_re - ci[..., None] * b_im
    bbi = cr[..., None] * b_im + ci[..., None] * b_re
    eye = jnp.eye(S5_GROUPS, dtype=F32)
    bd = lambda t: jnp.einsum('dgph,gk->dghkp', t, eye).reshape(2, W_GROUP, S5_NS).astype(BF16)
    cd = lambda t: jnp.einsum('dghp,gk->dgpkh', t.astype(F32), eye).reshape(2, S5_NS, W_GROUP).astype(BF16)
    ar = ar.reshape(2, S5_NS)
    ai = ai.reshape(2, S5_NS)
    pw_r, pw_i = [ar], [ai]
    for _ in range(SUBLANES - 1):
        pw_r.append(pw_r[-1] * ar - pw_i[-1] * ai)
        pw_i.append(pw_r[-2] * ai + pw_i[-1] * ar)
    k = jnp.arange(SUBLANES)[None, :, None]
    t1r, t1i, p2r, p2i = [], [], [], []
    for d in range(2):
        lr_, li_ = [], []
        for s in (1, 2, 4):
            m = (k + s <= SUBLANES - 1) if d == 1 else (k >= s)
            lr_.append(jnp.where(m, pw_r[s - 1][d][None, None, :], 0.0)[0])
            li_.append(jnp.where(m, pw_i[s - 1][d][None, None, :], 0.0)[0])
        t1r.append(jnp.stack(lr_))
        t1i.append(jnp.stack(li_))
        order = range(SUBLANES - 1, -1, -1) if d == 1 else range(SUBLANES)
        p2r.append(jnp.stack([pw_r[j][d] for j in order]))
        p2i.append(jnp.stack([pw_i[j][d] for j in order]))
    return dict(bre=bd(bbr), bim=bd(bbi), cre=cd(c_re), cim=cd(c_im),
                t1r=jnp.stack(t1r), t1i=jnp.stack(t1i), p2r=jnp.stack(p2r), p2i=jnp.stack(p2i))


def _s5_finish_body(yf_ref, yb_ref, u_ref, d_ref, w_ref, b_ref, o_ref):
    y = jax.nn.gelu(yf_ref[...] + yb_ref[...] + d_ref[...] * u_ref[...])
    gate = jax.nn.sigmoid(_dot(y.astype(BF16), w_ref[...]) + b_ref[...])
    o_ref[...] = (y * gate).astype(BF16)


def _s5_finish(yf, yb, za, d_skip, w_glu, b_glu):
    rows = yf.shape[0]
    blk = pl.BlockSpec((ROW_TILE, W_GROUP), lambda i: (i, 0))
    vec = pl.BlockSpec((1, W_GROUP), lambda i: (0, 0))
    return pl.pallas_call(
        _s5_finish_body,
        grid=(rows // ROW_TILE,),
        in_specs=[blk, blk, blk, vec, pl.BlockSpec((W_GROUP, W_GROUP), lambda i: (0, 0)), vec],
        out_specs=blk,
        out_shape=jax.ShapeDtypeStruct((rows, W_GROUP), BF16),
        compiler_params=_cparams(("arbitrary",)),
        name="s5_finish",
    )(yf, yb, za, d_skip.reshape(1, W_GROUP), w_glu.astype(BF16), b_glu.reshape(1, W_GROUP))


HALO = SUBLANES


def _halo_maps(tile_fn, rows, col):
    per = SEQ_TILE // HALO
    last = rows // HALO - 1
    prv = lambda *ids: (jnp.maximum(tile_fn(*ids) * per - 1, 0), col)
    nxt = lambda *ids: (jnp.minimum((tile_fn(*ids) + 1) * per, last), col)
    return prv, nxt


def _lru_body(rev, n_b, n_l, n_c, xp_ref, xc_ref, xn_ref, cw_ref, cb_ref, wa_ref, ba_ref, wx_ref, bx_ref,
              sp_ref, h_ref, buf_ref, a_ref, b_ref, car_ref):
    g_n = SEQ_TILE // SUBLANES
    b = pl.program_id(0)
    i = pl.program_id(1)

    @pl.when(i == 0)
    def _():
        car_ref[...] = jnp.zeros_like(car_ref)

    p, n = _seg_pos(_seq_block(rev, b, i, n_c, n_l, n_b), n_b, n_l, n_c)
    prev_ok = (p > 0).astype(F32)
    next_ok = (p < n - 1).astype(F32)
    buf_ref[0:HALO, :] = xp_ref[...] * prev_ok
    buf_ref[HALO:HALO + SEQ_TILE, :] = xc_ref[...]
    buf_ref[HALO + SEQ_TILE:, :] = xn_ref[...] * next_ok
    v = cb_ref[...] + cw_ref[2:3, :] * xc_ref[...]
    v = v + cw_ref[0:1, :] * buf_ref[HALO - 2:HALO - 2 + SEQ_TILE, :]
    v = v + cw_ref[1:2, :] * buf_ref[HALO - 1:HALO - 1 + SEQ_TILE, :]
    v = v + cw_ref[3:4, :] * buf_ref[HALO + 1:HALO + 1 + SEQ_TILE, :]
    vb = v.astype(BF16)
    r = jax.nn.sigmoid(_dot(vb, wa_ref[0]) + ba_ref[0])
    ig = jax.nn.sigmoid(_dot(vb, wx_ref[0]) + bx_ref[0])
    a = jnp.exp(-LRU_C * r * sp_ref[0])
    bb = jnp.sqrt(1.0 - a * a) * (ig * v)
    a3 = a.reshape(g_n, SUBLANES, W_GROUP)
    b3 = bb.reshape(g_n, SUBLANES, W_GROUP)
    k = lax.broadcasted_iota(jnp.int32, (g_n, SUBLANES, W_GROUP), 1)
    for s in (1, 2, 4):
        sh = (SUBLANES - s) if rev else s
        m = (k + s <= SUBLANES - 1) if rev else (k >= s)
        a_s = pltpu.roll(a3, sh, axis=1)
        b_s = pltpu.roll(b3, sh, axis=1)
        b3 = jnp.where(m, a3 * b_s + b3, b3)
        a3 = jnp.where(m, a3 * a_s, a3)
    a_ref[...] = a3
    b_ref[...] = b3
    row = 0 if rev else SUBLANES - 1

    def step(g, _):
        gg = (g_n - 1 - g) if rev else g
        hh = b_ref[gg] + a_ref[gg] * car_ref[...]
        b_ref[gg] = hh
        car_ref[...] = jnp.broadcast_to(hh[row:row + 1, :], (SUBLANES, W_GROUP))
        return 0

    lax.fori_loop(0, g_n, step, 0)
    h_ref[...] = b_ref[...].reshape(SEQ_TILE, W_GROUP)


def _lru_scan(za, p, d, n_b, n_l, n_c):
    rev = d == 1
    rows = za.shape[0]
    tile = lambda b, i: _seq_block(rev, b, i, n_c, n_l, n_b)
    cur = lambda b, i: (tile(b, i), 1)
    prv, nxt = _halo_maps(tile, rows, 1)
    out = lambda b, i: (tile(b, i), 0)
    vec = pl.BlockSpec((1, W_GROUP), lambda b, i: (0, 0))
    dvec = pl.BlockSpec((1, 1, W_GROUP), lambda b, i: (d, 0, 0))
    dmat = pl.BlockSpec((1, W_GROUP, W_GROUP), lambda b, i: (d, 0, 0))
    blk = lambda f: pl.BlockSpec((SEQ_TILE, W_GROUP), f)
    halo = lambda f: pl.BlockSpec((HALO, W_GROUP), f)
    g_n = SEQ_TILE // SUBLANES
    return pl.pallas_call(
        functools.partial(_lru_body, rev, n_b, n_l, n_c),
        grid=(n_b, n_c + n_l),
        in_specs=[halo(prv), blk(cur), halo(nxt),
                  pl.BlockSpec((4, W_GROUP), lambda b, i: (0, 0)), vec,
                  dmat, dvec, dmat, dvec, dvec],
        out_specs=blk(out),
        out_shape=jax.ShapeDtypeStruct((rows, W_GROUP), F32),
        scratch_shapes=[pltpu.VMEM((SEQ_TILE + 2 * HALO, W_GROUP), F32),
                        pltpu.VMEM((g_n, SUBLANES, W_GROUP), F32),
                        pltpu.VMEM((g_n, SUBLANES, W_GROUP), F32),
                        pltpu.VMEM((SUBLANES, W_GROUP), F32)],
        compiler_params=_cparams(("arbitrary", "arbitrary")),
        name="lru_scan_rev" if rev else "lru_scan_fwd",
    )(za, za, za, p["cw"], p["cb"], p["wa"], p["ba"], p["wx"], p["bx"], p["sp"])


def _blockdiag(w):
    nb, c = w.shape[1], w.shape[2]
    eye = jnp.eye(nb, dtype=F32)
    return jnp.einsum('dncf,nm->dncmf', w.astype(F32), eye).reshape(2, nb * c, nb * c)


def _lru_finish_body(hf_ref, hb_ref, g_ref, o_ref):
    o_ref[...] = ((hf_ref[...] + hb_ref[...]) * jax.nn.gelu(g_ref[...])).astype(BF16)


def _lru_finish(hf, hb, za):
    rows = hf.shape[0]
    blk = pl.BlockSpec((ROW_TILE, W_GROUP), lambda i: (i, 0))
    return pl.pallas_call(
        _lru_finish_body,
        grid=(rows // ROW_TILE,),
        in_specs=[blk, blk, pl.BlockSpec((ROW_TILE, W_GROUP), lambda i: (i, 2))],
        out_specs=blk,
        out_shape=jax.ShapeDtypeStruct((rows, W_GROUP), BF16),
        compiler_params=_cparams(("arbitrary",)),
        name="lru_finish",
    )(hf, hb, za)


def _qkv_prep_body(q_ref, k_ref, v_ref, cos_ref, s1_ref, s2_ref, gq_ref, gk_ref, e_ref, qo_ref, ko_ref, vo_ref):
    reps = W_GROUP // LANES
    cos = jnp.concatenate([cos_ref[...]] * reps, axis=1)
    s1 = jnp.concatenate([s1_ref[...]] * reps, axis=1)
    s2 = jnp.concatenate([s2_ref[...]] * reps, axis=1)

    def prep(x, g):
        ms = _segsum(x * x, e_ref) * (1.0 / DIFF_D)
        x = x * lax.rsqrt(ms + NORM_EPS) * g
        return x * cos + pltpu.roll(x, ROPE_AX, axis=1) * s1 + pltpu.roll(x, W_GROUP - ROPE_AX, axis=1) * s2

    qo_ref[...] = (prep(q_ref[...], gq_ref[...]) * (DIFF_D ** -0.5)).astype(BF16)
    ko_ref[...] = prep(k_ref[...], gk_ref[...]).astype(BF16)
    vo_ref[...] = v_ref[...].astype(BF16)


def _qkv_prep(za, rope, gq, gk, e_seg):
    rows = za.shape[0]
    col = lambda c: pl.BlockSpec((ROW_TILE, W_GROUP), lambda i: (i, c))
    tab = pl.BlockSpec((ROW_TILE, LANES), lambda i: (i, 0))
    vec = pl.BlockSpec((1, W_GROUP), lambda i: (0, 0))
    out = pl.BlockSpec((ROW_TILE, W_GROUP), lambda i: (i, 0))
    reps = W_GROUP // DIFF_D
    return pl.pallas_call(
        _qkv_prep_body,
        grid=(rows // ROW_TILE,),
        in_specs=[col(3), col(4), col(5), tab, tab, tab, vec, vec,
                  pl.BlockSpec((W_GROUP, W_GROUP), lambda i: (0, 0))],
        out_specs=[out, out, out],
        out_shape=[jax.ShapeDtypeStruct((rows, W_GROUP), BF16)] * 3,
        compiler_params=_cparams(("arbitrary",)),
        name="attn_qkv_prep",
    )(za, za, za, rope[0], rope[1], rope[2], jnp.tile(gq, reps).reshape(1, W_GROUP),
      jnp.tile(gk, reps).reshape(1, W_GROUP), e_seg)


def _attn_body(seg_lens, out_scale, lam_ref, sub_ref, q_ref, *rest):
    o_ref, q2_ref, m_ref, acc_ref, sa_ref, sb_ref = rest[-6:]
    kv = rest[:-6]
    tq = q_ref.shape[0]
    q = q_ref[...]
    lane = lax.broadcasted_iota(jnp.int32, (tq, LANES), 1)
    zero = jnp.zeros_like(q)
    q2 = jnp.concatenate([jnp.where(lane < DIFF_D, q, zero), jnp.where(lane < DIFF_D, zero, q)], axis=0)
    q2_ref[...] = q2
    m_ref[...] = jnp.full_like(m_ref, -jnp.inf)
    acc_ref[...] = jnp.zeros_like(acc_ref)
    n_rb = (2 * tq) // ATT_RB

    rows = [slice(rb * ATT_RB, (rb + 1) * ATT_RB) for rb in range(n_rb)]

    def scores(kc, s_ref):
        for rs in rows:
            s_ref[rs, 0:kc.shape[0]] = _dot_nt(q2_ref[rs, :], kc)

    def softmax_pv(s_ref, vc):
        tk = vc.shape[0]
        v_ext = jnp.concatenate([vc, jnp.ones_like(vc)], axis=1)
        for rs in rows:
            s = s_ref[rs, 0:tk]
            m_old = m_ref[rs, :]
            m_new = jnp.maximum(m_old, jnp.max(s, axis=-1, keepdims=True))
            alpha = jnp.exp(m_old - m_new)
            pm = jnp.exp((s - jnp.concatenate([m_new] * (tk // LANES), axis=1)).astype(BF16))
            acc_ref[rs, :] = jnp.concatenate([alpha, alpha], axis=1) * acc_ref[rs, :] + _dot(pm, v_ext)
            m_ref[rs, :] = m_new

    for si, n_k in enumerate(seg_lens):
        k_ref, v_ref = kv[2 * si], kv[2 * si + 1]
        tk = min(ATT_TK, n_k)
        n_ch = n_k // tk
        if n_ch < 2:
            for j in range(n_ch):
                scores(k_ref[j * tk:(j + 1) * tk, :], sa_ref)
                softmax_pv(sa_ref, v_ref[j * tk:(j + 1) * tk, :])
            continue
        unroll = min(ATT_UNROLL, n_ch)
        assert unroll % 2 == 0 and n_ch % unroll == 0
        chunk = lambda ref, j, tk=tk: ref[pl.ds(pl.multiple_of(j * tk, tk), tk), :]
        scores(chunk(k_ref, 0), sa_ref)

        def body(jj, _, k_ref=k_ref, v_ref=v_ref, n_ch=n_ch, unroll=unroll):
            for u in range(unroll):
                j = unroll * jj + u
                cur, nxt = (sa_ref, sb_ref) if u % 2 == 0 else (sb_ref, sa_ref)
                scores(chunk(k_ref, jnp.minimum(j + 1, n_ch - 1)), nxt)
                softmax_pv(cur, chunk(v_ref, j))
            return 0

        lax.fori_loop(0, n_ch // unroll, body, 0)

    o = acc_ref[:, 0:LANES] / acc_ref[:, LANES:]
    o = o[:tq] - lam_ref[...] * o[tq:]
    ms = jnp.mean(o * o, axis=-1, keepdims=True)
    o_ref[...] = (o * lax.rsqrt(ms + NORM_EPS) * sub_ref[...] * out_scale).astype(BF16)


def _attention(qn, kn, vn, lam_vec, subln, lam_init, n_b, n_seq, n_ctx):
    vec = lambda nd: pl.BlockSpec((1, LANES), (lambda b, h, i: (0, 0)) if nd == 3 else (lambda b, h: (0, 0)))
    scratch = lambda tq: [pltpu.VMEM((2 * tq, LANES), BF16), pltpu.VMEM((2 * tq, LANES), F32),
                          pltpu.VMEM((2 * tq, 2 * LANES), F32),
                          pltpu.VMEM((2 * tq, ATT_TK), F32), pltpu.VMEM((2 * tq, ATT_TK), F32)]
    sub = subln.reshape(1, LANES)
    ctx_blk0 = n_b * n_seq // n_ctx
    qpb = n_seq // ATT_TQ
    lat_kv = pl.BlockSpec((n_seq, LANES), lambda b, h, i: (b, h))
    ctx_kv = pl.BlockSpec((n_ctx, LANES), lambda b, h, i: (ctx_blk0 + b, h))
    o_lat = pl.pallas_call(
        functools.partial(_attn_body, (n_seq, n_ctx), 1.0 - lam_init),
        grid=(n_b, DIFF_HEADS, qpb),
        in_specs=[vec(3), vec(3), pl.BlockSpec((ATT_TQ, LANES), lambda b, h, i: (b * qpb + i, h)),
                  lat_kv, lat_kv, ctx_kv, ctx_kv],
        out_specs=pl.BlockSpec((ATT_TQ, LANES), lambda b, h, i: (b * qpb + i, h)),
        out_shape=jax.ShapeDtypeStruct((n_b * n_seq, W_GROUP), BF16),
        scratch_shapes=scratch(ATT_TQ),
        compiler_params=_cparams(("arbitrary", "arbitrary", "arbitrary")),
        name="diff_attn_latent",
    )(lam_vec, sub, qn, kn, vn, kn, vn)
    ckv = pl.BlockSpec((n_ctx, LANES), lambda b, h: (ctx_blk0 + b, h))
    o_ctx = pl.pallas_call(
        functools.partial(_attn_body, (n_ctx,), 1.0 - lam_init),
        grid=(n_b, DIFF_HEADS),
        in_specs=[vec(2), vec(2), ckv, ckv, ckv],
        out_specs=pl.BlockSpec((n_ctx, LANES), lambda b, h: (b, h)),
        out_shape=jax.ShapeDtypeStruct((n_b * n_ctx, W_GROUP), BF16),
        scratch_shapes=scratch(n_ctx),
        compiler_params=_cparams(("arbitrary", "arbitrary")),
        name="diff_attn_context",
    )(lam_vec, sub, qn, kn, vn)
    return o_lat, o_ctx


def _rope_tables(n_b, n_seq, n_ctx):
    pos = jnp.arange(n_seq)
    row = (pos // GRID_W).astype(F32)
    col = (pos % GRID_W).astype(F32)
    inv_freq = ROPE_BASE ** (-jnp.arange(ROPE_AX, dtype=F32) / ROPE_AX)
    ang_r, ang_c = row[:, None] * inv_freq, col[:, None] * inv_freq
    z = jnp.zeros_like(ang_r)
    cos = jnp.concatenate([jnp.cos(ang_r)] * 2 + [jnp.cos(ang_c)] * 2, axis=1)
    s1 = jnp.concatenate([z, jnp.sin(ang_r), z, jnp.sin(ang_c)], axis=1)
    s2 = jnp.concatenate([-jnp.sin(ang_r), z, -jnp.sin(ang_c), z], axis=1)

    def full(t, ctx_val):
        t = jnp.tile(jnp.tile(t, (1, LANES // DIFF_D)), (n_b, 1))
        return jnp.concatenate([t, jnp.full((n_b * n_ctx, LANES), ctx_val, F32)], axis=0)
    return full(cos, 1.0), full(s1, 0.0), full(s2, 0.0)


def _rwkv_prep_body(n_b, n_l, n_c,
                    rp_ref, rc_ref, rn_ref, kp_ref, kc_ref, kn_ref, vp_ref, vc_ref, vn_ref, lp_ref, lc_ref, ln_ref,
                    mu_ref, w2_ref, a2_ref, g2_ref, w0_ref, a0_ref, kk_w_ref, ka_ref, rk_ref, e_ref,
                    r_ref, v_ref, kk_ref, lw0_ref, kd0_ref, b0_ref, lw1_ref, kd1_ref, b1_ref, bon_ref, g_ref,
                    buf_ref, bufl_ref):
    t = pl.program_id(0)
    p, n = _seg_pos(t, n_b, n_l, n_c)
    prev_ok = (p > 0).astype(F32)
    next_ok = (p < n - 1).astype(F32)

    def shifted(zp_ref, zc_ref, zn_ref, buf, mu):
        buf[0:HALO, :] = zp_ref[...] * prev_ok
        buf[HALO:HALO + SEQ_TILE, :] = zc_ref[...]
        buf[HALO + SEQ_TILE:, :] = zn_ref[...] * next_ok
        z = zc_ref[...]
        return z + mu * (0.5 * (buf[HALO - 1:HALO - 1 + SEQ_TILE, :] + buf[HALO + 1:HALO + 1 + SEQ_TILE, :]) - z)

    r = shifted(rp_ref, rc_ref, rn_ref, buf_ref, mu_ref[:, 0:W_GROUP])
    k = shifted(kp_ref, kc_ref, kn_ref, buf_ref, mu_ref[:, W_GROUP:2 * W_GROUP])
    v = shifted(vp_ref, vc_ref, vn_ref, buf_ref, mu_ref[:, 2 * W_GROUP:3 * W_GROUP])
    low = shifted(lp_ref, lc_ref, ln_ref, bufl_ref, mu_ref[:, 3 * W_GROUP:])
    kk = k * kk_w_ref[...]
    ss = _segsum(kk * kk, e_ref)
    kk = kk / jnp.maximum(jnp.sqrt(ss), 1e-12)
    tw = jnp.tanh(low).astype(BF16)
    lb = low.astype(BF16)
    r_ref[...] = r
    v_ref[...] = v
    kk_ref[...] = kk
    ksum = jnp.zeros_like(k)
    for d, (lw_ref, kd_ref, b_ref) in enumerate(((lw0_ref, kd0_ref, b0_ref), (lw1_ref, kd1_ref, b1_ref))):
        y = w0_ref[d:d + 1, :] + _dot(tw, w2_ref[d])
        y = -y
        softplus = jnp.maximum(y, 0.0) + jnp.log(1.0 + jnp.exp(-jnp.abs(y)))
        lw_ref[...] = -jnp.exp(-softplus - 0.5)
        ag = jax.nn.sigmoid(a0_ref[d:d + 1, :] + _dot(lb, a2_ref[d]))
        kd = k * (1.0 + (ag - 1.0) * ka_ref[...])
        kd_ref[...] = kd
        b_ref[...] = kk * ag
        ksum = ksum + kd
    bon_ref[...] = _segsum(r * ksum * rk_ref[...], e_ref) * v
    g_ref[...] = _dot(jax.nn.sigmoid(low).astype(BF16), g2_ref[...])


def _rwkv_prep(z, p, e_seg, n_b, n_l, n_c):
    rows = z.shape[0]
    vec = pl.BlockSpec((1, W_GROUP), lambda t: (0, 0))
    two = pl.BlockSpec((2, W_GROUP), lambda t: (0, 0))
    lowm = pl.BlockSpec((2, LANES, W_GROUP), lambda t: (0, 0, 0))
    out = pl.BlockSpec((SEQ_TILE, W_GROUP), lambda t: (t, 0))
    z_specs = []
    for col, width in ((COL_R, W_GROUP), (COL_K, W_GROUP), (COL_V, W_GROUP), (COL_LOW, LANES)):
        prv, nxt = _halo_maps(lambda t: t, rows, col)
        z_specs += [pl.BlockSpec((HALO, width), prv),
                    pl.BlockSpec((SEQ_TILE, width), lambda t, col=col: (t, col)),
                    pl.BlockSpec((HALO, width), nxt)]
    return pl.pallas_call(
        functools.partial(_rwkv_prep_body, n_b, n_l, n_c),
        grid=(rows // SEQ_TILE,),
        in_specs=z_specs + [pl.BlockSpec((1, RWKV_COLS), lambda t: (0, 0)), lowm, lowm,
                            pl.BlockSpec((LANES, W_GROUP), lambda t: (0, 0)), two, two, vec, vec, vec,
                            pl.BlockSpec((W_GROUP, W_GROUP), lambda t: (0, 0))],
        out_specs=[out] * 11,
        out_shape=[jax.ShapeDtypeStruct((rows, W_GROUP), F32)] * 11,
        scratch_shapes=[pltpu.VMEM((SEQ_TILE + 2 * HALO, W_GROUP), F32),
                        pltpu.VMEM((SEQ_TILE + 2 * HALO, LANES), F32)],
        compiler_params=_cparams(("arbitrary",)),
        name="rwkv_prep",
    )(*([z] * 12), p["mu"], p["w2"], p["a2"], p["g2"], p["w0"], p["a0"], p["kk"], p["ka"], p["rk"], e_seg)


def _rwkv_scan_body(*refs):
    c = RW_CHUNK
    in_refs, y_refs, st_ref = refs[:12], refs[12:14], refs[14]
    n_pair = RWKV_HEADS // 2

    @pl.when(pl.program_id(1) == 0)
    def _():
        st_ref[...] = jnp.zeros_like(st_ref)

    rt = lax.broadcasted_iota(jnp.int32, (c, c), 0)
    ct = lax.broadcasted_iota(jnp.int32, (c, c), 1)
    lane = lax.broadcasted_iota(jnp.int32, (c, LANES), 1)
    head0 = lane < RWKV_N
    rr = lax.broadcasted_iota(jnp.int32, (2 * c, 2 * c), 0)
    cc = lax.broadcasted_iota(jnp.int32, (2 * c, 2 * c), 1)
    same = (rr >= c) == (cc >= c)
    tt = rr & (c - 1)
    ss = cc & (c - 1)

    def stack(x):
        z = jnp.zeros_like(x)
        return jnp.concatenate([jnp.where(head0, x, z), jnp.where(head0, z, x)], axis=0).astype(BF16)

    chains = []
    for d in range(2):
        rev = d == 1
        r_ref, v_ref, kk_ref, lw_ref, kd_ref, b_ref = in_refs[6 * d:6 * d + 6]
        tri = ((rt <= ct) if rev else (rt >= ct)).astype(BF16)
        lw = lw_ref[...]
        hi = lw.astype(BF16)
        r1 = lw - hi.astype(F32)
        mid = r1.astype(BF16)
        lo = (r1 - mid.astype(F32)).astype(BF16)
        cum = _dot(tri, hi) + _dot(tri, mid) + _dot(tri, lo)
        last = 0 if rev else c - 1
        tot = cum[last:last + 1, :]
        g_end = jnp.exp(tot - cum)
        g_inv = jnp.exp(-cum)
        full = dict(a=-kk_ref[...] * jnp.exp(cum - lw), r=r_ref[...] * jnp.exp(cum), k=kd_ref[...] * g_inv,
                    b=b_ref[...] * g_inv, kg=kd_ref[...] * g_end, bg=b_ref[...] * g_end, v=v_ref[...])
        g_tot = jnp.exp(tot)
        strict = same & ((tt < ss) if rev else (tt > ss))
        incl = same & ((tt <= ss) if rev else (tt >= ss))
        for p in range(n_pair):
            sl = slice(p * LANES, (p + 1) * LANES)
            ch = {k: stack(x[:, sl]) for k, x in full.items()}
            ch.update(d=d, p=p, sl=sl, strict=strict, incl=incl, g_tot=g_tot[:, sl])
            chains.append(ch)

    for ch in chains:
        ch["low"] = jnp.where(ch["strict"], _dot_nt(ch["a"], ch["b"]), 0.0).astype(BF16)
        ch["a_k"] = jnp.where(ch["strict"], _dot_nt(ch["a"], ch["k"]), 0.0).astype(BF16)
    for ch in chains:
        ch["st"] = st_ref[ch["d"], ch["p"]]
        ch["stb"] = ch["st"].astype(BF16)
        ch["u"] = _dot_nt(ch["a"], ch["stb"]) + _dot(ch["a_k"], ch["v"])
    for ch in chains:
        ch["u"] = ch["u"] + _dot(ch["low"], ch["u"].astype(BF16))
    for _ in range(int(math.log2(c)) - 1):
        for ch in chains:
            ch["low"] = _dot(ch["low"], ch["low"]).astype(BF16)
        for ch in chains:
            ch["u"] = ch["u"] + _dot(ch["low"], ch["u"].astype(BF16))
    for ch in chains:
        ch["ub"] = ch["u"].astype(BF16)
        ch["r_k"] = jnp.where(ch["incl"], _dot_nt(ch["r"], ch["k"]), 0.0).astype(BF16)
        ch["r_b"] = jnp.where(ch["incl"], _dot_nt(ch["r"], ch["b"]), 0.0).astype(BF16)
    for ch in chains:
        y = _dot_nt(ch["r"], ch["stb"]) + _dot(ch["r_k"], ch["v"]) + _dot(ch["r_b"], ch["ub"])
        y_refs[ch["d"]][:, ch["sl"]] = y[:c] + y[c:]
        st_ref[ch["d"], ch["p"]] = ch["st"] * ch["g_tot"] + _dot_tn(ch["v"], ch["kg"]) + _dot_tn(ch["ub"], ch["bg"])


def _rwkv_scan(pre, n_b, n_l, n_c):
    rows = pre[0].shape[0]
    blk = lambda rev: pl.BlockSpec((RW_CHUNK, W_GROUP),
                                   lambda b, i, rev=rev: (_seq_block(rev, b, i, n_c, n_l, n_b), 0))
    args, specs = [], []
    for d in range(2):
        args += [pre[0], pre[1], pre[2], pre[3 + 3 * d], pre[4 + 3 * d], pre[5 + 3 * d]]
        specs += [blk(d == 1)] * 6
    return pl.pallas_call(
        _rwkv_scan_body,
        grid=(n_b, n_c + n_l),
        in_specs=specs,
        out_specs=[blk(False), blk(True)],
        out_shape=[jax.ShapeDtypeStruct((rows, W_GROUP), F32)] * 2,
        scratch_shapes=[pltpu.VMEM((2, RWKV_HEADS // 2, LANES, LANES), F32)],
        compiler_params=_cparams(("arbitrary", "arbitrary")),
        name="rwkv_scan",
    )(*args)


def _rwkv_finish_body(yf_ref, yb_ref, bon_ref, g_ref, lg_ref, lb_ref, e_ref, o_ref):
    y = yf_ref[...] + yb_ref[...]
    dlt = y - _segsum(y, e_ref) * (1.0 / RWKV_N)
    var = _segsum(dlt * dlt, e_ref) * (1.0 / RWKV_N)
    yn = dlt * lax.rsqrt(var + GN_EPS) * lg_ref[...] + lb_ref[...]
    o_ref[...] = ((yn + bon_ref[...]) * g_ref[...]).astype(BF16)


def _rwkv_finish(yf, yb, bonus, gate, lnx_g, lnx_b, e_seg):
    rows = yf.shape[0]
    blk = pl.BlockSpec((ROW_TILE, W_GROUP), lambda i: (i, 0))
    vec = pl.BlockSpec((1, W_GROUP), lambda i: (0, 0))
    return pl.pallas_call(
        _rwkv_finish_body,
        grid=(rows // ROW_TILE,),
        in_specs=[blk, blk, blk, blk, vec, vec, pl.BlockSpec((W_GROUP, W_GROUP), lambda i: (0, 0))],
        out_specs=blk,
        out_shape=jax.ShapeDtypeStruct((rows, W_GROUP), BF16),
        compiler_params=_cparams(("arbitrary",)),
        name="rwkv_finish",
    )(yf, yb, bonus, gate, lnx_g.reshape(1, W_GROUP), lnx_b.reshape(1, W_GROUP), e_seg)


def _outproj_body(x_ref, ya_ref, yb_ref, yc_ref, yd_ref, w_ref, mod_ref, g2_ref, wrh_ref, wrl_ref, br_ref,
                  xo_ref, h2_ref, lg_ref):
    mix = _dot(ya_ref[...], w_ref[0:W_GROUP, :])
    mix = mix + _dot(yb_ref[...], w_ref[W_GROUP:2 * W_GROUP, :])
    mix = mix + _dot(yc_ref[...], w_ref[2 * W_GROUP:3 * W_GROUP, :])
    mix = mix + _dot(yd_ref[...], w_ref[3 * W_GROUP:, :])
    x = x_ref[...] + mod_ref[0, 2:3, :] * mix
    xo_ref[...] = x
    ms = jnp.mean(x * x, axis=-1, keepdims=True)
    h2 = x * lax.rsqrt(ms + NORM_EPS) * g2_ref[...]
    h2 = h2 * (1.0 + mod_ref[0, 4:5, :]) + mod_ref[0, 3:4, :]
    h2_ref[...] = h2.astype(BF16)
    lg_ref[...] = _dot3(h2, wrh_ref[...], wrl_ref[...]) + br_ref[...]


def _outproj(x, ys, w_out, mod3, g2, wr_hi, wr_lo, br, n_b, n_seq):
    rows = x.shape[0]
    big = pl.BlockSpec((OUT_TILE, D_MODEL), lambda i: (i, 0))
    yb = pl.BlockSpec((OUT_TILE, W_GROUP), lambda i: (i, 0))
    cst = lambda shape: pl.BlockSpec(shape, lambda i: (0, 0))
    return pl.pallas_call(
        _outproj_body,
        grid=(rows // OUT_TILE,),
        in_specs=[big, yb, yb, yb, yb, cst((D_MODEL, D_MODEL)),
                  pl.BlockSpec((1, 6, D_MODEL), lambda i: (_mod_row(i, OUT_TILE, n_b, n_seq), 0, 0)),
                  cst((1, D_MODEL)), cst((D_MODEL, LANES)), cst((D_MODEL, LANES)), cst((1, LANES))],
        out_specs=[big, big, pl.BlockSpec((OUT_TILE, LANES), lambda i: (i, 0))],
        out_shape=[jax.ShapeDtypeStruct((rows, D_MODEL), F32),
                   jax.ShapeDtypeStruct((rows, D_MODEL), BF16),
                   jax.ShapeDtypeStruct((rows, LANES), F32)],
        compiler_params=_cparams(("arbitrary",)),
        name="outproj_norm_router",
    )(x, ys[0], ys[1], ys[2], ys[3], w_out, mod3, g2.reshape(1, D_MODEL), wr_hi, wr_lo, br)


def _expert_body(be_ref, nv_ref, x_ref, w1_ref, w3_ref, w2_ref, o_ref):
    i = pl.program_id(0)

    @pl.when(i < nv_ref[0])
    def _():
        x = x_ref[...]
        h1 = _dot(x, w1_ref[0])
        hid = (h1 * jax.nn.sigmoid(h1)) * _dot(x, w3_ref[0])
        o_ref[...] = _dot(hid.astype(BF16), w2_ref[0])

    @pl.when(i >= nv_ref[0])
    def _():
        o_ref[...] = jnp.zeros_like(o_ref)


def _experts(xb, block_e, n_valid, w1, w3, w2):
    n_blocks = xb.shape[0] // MOE_BM
    return pl.pallas_call(
        _expert_body,
        grid_spec=pltpu.PrefetchScalarGridSpec(
            num_scalar_prefetch=2,
            grid=(n_blocks,),
            in_specs=[pl.BlockSpec((MOE_BM, D_MODEL), lambda i, be, nv: (i, 0)),
                      pl.BlockSpec((1, D_MODEL, D_EXPERT), lambda i, be, nv: (be[i], 0, 0)),
                      pl.BlockSpec((1, D_MODEL, D_EXPERT), lambda i, be, nv: (be[i], 0, 0)),
                      pl.BlockSpec((1, D_EXPERT, D_MODEL), lambda i, be, nv: (be[i], 0, 0))],
            out_specs=pl.BlockSpec((MOE_BM, D_MODEL), lambda i, be, nv: (i, 0))),
        out_shape=jax.ShapeDtypeStruct((xb.shape[0], D_MODEL), F32),
        compiler_params=_cparams(("arbitrary",)),
        name="moe_experts",
    )(block_e, n_valid, xb, w1, w3, w2)


def _combine_body(x_ref, y0_ref, y1_ref, w_ref, mod_ref, o_ref):
    w = w_ref[...]
    mo = w[:, 0:1] * y0_ref[...] + w[:, 1:2] * y1_ref[...]
    o_ref[...] = x_ref[...] + mod_ref[0, 5:6, :] * mo


def _combine(x, y0, y1, wts, mod3, n_b, n_seq):
    rows = x.shape[0]
    big = pl.BlockSpec((ROW_TILE, D_MODEL), lambda i: (i, 0))
    return pl.pallas_call(
        _combine_body,
        grid=(rows // ROW_TILE,),
        in_specs=[big, big, big, pl.BlockSpec((ROW_TILE, LANES), lambda i: (i, 0)),
                  pl.BlockSpec((1, 6, D_MODEL), lambda i: (_mod_row(i, ROW_TILE, n_b, n_seq), 0, 0))],
        out_specs=big,
        out_shape=jax.ShapeDtypeStruct((rows, D_MODEL), F32),
        compiler_params=_cparams(("arbitrary",)),
        name="moe_combine",
    )(x, y0, y1, wts, mod3)


def _moe(x, h2, logits, mod3, w1, w3, w2, n_b, n_seq):
    rows = x.shape[0]
    lg = logits[:, :N_EGROUPS]
    le = logits[:, N_EGROUPS:N_EGROUPS + N_EXPERTS].reshape(rows, N_EGROUPS, EXP_PER_GROUP)
    gsel = jnp.argmax(lg, axis=-1)
    gate_g = jnp.max(jax.nn.softmax(lg, axis=-1), axis=-1)
    le_sel = jnp.take_along_axis(le, gsel[:, None, None], axis=1)[:, 0]
    top_p, top_i = lax.top_k(jax.nn.softmax(le_sel, axis=-1), 2)
    wts = gate_g[:, None] * top_p / jnp.sum(top_p, axis=-1, keepdims=True)
    eid = (gsel[:, None] * EXP_PER_GROUP + top_i).astype(jnp.int32)
    n_asg = rows * 2
    e_flat = eid.reshape(n_asg)
    order = jnp.argsort(e_flat)
    se = e_flat[order]
    stok = (order // 2).astype(jnp.int32)
    counts = jnp.bincount(e_flat, length=N_EXPERTS)
    start = jnp.cumsum(counts) - counts
    pcounts = ((counts + MOE_BM - 1) // MOE_BM) * MOE_BM
    pend = jnp.cumsum(pcounts)
    pstart = pend - pcounts
    dest = (pstart[se] + (jnp.arange(n_asg) - start[se])).astype(jnp.int32)
    n_blocks = -(-n_asg // MOE_BM) + N_EXPERTS
    slot_tok = jnp.zeros((n_blocks * MOE_BM,), jnp.int32).at[dest].set(stok)
    xb = jnp.take(h2, slot_tok, axis=0)
    blk_start = jnp.arange(n_blocks) * MOE_BM
    block_e = jnp.minimum(jnp.searchsorted(pend, blk_start, side='right'), N_EXPERTS - 1).astype(jnp.int32)
    n_valid = (pend[-1] // MOE_BM).astype(jnp.int32).reshape(1)
    yb = _experts(xb, block_e, n_valid, w1, w3, w2)
    slot_of = jnp.zeros((n_asg,), jnp.int32).at[order].set(dest).reshape(rows, 2)
    y0 = jnp.take(yb, slot_of[:, 0], axis=0)
    y1 = jnp.take(yb, slot_of[:, 1], axis=0)
    wpad = jnp.zeros((rows, LANES), F32).at[:, :2].set(wts)
    return _combine(x, y0, y1, wpad, mod3, n_b, n_seq)


def _pad_rows(w, lo, n):
    pad = [(0, 0)] * (w.ndim - 2) + [(lo, n - lo - w.shape[-2]), (0, 0)]
    return jnp.pad(w.astype(F32), pad).astype(BF16)


def kernel(x, c, ctx, c_ctx, w_ada, b_ada, g_norm1, g_norm2, w_in, w_out, s5_lam_re, s5_lam_im, s5_log_step, s5_b_re, s5_b_im, s5_c_re, s5_c_im, s5_d, s5_w_glu, s5_b_glu, lru_conv_w, lru_conv_b, lru_lam, lru_wa, lru_ba, lru_wx, lru_bx, diff_gq, diff_gk, diff_lq1, diff_lk1, diff_lq2, diff_lk2, diff_subln, rw_mu, rw_w0, rw_w2, rw_a0, rw_a2, rw_g2, rw_kk, rw_ka, rw_rk, rw_lnx_g, rw_lnx_b, moe_w_rg, moe_b_rg, moe_w_re, moe_b_re, moe_w1, moe_w3, moe_w2):
    n_b, n_seq, _ = x.shape
    n_ctx = ctx.shape[1]
    depth = w_ada.shape[0]
    assert n_seq % SEQ_TILE == 0 and n_ctx % SEQ_TILE == 0 and n_seq % n_ctx == 0
    assert (n_b * n_ctx) % ROW_TILE == 0 and n_seq % ROW_TILE == 0 and n_seq % ATT_TK == 0
    xs = jnp.concatenate([x.reshape(n_b * n_seq, D_MODEL), ctx.reshape(n_b * n_ctx, D_MODEL)], axis=0)
    rows = xs.shape[0]
    cvec = jnp.zeros((SUBLANES, D_MODEL), F32).at[0].set(c_ctx).at[1:1 + n_b].set(c)
    rope = _rope_tables(n_b, n_seq, n_ctx)
    seg = jnp.arange(W_GROUP) // RWKV_N
    e_seg = (seg[:, None] == seg[None, :]).astype(BF16)
    nl_s, nc_s = n_seq // SEQ_TILE, n_ctx // SEQ_TILE
    nl_r, nc_r = n_seq // RW_CHUNK, n_ctx // RW_CHUNK

    for l in range(depth):
        lam_init = 0.8 - 0.6 * math.exp(-0.3 * l)
        mod = _modulation(cvec, w_ada[l], b_ada[l])
        mod3 = mod[:1 + n_b].reshape(1 + n_b, 6, D_MODEL)
        w_in_b = jnp.pad(w_in[l].astype(BF16), ((0, 0), (0, D_IN_PAD - D_IN)))
        za = _inproj(xs, mod3, g_norm1[l], w_in_b, n_b, n_seq)
        s5p = _s5_prepare(s5_lam_re[l], s5_lam_im[l], s5_log_step[l], s5_b_re[l], s5_b_im[l],
                          s5_c_re[l], s5_c_im[l])
        y5 = [_s5_scan(za, s5p, d, n_b, nl_s, nc_s) for d in range(2)]
        ya = _s5_finish(y5[0], y5[1], za, s5_d[l], s5_w_glu[l], s5_b_glu[l])
        lrp = dict(cw=lru_conv_w[l], cb=lru_conv_b[l].reshape(1, W_GROUP),
                   wa=_blockdiag(lru_wa[l]).astype(BF16), wx=_blockdiag(lru_wx[l]).astype(BF16),
                   ba=lru_ba[l].reshape(2, 1, W_GROUP), bx=lru_bx[l].reshape(2, 1, W_GROUP),
                   sp=jax.nn.softplus(-lru_lam[l]).reshape(2, 1, W_GROUP))
        hl = [_lru_scan(za, lrp, d, n_b, nl_s, nc_s) for d in range(2)]
        yb_ = _lru_finish(hl[0], hl[1], za)
        lam = jnp.exp(jnp.sum(diff_lq1[l] * diff_lk1[l])) - jnp.exp(jnp.sum(diff_lq2[l] * diff_lk2[l])) + lam_init
        lam_vec = jnp.full((1, LANES), lam, F32)
        qn, kn, vn = _qkv_prep(za, rope, diff_gq[l], diff_gk[l], e_seg)
        o_lat, o_ctx = _attention(qn, kn, vn, lam_vec, diff_subln[l], lam_init, n_b, n_seq, n_ctx)
        yc = jnp.concatenate([o_lat, o_ctx], axis=0)
        lo_w, lo_a = RANK_G, RANK_G + RANK_W
        rwp = dict(mu=rw_mu[l].reshape(1, RWKV_COLS),
                   w2=_pad_rows(rw_w2[l], lo_w, LANES), a2=_pad_rows(rw_a2[l], lo_a, LANES),
                   g2=_pad_rows(rw_g2[l], 0, LANES), w0=rw_w0[l], a0=rw_a0[l],
                   kk=rw_kk[l].reshape(1, W_GROUP), ka=rw_ka[l].reshape(1, W_GROUP),
                   rk=rw_rk[l].reshape(1, W_GROUP))
        pre = _rwkv_prep(za, rwp, e_seg, n_b, nl_s, nc_s)
        yr = _rwkv_scan(pre, n_b, nl_r, nc_r)
        yd = _rwkv_finish(yr[0], yr[1], pre[9], pre[10], rw_lnx_g[l], rw_lnx_b[l], e_seg)
        wr = jnp.zeros((D_MODEL, LANES), F32).at[:, :N_EGROUPS].set(moe_w_rg[l])
        wr = wr.at[:, N_EGROUPS:N_EGROUPS + N_EXPERTS].set(moe_w_re[l])
        wr_hi, wr_lo = _split2(wr)
        br = jnp.zeros((1, LANES), F32).at[0, :N_EGROUPS].set(moe_b_rg[l])
        br = br.at[0, N_EGROUPS:N_EGROUPS + N_EXPERTS].set(moe_b_re[l])
        xs, h2, logits = _outproj(xs, (ya, yb_, yc, yd), w_out[l].astype(BF16), mod3, g_norm2[l],
                                  wr_hi, wr_lo, br, n_b, n_seq)
        xs = _moe(xs, h2, logits, mod3, moe_w1[l].astype(BF16), moe_w3[l].astype(BF16), moe_w2[l].astype(BF16),
                  n_b, n_seq)
    return xs[:n_b * n_seq].reshape(n_b, n_seq, D_MODEL)
```

```python
import functools
import math

import jax
import jax.numpy as jnp
from jax import lax
from jax.experimental import pallas as pl
from jax.experimental.pallas import tpu as pltpu

F32 = jnp.float32
BF16 = jnp.bfloat16

D_MODEL = 2048
W_GROUP = 512
S5_CH, S5_GROUPS, S5_STATE = 16, 32, 64
S5_NS = S5_GROUPS * S5_STATE
LRU_C = 8.0
DIFF_HEADS, DIFF_D = 4, 64
ROPE_AX = 16
ROPE_BASE = 10000.0
RWKV_N, RWKV_HEADS = 64, 8
RANK_G, RANK_W, RANK_A = 64, 32, 32
RWKV_COLS = 3 * W_GROUP + RANK_G + RANK_W + RANK_A
N_EGROUPS, EXP_PER_GROUP, N_EXPERTS = 4, 8, 32
D_EXPERT = 1024
NORM_EPS = 1e-6
GN_EPS = 64e-5
GRID_W = 64

SUBLANES = 8
LANES = 128
ROW_TILE = 512
SEQ_TILE = 256
RW_CHUNK = 64
ATT_TQ = 256
ATT_UNROLL = 8
ATT_TK = 512
ATT_RB = 128
MOE_BM = 256
OUT_TILE = 256
IN_TN = 1024
D_IN = 6 * W_GROUP + RWKV_COLS
D_IN_PAD = -(-D_IN // IN_TN) * IN_TN
COL_R, COL_K, COL_V = 6, 7, 8
COL_LOW = (9 * W_GROUP) // LANES
VMEM_LIMIT = 56 * 1024 * 1024


def _cparams(sem):
    return pltpu.CompilerParams(dimension_semantics=sem, vmem_limit_bytes=VMEM_LIMIT)


def _split2(x):
    hi = x.astype(BF16)
    lo = (x - hi.astype(F32)).astype(BF16)
    return hi, lo


def _dot(a, b):
    return jnp.dot(a, b, preferred_element_type=F32)


def _dot_nt(a, b):
    return lax.dot_general(a, b, (((1,), (1,)), ((), ())), preferred_element_type=F32)


def _dot_tn(a, b):
    return lax.dot_general(a, b, (((0,), (0,)), ((), ())), preferred_element_type=F32)


def _dot3(a, b_hi, b_lo):
    a_hi, a_lo = _split2(a)
    return _dot(a_hi, b_hi) + _dot(a_hi, b_lo) + _dot(a_lo, b_hi)


def _segsum(x, e_ref):
    hi, lo = _split2(x)
    e = e_ref[...]
    return _dot(hi, e) + _dot(lo, e)


def _seq_block(rev, b, i, n_c, n_l, n_b):
    if rev:
        ctx = n_b * n_l + b * n_c + (n_c - 1 - i)
        lat = b * n_l + (n_l - 1 - (i - n_c))
    else:
        ctx = n_b * n_l + b * n_c + i
        lat = b * n_l + (i - n_c)
    return jnp.where(i < n_c, ctx, lat)


def _seg_pos(t, n_b, n_l, n_c):
    is_lat = t < n_b * n_l
    p = jnp.where(is_lat, t % n_l, (t - n_b * n_l) % n_c)
    n = jnp.where(is_lat, n_l, n_c)
    return p, n


def _mod_body(c_ref, w_ref, b_ref, o_ref):
    c = c_ref[...]
    s = c * jax.nn.sigmoid(c)
    w_hi, w_lo = _split2(w_ref[...])
    o_ref[...] = _dot3(s, w_hi, w_lo) + b_ref[...]


def _modulation(cvec, w_ada, b_ada):
    d6 = w_ada.shape[1]
    tn = 1536
    return pl.pallas_call(
        _mod_body,
        grid=(d6 // tn,),
        in_specs=[pl.BlockSpec((SUBLANES, D_MODEL), lambda j: (0, 0)),
                  pl.BlockSpec((D_MODEL, tn), lambda j: (0, j)),
                  pl.BlockSpec((1, tn), lambda j: (0, j))],
        out_specs=pl.BlockSpec((SUBLANES, tn), lambda j: (0, j)),
        out_shape=jax.ShapeDtypeStruct((SUBLANES, d6), F32),
        compiler_params=_cparams(("arbitrary",)),
        name="adaln_mod",
    )(cvec, w_ada, b_ada.reshape(1, d6))


def _mod_row(i, tile, n_b, n_seq):
    return jnp.where(i < n_b * n_seq // tile, 1 + i // (n_seq // tile), 0)


def _inproj_body(x_ref, mod_ref, g_ref, w_ref, z_ref, h_ref):
    @pl.when(pl.program_id(1) == 0)
    def _():
        x = x_ref[...]
        ms = jnp.mean(x * x, axis=-1, keepdims=True)
        xn = x * lax.rsqrt(ms + NORM_EPS) * g_ref[...]
        h_ref[...] = (xn * (1.0 + mod_ref[0, 1:2, :]) + mod_ref[0, 0:1, :]).astype(BF16)

    z_ref[...] = _dot(h_ref[...], w_ref[...])


def _inproj(x, mod3, g1, w, n_b, n_seq):
    rows = x.shape[0]
    tn = IN_TN
    return pl.pallas_call(
        _inproj_body,
        grid=(rows // ROW_TILE, w.shape[1] // tn),
        in_specs=[pl.BlockSpec((ROW_TILE, D_MODEL), lambda i, j: (i, 0)),
                  pl.BlockSpec((1, 6, D_MODEL), lambda i, j: (_mod_row(i, ROW_TILE, n_b, n_seq), 0, 0)),
                  pl.BlockSpec((1, D_MODEL), lambda i, j: (0, 0)),
                  pl.BlockSpec((D_MODEL, tn), lambda i, j: (0, j))],
        out_specs=pl.BlockSpec((ROW_TILE, tn), lambda i, j: (i, j)),
        out_shape=jax.ShapeDtypeStruct((rows, w.shape[1]), F32),
        scratch_shapes=[pltpu.VMEM((ROW_TILE, D_MODEL), BF16)],
        compiler_params=_cparams(("arbitrary", "arbitrary")),
        name="norm_inproj",
    )(x, mod3, g1.reshape(1, D_MODEL), w)


def _s5_body(rev, u_ref, bre_ref, bim_ref, t1r_ref, t1i_ref, p2r_ref, p2i_ref, cre_ref, cim_ref,
             y_ref, sr_ref, si_ref, car_ref):
    g_n = SEQ_TILE // SUBLANES

    @pl.when(pl.program_id(1) == 0)
    def _():
        car_ref[...] = jnp.zeros_like(car_ref)

    u = u_ref[...].astype(BF16)
    xr = _dot(u, bre_ref[0]).reshape(g_n, SUBLANES, S5_NS)
    xi = _dot(u, bim_ref[0]).reshape(g_n, SUBLANES, S5_NS)
    for lvl, s in enumerate((1, 2, 4)):
        sh = (SUBLANES - s) if rev else s
        pr = pltpu.roll(xr, sh, axis=1)
        pi = pltpu.roll(xi, sh, axis=1)
        ar = t1r_ref[0, lvl][None]
        ai = t1i_ref[0, lvl][None]
        xr, xi = xr + ar * pr - ai * pi, xi + ar * pi + ai * pr
    sr_ref[...] = xr
    si_ref[...] = xi
    p2r = p2r_ref[0]
    p2i = p2i_ref[0]
    row = 0 if rev else SUBLANES - 1

    def step(g, _):
        gg = (g_n - 1 - g) if rev else g
        cr = car_ref[0]
        ci = car_ref[1]
        nr = sr_ref[gg] + p2r * cr - p2i * ci
        ni = si_ref[gg] + p2r * ci + p2i * cr
        sr_ref[gg] = nr
        si_ref[gg] = ni
        car_ref[0] = jnp.broadcast_to(nr[row:row + 1, :], (SUBLANES, S5_NS))
        car_ref[1] = jnp.broadcast_to(ni[row:row + 1, :], (SUBLANES, S5_NS))
        return 0

    lax.fori_loop(0, g_n, step, 0)
    s_r = sr_ref[...].reshape(SEQ_TILE, S5_NS).astype(BF16)
    s_i = si_ref[...].reshape(SEQ_TILE, S5_NS).astype(BF16)
    y_ref[...] = _dot(s_r, cre_ref[0]) - _dot(s_i, cim_ref[0])


def _s5_scan(za, p, d, n_b, n_l, n_c):
    rev = d == 1
    rows = za.shape[0]
    seq = lambda b, i: (_seq_block(rev, b, i, n_c, n_l, n_b), 0)
    cst3 = lambda b, i: (d, 0, 0)
    cst4 = lambda b, i: (d, 0, 0, 0)
    return pl.pallas_call(
        functools.partial(_s5_body, rev),
        grid=(n_b, n_c + n_l),
        in_specs=[pl.BlockSpec((SEQ_TILE, W_GROUP), seq),
                  pl.BlockSpec((1, W_GROUP, S5_NS), cst3),
                  pl.BlockSpec((1, W_GROUP, S5_NS), cst3),
                  pl.BlockSpec((1, 3, SUBLANES, S5_NS), cst4),
                  pl.BlockSpec((1, 3, SUBLANES, S5_NS), cst4),
                  pl.BlockSpec((1, SUBLANES, S5_NS), cst3),
                  pl.BlockSpec((1, SUBLANES, S5_NS), cst3),
                  pl.BlockSpec((1, S5_NS, W_GROUP), cst3),
                  pl.BlockSpec((1, S5_NS, W_GROUP), cst3)],
        out_specs=pl.BlockSpec((SEQ_TILE, W_GROUP), seq),
        out_shape=jax.ShapeDtypeStruct((rows, W_GROUP), F32),
        scratch_shapes=[pltpu.VMEM((SEQ_TILE // SUBLANES, SUBLANES, S5_NS), F32),
                        pltpu.VMEM((SEQ_TILE // SUBLANES, SUBLANES, S5_NS), F32),
                        pltpu.VMEM((2, SUBLANES, S5_NS), F32)],
        compiler_params=_cparams(("arbitrary", "arbitrary")),
        name="s5_scan_rev" if rev else "s5_scan_fwd",
    )(za, p["bre"], p["bim"], p["t1r"], p["t1i"], p["p2r"], p["p2i"], p["cre"], p["cim"])


def _s5_prepare(lam_re, lam_im, log_step, b_re, b_im, c_re, c_im):
    lr, li = lam_re.astype(F32), lam_im.astype(F32)
    dt = jnp.exp(log_step.astype(F32))[..., None]
    mag = jnp.exp(lr * dt)
    ar, ai = mag * jnp.cos(li * dt), mag * jnp.sin(li * dt)
    den = lr * lr + li * li
    cr = ((ar - 1.0) * lr + ai * li) / den
    ci = (ai * lr - (ar - 1.0) * li) / den
    bbr = cr[..., None] * b_re - ci[..., None] * b_im
    bbi = cr[..., None] * b_im + ci[..., None] * b_re
    eye = jnp.eye(S5_GROUPS, dtype=F32)
    bd = lambda t: jnp.einsum('dgph,gk->dghkp', t, eye).reshape(2, W_GROUP, S5_NS).astype(BF16)
    cd = lambda t: jnp.einsum('dghp,gk->dgpkh', t.astype(F32), eye).reshape(2, S5_NS, W_GROUP).astype(BF16)
    ar = ar.reshape(2, S5_NS)
    ai = ai.reshape(2, S5_NS)
    pw_r, pw_i = [ar], [ai]
    for _ in range(SUBLANES - 1):
        pw_r.append(pw_r[-1] * ar - pw_i[-1] * ai)
        pw_i.append(pw_r[-2] * ai + pw_i[-1] * ar)
    k = jnp.arange(SUBLANES)[None, :, None]
    t1r, t1i, p2r, p2i = [], [], [], []
    for d in range(2):
        lr_, li_ = [], []
        for s in (1, 2, 4):
            m = (k + s <= SUBLANES - 1) if d == 1 else (k >= s)
            lr_.append(jnp.where(m, pw_r[s - 1][d][None, None, :], 0.0)[0])
            li_.append(jnp.where(m, pw_i[s - 1][d][None, None, :], 0.0)[0])
        t1r.append(jnp.stack(lr_))
        t1i.append(jnp.stack(li_))
        order = range(SUBLANES - 1, -1, -1) if d == 1 else range(SUBLANES)
        p2r.append(jnp.stack([pw_r[j][d] for j in order]))
        p2i.append(jnp.stack([pw_i[j][d] for j in order]))
    return dict(bre=bd(bbr), bim=bd(bbi), cre=cd(c_re), cim=cd(c_im),
                t1r=jnp.stack(t1r), t1i=jnp.stack(t1i), p2r=jnp.stack(p2r), p2i=jnp.stack(p2i))


def _s5_finish_body(yf_ref, yb_ref, u_ref, d_ref, w_ref, b_ref, o_ref):
    y = jax.nn.gelu(yf_ref[...] + yb_ref[...] + d_ref[...] * u_ref[...])
    gate = jax.nn.sigmoid(_dot(y.astype(BF16), w_ref[...]) + b_ref[...])
    o_ref[...] = (y * gate).astype(BF16)


def _s5_finish(yf, yb, za, d_skip, w_glu, b_glu):
    rows = yf.shape[0]
    blk = pl.BlockSpec((ROW_TILE, W_GROUP), lambda i: (i, 0))
    vec = pl.BlockSpec((1, W_GROUP), lambda i: (0, 0))
    return pl.pallas_call(
        _s5_finish_body,
        grid=(rows // ROW_TILE,),
        in_specs=[blk, blk, blk, vec, pl.BlockSpec((W_GROUP, W_GROUP), lambda i: (0, 0)), vec],
        out_specs=blk,
        out_shape=jax.ShapeDtypeStruct((rows, W_GROUP), BF16),
        compiler_params=_cparams(("arbitrary",)),
        name="s5_finish",
    )(yf, yb, za, d_skip.reshape(1, W_GROUP), w_glu.astype(BF16), b_glu.reshape(1, W_GROUP))


HALO = SUBLANES


def _halo_maps(tile_fn, rows, col):
    per = SEQ_TILE // HALO
    last = rows // HALO - 1
    prv = lambda *ids: (jnp.maximum(tile_fn(*ids) * per - 1, 0), col)
    nxt = lambda *ids: (jnp.minimum((tile_fn(*ids) + 1) * per, last), col)
    return prv, nxt


def _lru_body(rev, n_b, n_l, n_c, xp_ref, xc_ref, xn_ref, cw_ref, cb_ref, wa_ref, ba_ref, wx_ref, bx_ref,
              sp_ref, h_ref, buf_ref, a_ref, b_ref, car_ref):
    g_n = SEQ_TILE // SUBLANES
    b = pl.program_id(0)
    i = pl.program_id(1)

    @pl.when(i == 0)
    def _():
        car_ref[...] = jnp.zeros_like(car_ref)

    p, n = _seg_pos(_seq_block(rev, b, i, n_c, n_l, n_b), n_b, n_l, n_c)
    prev_ok = (p > 0).astype(F32)
    next_ok = (p < n - 1).astype(F32)
    buf_ref[0:HALO, :] = xp_ref[...] * prev_ok
    buf_ref[HALO:HALO + SEQ_TILE, :] = xc_ref[...]
    buf_ref[HALO + SEQ_TILE:, :] = xn_ref[...] * next_ok
    v = cb_ref[...] + cw_ref[2:3, :] * xc_ref[...]
    v = v + cw_ref[0:1, :] * buf_ref[HALO - 2:HALO - 2 + SEQ_TILE, :]
    v = v + cw_ref[1:2, :] * buf_ref[HALO - 1:HALO - 1 + SEQ_TILE, :]
    v = v + cw_ref[3:4, :] * buf_ref[HALO + 1:HALO + 1 + SEQ_TILE, :]
    vb = v.astype(BF16)
    r = jax.nn.sigmoid(_dot(vb, wa_ref[0]) + ba_ref[0])
    ig = jax.nn.sigmoid(_dot(vb, wx_ref[0]) + bx_ref[0])
    a = jnp.exp(-LRU_C * r * sp_ref[0])
    bb = jnp.sqrt(1.0 - a * a) * (ig * v)
    a3 = a.reshape(g_n, SUBLANES, W_GROUP)
    b3 = bb.reshape(g_n, SUBLANES, W_GROUP)
    k = lax.broadcasted_iota(jnp.int32, (g_n, SUBLANES, W_GROUP), 1)
    for s in (1, 2, 4):
        sh = (SUBLANES - s) if rev else s
        m = (k + s <= SUBLANES - 1) if rev else (k >= s)
        a_s = pltpu.roll(a3, sh, axis=1)
        b_s = pltpu.roll(b3, sh, axis=1)
        b3 = jnp.where(m, a3 * b_s + b3, b3)
        a3 = jnp.where(m, a3 * a_s, a3)
    a_ref[...] = a3
    b_ref[...] = b3
    row = 0 if rev else SUBLANES - 1

    def step(g, _):
        gg = (g_n - 1 - g) if rev else g
        hh = b_ref[gg] + a_ref[gg] * car_ref[...]
        b_ref[gg] = hh
        car_ref[...] = jnp.broadcast_to(hh[row:row + 1, :], (SUBLANES, W_GROUP))
        return 0

    lax.fori_loop(0, g_n, step, 0)
    h_ref[...] = b_ref[...].reshape(SEQ_TILE, W_GROUP)


def _lru_scan(za, p, d, n_b, n_l, n_c):
    rev = d == 1
    rows = za.shape[0]
    tile = lambda b, i: _seq_block(rev, b, i, n_c, n_l, n_b)
    cur = lambda b, i: (tile(b, i), 1)
    prv, nxt = _halo_maps(tile, rows, 1)
    out = lambda b, i: (tile(b, i), 0)
    vec = pl.BlockSpec((1, W_GROUP), lambda b, i: (0, 0))
    dvec = pl.BlockSpec((1, 1, W_GROUP), lambda b, i: (d, 0, 0))
    dmat = pl.BlockSpec((1, W_GROUP, W_GROUP), lambda b, i: (d, 0, 0))
    blk = lambda f: pl.BlockSpec((SEQ_TILE, W_GROUP), f)
    halo = lambda f: pl.BlockSpec((HALO, W_GROUP), f)
    g_n = SEQ_TILE // SUBLANES
    return pl.pallas_call(
        functools.partial(_lru_body, rev, n_b, n_l, n_c),
        grid=(n_b, n_c + n_l),
        in_specs=[halo(prv), blk(cur), halo(nxt),
                  pl.BlockSpec((4, W_GROUP), lambda b, i: (0, 0)), vec,
                  dmat, dvec, dmat, dvec, dvec],
        out_specs=blk(out),
        out_shape=jax.ShapeDtypeStruct((rows, W_GROUP), F32),
        scratch_shapes=[pltpu.VMEM((SEQ_TILE + 2 * HALO, W_GROUP), F32),
                        pltpu.VMEM((g_n, SUBLANES, W_GROUP), F32),
                        pltpu.VMEM((g_n, SUBLANES, W_GROUP), F32),
                        pltpu.VMEM((SUBLANES, W_GROUP), F32)],
        compiler_params=_cparams(("arbitrary", "arbitrary")),
        name="lru_scan_rev" if rev else "lru_scan_fwd",
    )(za, za, za, p["cw"], p["cb"], p["wa"], p["ba"], p["wx"], p["bx"], p["sp"])


def _blockdiag(w):
    nb, c = w.shape[1], w.shape[2]
    eye = jnp.eye(nb, dtype=F32)
    return jnp.einsum('dncf,nm->dncmf', w.astype(F32), eye).reshape(2, nb * c, nb * c)


def _lru_finish_body(hf_ref, hb_ref, g_ref, o_ref):
    o_ref[...] = ((hf_ref[...] + hb_ref[...]) * jax.nn.gelu(g_ref[...])).astype(BF16)


def _lru_finish(hf, hb, za):
    rows = hf.shape[0]
    blk = pl.BlockSpec((ROW_TILE, W_GROUP), lambda i: (i, 0))
    return pl.pallas_call(
        _lru_finish_body,
        grid=(rows // ROW_TILE,),
        in_specs=[blk, blk, pl.BlockSpec((ROW_TILE, W_GROUP), lambda i: (i, 2))],
        out_specs=blk,
        out_shape=jax.ShapeDtypeStruct((rows, W_GROUP), BF16),
        compiler_params=_cparams(("arbitrary",)),
        name="lru_finish",
    )(hf, hb, za)


def _qkv_prep_body(n_lat_tiles, q_ref, k_ref, v_ref, cos_ref, s1_ref, s2_ref, gq_ref, gk_ref, e_ref,
                   qo_ref, ko_ref, vo_ref):
    reps = W_GROUP // LANES
    is_ctx = pl.program_id(0) >= n_lat_tiles
    cos = jnp.concatenate([jnp.where(is_ctx, 1.0, cos_ref[...])] * reps, axis=1)
    s1 = jnp.concatenate([jnp.where(is_ctx, 0.0, s1_ref[...])] * reps, axis=1)
    s2 = jnp.concatenate([jnp.where(is_ctx, 0.0, s2_ref[...])] * reps, axis=1)

    def prep(x, g):
        ms = _segsum(x * x, e_ref) * (1.0 / DIFF_D)
        x = x * lax.rsqrt(ms + NORM_EPS) * g
        return x * cos + pltpu.roll(x, ROPE_AX, axis=1) * s1 + pltpu.roll(x, W_GROUP - ROPE_AX, axis=1) * s2

    qo_ref[...] = (prep(q_ref[...], gq_ref[...]) * (DIFF_D ** -0.5)).astype(BF16)
    ko_ref[...] = prep(k_ref[...], gk_ref[...]).astype(BF16)
    vo_ref[...] = v_ref[...].astype(BF16)


def _qkv_prep(za, rope, gq, gk, e_seg, n_b, n_seq):
    rows = za.shape[0]
    per_batch = n_seq // ROW_TILE
    col = lambda c: pl.BlockSpec((ROW_TILE, W_GROUP), lambda i: (i, c))
    tab = pl.BlockSpec((ROW_TILE, LANES), lambda i: (i % per_batch, 0))
    vec = pl.BlockSpec((1, W_GROUP), lambda i: (0, 0))
    out = pl.BlockSpec((ROW_TILE, W_GROUP), lambda i: (i, 0))
    reps = W_GROUP // DIFF_D
    return pl.pallas_call(
        functools.partial(_qkv_prep_body, n_b * per_batch),
        grid=(rows // ROW_TILE,),
        in_specs=[col(3), col(4), col(5), tab, tab, tab, vec, vec,
                  pl.BlockSpec((W_GROUP, W_GROUP), lambda i: (0, 0))],
        out_specs=[out, out, out],
        out_shape=[jax.ShapeDtypeStruct((rows, W_GROUP), BF16)] * 3,
        compiler_params=_cparams(("arbitrary",)),
        name="attn_qkv_prep",
    )(za, za, za, rope[0], rope[1], rope[2], jnp.tile(gq, reps).reshape(1, W_GROUP),
      jnp.tile(gk, reps).reshape(1, W_GROUP), e_seg)


def _attn_body(seg_lens, out_scale, lam_ref, sub_ref, q_ref, *rest):
    o_ref, q2_ref, m_ref, acc_ref, sa_ref, sb_ref = rest[-6:]
    kv = rest[:-6]
    tq = q_ref.shape[0]
    q = q_ref[...]
    lane = lax.broadcasted_iota(jnp.int32, (tq, LANES), 1)
    zero = jnp.zeros_like(q)
    q2 = jnp.concatenate([jnp.where(lane < DIFF_D, q, zero), jnp.where(lane < DIFF_D, zero, q)], axis=0)
    q2_ref[...] = q2
    m_ref[...] = jnp.full_like(m_ref, -jnp.inf)
    acc_ref[...] = jnp.zeros_like(acc_ref)
    n_rb = (2 * tq) // ATT_RB

    rows = [slice(rb * ATT_RB, (rb + 1) * ATT_RB) for rb in range(n_rb)]

    def scores(kc, s_ref):
        for rs in rows:
            s_ref[rs, 0:kc.shape[0]] = _dot_nt(q2_ref[rs, :], kc)

    def softmax_pv(s_ref, vc):
        tk = vc.shape[0]
        v_ext = jnp.concatenate([vc, jnp.ones_like(vc)], axis=1)
        for rs in rows:
            s = s_ref[rs, 0:tk]
            m_old = m_ref[rs, :]
            m_new = jnp.maximum(m_old, jnp.max(s, axis=-1, keepdims=True))
            alpha = jnp.exp(m_old - m_new)
            pm = jnp.exp((s - jnp.concatenate([m_new] * (tk // LANES), axis=1)).astype(BF16))
            acc_ref[rs, :] = jnp.concatenate([alpha, alpha], axis=1) * acc_ref[rs, :] + _dot(pm, v_ext)
            m_ref[rs, :] = m_new

    for si, n_k in enumerate(seg_lens):
        k_ref, v_ref = kv[2 * si], kv[2 * si + 1]
        tk = min(ATT_TK, n_k)
        n_ch = n_k // tk
        if n_ch < 2:
            for j in range(n_ch):
                scores(k_ref[j * tk:(j + 1) * tk, :], sa_ref)
                softmax_pv(sa_ref, v_ref[j * tk:(j + 1) * tk, :])
            continue
        unroll = min(ATT_UNROLL, n_ch)
        assert unroll % 2 == 0 and n_ch % unroll == 0
        chunk = lambda ref, j, tk=tk: ref[pl.ds(pl.multiple_of(j * tk, tk), tk), :]
        scores(chunk(k_ref, 0), sa_ref)

        def body(jj, _, k_ref=k_ref, v_ref=v_ref, n_ch=n_ch, unroll=unroll):
            for u in range(unroll):
                j = unroll * jj + u
                cur, nxt = (sa_ref, sb_ref) if u % 2 == 0 else (sb_ref, sa_ref)
                scores(chunk(k_ref, jnp.minimum(j + 1, n_ch - 1)), nxt)
                softmax_pv(cur, chunk(v_ref, j))
            return 0

        lax.fori_loop(0, n_ch // unroll, body, 0)

    o = acc_ref[:, 0:LANES] / acc_ref[:, LANES:]
    o = o[:tq] - lam_ref[...] * o[tq:]
    ms = jnp.mean(o * o, axis=-1, keepdims=True)
    o_ref[...] = (o * lax.rsqrt(ms + NORM_EPS) * sub_ref[...] * out_scale).astype(BF16)


def _attention(qn, kn, vn, lam_vec, subln, lam_init, n_b, n_seq, n_ctx):
    vec = lambda nd: pl.BlockSpec((1, LANES), (lambda b, h, i: (0, 0)) if nd == 3 else (lambda b, h: (0, 0)))
    scratch = lambda tq: [pltpu.VMEM((2 * tq, LANES), BF16), pltpu.VMEM((2 * tq, LANES), F32),
                          pltpu.VMEM((2 * tq, 2 * LANES), F32),
                          pltpu.VMEM((2 * tq, ATT_TK), F32), pltpu.VMEM((2 * tq, ATT_TK), F32)]
    sub = subln.reshape(1, LANES)
    ctx_blk0 = n_b * n_seq // n_ctx
    qpb = n_seq // ATT_TQ
    lat_kv = pl.BlockSpec((n_seq, LANES), lambda b, h, i: (b, h))
    ctx_kv = pl.BlockSpec((n_ctx, LANES), lambda b, h, i: (ctx_blk0 + b, h))
    o_lat = pl.pallas_call(
        functools.partial(_attn_body, (n_seq, n_ctx), 1.0 - lam_init),
        grid=(n_b, DIFF_HEADS, qpb),
        in_specs=[vec(3), vec(3), pl.BlockSpec((ATT_TQ, LANES), lambda b, h, i: (b * qpb + i, h)),
                  lat_kv, lat_kv, ctx_kv, ctx_kv],
        out_specs=pl.BlockSpec((ATT_TQ, LANES), lambda b, h, i: (b * qpb + i, h)),
        out_shape=jax.ShapeDtypeStruct((n_b * n_seq, W_GROUP), BF16),
        scratch_shapes=scratch(ATT_TQ),
        compiler_params=_cparams(("arbitrary", "arbitrary", "arbitrary")),
        name="diff_attn_latent",
    )(lam_vec, sub, qn, kn, vn, kn, vn)
    ckv = pl.BlockSpec((n_ctx, LANES), lambda b, h: (ctx_blk0 + b, h))
    o_ctx = pl.pallas_call(
        functools.partial(_attn_body, (n_ctx,), 1.0 - lam_init),
        grid=(n_b, DIFF_HEADS),
        in_specs=[vec(2), vec(2), ckv, ckv, ckv],
        out_specs=pl.BlockSpec((n_ctx, LANES), lambda b, h: (b, h)),
        out_shape=jax.ShapeDtypeStruct((n_b * n_ctx, W_GROUP), BF16),
        scratch_shapes=scratch(n_ctx),
        compiler_params=_cparams(("arbitrary", "arbitrary")),
        name="diff_attn_context",
    )(lam_vec, sub, qn, kn, vn)
    return o_lat, o_ctx


def _rope_tables(n_seq):
    pos = jnp.arange(n_seq)
    row = (pos // GRID_W).astype(F32)
    col = (pos % GRID_W).astype(F32)
    inv_freq = ROPE_BASE ** (-jnp.arange(ROPE_AX, dtype=F32) / ROPE_AX)
    ang_r, ang_c = row[:, None] * inv_freq, col[:, None] * inv_freq
    z = jnp.zeros_like(ang_r)
    cos = jnp.concatenate([jnp.cos(ang_r)] * 2 + [jnp.cos(ang_c)] * 2, axis=1)
    s1 = jnp.concatenate([z, jnp.sin(ang_r), z, jnp.sin(ang_c)], axis=1)
    s2 = jnp.concatenate([-jnp.sin(ang_r), z, -jnp.sin(ang_c), z], axis=1)

    wide = lambda t: jnp.tile(t, (1, LANES // DIFF_D))
    return wide(cos), wide(s1), wide(s2)


def _rwkv_prep_body(n_b, n_l, n_c,
                    rp_ref, rc_ref, rn_ref, kp_ref, kc_ref, kn_ref, vp_ref, vc_ref, vn_ref, lp_ref, lc_ref, ln_ref,
                    mu_ref, w2_ref, a2_ref, g2_ref, w0_ref, a0_ref, kk_w_ref, ka_ref, rk_ref, e_ref,
                    r_ref, v_ref, kk_ref, lw0_ref, kd0_ref, b0_ref, lw1_ref, kd1_ref, b1_ref, bon_ref, g_ref,
                    buf_ref, bufl_ref):
    t = pl.program_id(0)
    p, n = _seg_pos(t, n_b, n_l, n_c)
    prev_ok = (p > 0).astype(F32)
    next_ok = (p < n - 1).astype(F32)

    def shifted(zp_ref, zc_ref, zn_ref, buf, mu):
        buf[0:HALO, :] = zp_ref[...] * prev_ok
        buf[HALO:HALO + SEQ_TILE, :] = zc_ref[...]
        buf[HALO + SEQ_TILE:, :] = zn_ref[...] * next_ok
        z = zc_ref[...]
        return z + mu * (0.5 * (buf[HALO - 1:HALO - 1 + SEQ_TILE, :] + buf[HALO + 1:HALO + 1 + SEQ_TILE, :]) - z)

    r = shifted(rp_ref, rc_ref, rn_ref, buf_ref, mu_ref[:, 0:W_GROUP])
    k = shifted(kp_ref, kc_ref, kn_ref, buf_ref, mu_ref[:, W_GROUP:2 * W_GROUP])
    v = shifted(vp_ref, vc_ref, vn_ref, buf_ref, mu_ref[:, 2 * W_GROUP:3 * W_GROUP])
    low = shifted(lp_ref, lc_ref, ln_ref, bufl_ref, mu_ref[:, 3 * W_GROUP:])
    kk = k * kk_w_ref[...]
    ss = _segsum(kk * kk, e_ref)
    kk = kk / jnp.maximum(jnp.sqrt(ss), 1e-12)
    tw = jnp.tanh(low).astype(BF16)
    lb = low.astype(BF16)
    r_ref[...] = r
    v_ref[...] = v
    kk_ref[...] = kk
    ksum = jnp.zeros_like(k)
    for d, (lw_ref, kd_ref, b_ref) in enumerate(((lw0_ref, kd0_ref, b0_ref), (lw1_ref, kd1_ref, b1_ref))):
        y = w0_ref[d:d + 1, :] + _dot(tw, w2_ref[d])
        y = -y
        softplus = jnp.maximum(y, 0.0) + jnp.log(1.0 + jnp.exp(-jnp.abs(y)))
        lw_ref[...] = -jnp.exp(-softplus - 0.5)
        ag = jax.nn.sigmoid(a0_ref[d:d + 1, :] + _dot(lb, a2_ref[d]))
        kd = k * (1.0 + (ag - 1.0) * ka_ref[...])
        kd_ref[...] = kd
        b_ref[...] = kk * ag
        ksum = ksum + kd
    bon_ref[...] = _segsum(r * ksum * rk_ref[...], e_ref) * v
    g_ref[...] = _dot(jax.nn.sigmoid(low).astype(BF16), g2_ref[...])


def _rwkv_prep(z, p, e_seg, n_b, n_l, n_c):
    rows = z.shape[0]
    vec = pl.BlockSpec((1, W_GROUP), lambda t: (0, 0))
    two = pl.BlockSpec((2, W_GROUP), lambda t: (0, 0))
    lowm = pl.BlockSpec((2, LANES, W_GROUP), lambda t: (0, 0, 0))
    out = pl.BlockSpec((SEQ_TILE, W_GROUP), lambda t: (t, 0))
    z_specs = []
    for col, width in ((COL_R, W_GROUP), (COL_K, W_GROUP), (COL_V, W_GROUP), (COL_LOW, LANES)):
        prv, nxt = _halo_maps(lambda t: t, rows, col)
        z_specs += [pl.BlockSpec((HALO, width), prv),
                    pl.BlockSpec((SEQ_TILE, width), lambda t, col=col: (t, col)),
                    pl.BlockSpec((HALO, width), nxt)]
    return pl.pallas_call(
        functools.partial(_rwkv_prep_body, n_b, n_l, n_c),
        grid=(rows // SEQ_TILE,),
        in_specs=z_specs + [pl.BlockSpec((1, RWKV_COLS), lambda t: (0, 0)), lowm, lowm,
                            pl.BlockSpec((LANES, W_GROUP), lambda t: (0, 0)), two, two, vec, vec, vec,
                            pl.BlockSpec((W_GROUP, W_GROUP), lambda t: (0, 0))],
        out_specs=[out] * 11,
        out_shape=[jax.ShapeDtypeStruct((rows, W_GROUP), F32)] * 11,
        scratch_shapes=[pltpu.VMEM((SEQ_TILE + 2 * HALO, W_GROUP), F32),
                        pltpu.VMEM((SEQ_TILE + 2 * HALO, LANES), F32)],
        compiler_params=_cparams(("arbitrary",)),
        name="rwkv_prep",
    )(*([z] * 12), p["mu"], p["w2"], p["a2"], p["g2"], p["w0"], p["a0"], p["kk"], p["ka"], p["rk"], e_seg)


def _rwkv_scan_body(*refs):
    c = RW_CHUNK
    in_refs, y_refs, st_ref = refs[:12], refs[12:14], refs[14]
    n_pair = RWKV_HEADS // 2

    @pl.when(pl.program_id(1) == 0)
    def _():
        st_ref[...] = jnp.zeros_like(st_ref)

    rt = lax.broadcasted_iota(jnp.int32, (c, c), 0)
    ct = lax.broadcasted_iota(jnp.int32, (c, c), 1)
    lane = lax.broadcasted_iota(jnp.int32, (c, LANES), 1)
    head0 = lane < RWKV_N
    rr = lax.broadcasted_iota(jnp.int32, (2 * c, 2 * c), 0)
    cc = lax.broadcasted_iota(jnp.int32, (2 * c, 2 * c), 1)
    same = (rr >= c) == (cc >= c)
    tt = rr & (c - 1)
    ss = cc & (c - 1)

    def stack(x):
        z = jnp.zeros_like(x)
        return jnp.concatenate([jnp.where(head0, x, z), jnp.where(head0, z, x)], axis=0).astype(BF16)

    chains = []
    for d in range(2):
        rev = d == 1
        r_ref, v_ref, kk_ref, lw_ref, kd_ref, b_ref = in_refs[6 * d:6 * d + 6]
        tri = ((rt <= ct) if rev else (rt >= ct)).astype(BF16)
        lw = lw_ref[...]
        hi = lw.astype(BF16)
        r1 = lw - hi.astype(F32)
        mid = r1.astype(BF16)
        lo = (r1 - mid.astype(F32)).astype(BF16)
        cum = _dot(tri, hi) + _dot(tri, mid) + _dot(tri, lo)
        last = 0 if rev else c - 1
        tot = cum[last:last + 1, :]
        g_end = jnp.exp(tot - cum)
        g_inv = jnp.exp(-cum)
        full = dict(a=-kk_ref[...] * jnp.exp(cum - lw), r=r_ref[...] * jnp.exp(cum), k=kd_ref[...] * g_inv,
                    b=b_ref[...] * g_inv, kg=kd_ref[...] * g_end, bg=b_ref[...] * g_end, v=v_ref[...])
        g_tot = jnp.exp(tot)
        strict = same & ((tt < ss) if rev else (tt > ss))
        incl = same & ((tt <= ss) if rev else (tt >= ss))
        for p in range(n_pair):
            sl = slice(p * LANES, (p + 1) * LANES)
            ch = {k: stack(x[:, sl]) for k, x in full.items()}
            ch.update(d=d, p=p, sl=sl, strict=strict, incl=incl, g_tot=g_tot[:, sl])
            chains.append(ch)

    c2 = 2 * c
    mask2 = lambda m, x: jnp.where(jnp.concatenate([m, m], axis=1), x, 0.0).astype(BF16)
    for ch in chains:
        kb = jnp.concatenate([ch["b"], ch["k"]], axis=0)
        sc = mask2(ch["strict"], _dot_nt(ch["a"], kb))
        ch["low"], ch["a_k"] = sc[:, :c2], sc[:, c2:]
        ch["r_bk"] = mask2(ch["incl"], _dot_nt(ch["r"], kb))
    for ch in chains:
        ch["st"] = st_ref[ch["d"], ch["p"]]
        ch["stb"] = ch["st"].astype(BF16)
        ch["u"] = _dot_nt(ch["a"], ch["stb"]) + _dot(ch["a_k"], ch["v"])
    n_sq = int(math.log2(c)) - 1
    for it in range(n_sq + 1):
        for ch in chains:
            if it < n_sq:
                both = _dot(ch["low"], jnp.concatenate([ch["low"], ch["u"].astype(BF16)], axis=1))
                ch["low"], ch["u"] = both[:, :c2].astype(BF16), ch["u"] + both[:, c2:]
            else:
                ch["u"] = ch["u"] + _dot(ch["low"], ch["u"].astype(BF16))
    for ch in chains:
        ub = ch["u"].astype(BF16)
        uv = jnp.concatenate([ub, ch["v"]], axis=0)
        y = _dot_nt(ch["r"], ch["stb"]) + _dot(ch["r_bk"], uv)
        y_refs[ch["d"]][:, ch["sl"]] = y[:c] + y[c:]
        st_ref[ch["d"], ch["p"]] = ch["st"] * ch["g_tot"] + _dot_tn(uv, jnp.concatenate([ch["bg"], ch["kg"]], axis=0))


def _rwkv_scan(pre, n_b, n_l, n_c):
    rows = pre[0].shape[0]
    blk = lambda rev: pl.BlockSpec((RW_CHUNK, W_GROUP),
                                   lambda b, i, rev=rev: (_seq_block(rev, b, i, n_c, n_l, n_b), 0))
    args, specs = [], []
    for d in range(2):
        args += [pre[0], pre[1], pre[2], pre[3 + 3 * d], pre[4 + 3 * d], pre[5 + 3 * d]]
        specs += [blk(d == 1)] * 6
    return pl.pallas_call(
        _rwkv_scan_body,
        grid=(n_b, n_c + n_l),
        in_specs=specs,
        out_specs=[blk(False), blk(True)],
        out_shape=[jax.ShapeDtypeStruct((rows, W_GROUP), F32)] * 2,
        scratch_shapes=[pltpu.VMEM((2, RWKV_HEADS // 2, LANES, LANES), F32)],
        compiler_params=_cparams(("arbitrary", "arbitrary")),
        name="rwkv_scan",
    )(*args)


def _rwkv_finish_body(yf_ref, yb_ref, bon_ref, g_ref, lg_ref, lb_ref, e_ref, o_ref):
    y = yf_ref[...] + yb_ref[...]
    dlt = y - _segsum(y, e_ref) * (1.0 / RWKV_N)
    var = _segsum(dlt * dlt, e_ref) * (1.0 / RWKV_N)
    yn = dlt * lax.rsqrt(var + GN_EPS) * lg_ref[...] + lb_ref[...]
    o_ref[...] = ((yn + bon_ref[...]) * g_ref[...]).astype(BF16)


def _rwkv_finish(yf, yb, bonus, gate, lnx_g, lnx_b, e_seg):
    rows = yf.shape[0]
    blk = pl.BlockSpec((ROW_TILE, W_GROUP), lambda i: (i, 0))
    vec = pl.BlockSpec((1, W_GROUP), lambda i: (0, 0))
    return pl.pallas_call(
        _rwkv_finish_body,
        grid=(rows // ROW_TILE,),
        in_specs=[blk, blk, blk, blk, vec, vec, pl.BlockSpec((W_GROUP, W_GROUP), lambda i: (0, 0))],
        out_specs=blk,
        out_shape=jax.ShapeDtypeStruct((rows, W_GROUP), BF16),
        compiler_params=_cparams(("arbitrary",)),
        name="rwkv_finish",
    )(yf, yb, bonus, gate, lnx_g.reshape(1, W_GROUP), lnx_b.reshape(1, W_GROUP), e_seg)


def _outproj_body(x_ref, ya_ref, yb_ref, yc_ref, yd_ref, w_ref, mod_ref, g2_ref, wrh_ref, wrl_ref, br_ref,
                  xo_ref, h2_ref, lg_ref):
    mix = _dot(ya_ref[...], w_ref[0:W_GROUP, :])
    mix = mix + _dot(yb_ref[...], w_ref[W_GROUP:2 * W_GROUP, :])
    mix = mix + _dot(yc_ref[...], w_ref[2 * W_GROUP:3 * W_GROUP, :])
    mix = mix + _dot(yd_ref[...], w_ref[3 * W_GROUP:, :])
    x = x_ref[...] + mod_ref[0, 2:3, :] * mix
    xo_ref[...] = x
    ms = jnp.mean(x * x, axis=-1, keepdims=True)
    h2 = x * lax.rsqrt(ms + NORM_EPS) * g2_ref[...]
    h2 = h2 * (1.0 + mod_ref[0, 4:5, :]) + mod_ref[0, 3:4, :]
    h2_ref[...] = h2.astype(BF16)
    lg_ref[...] = _dot3(h2, wrh_ref[...], wrl_ref[...]) + br_ref[...]


def _outproj(x, ys, w_out, mod3, g2, wr_hi, wr_lo, br, n_b, n_seq):
    rows = x.shape[0]
    big = pl.BlockSpec((OUT_TILE, D_MODEL), lambda i: (i, 0))
    yb = pl.BlockSpec((OUT_TILE, W_GROUP), lambda i: (i, 0))
    cst = lambda shape: pl.BlockSpec(shape, lambda i: (0, 0))
    return pl.pallas_call(
        _outproj_body,
        grid=(rows // OUT_TILE,),
        in_specs=[big, yb, yb, yb, yb, cst((D_MODEL, D_MODEL)),
                  pl.BlockSpec((1, 6, D_MODEL), lambda i: (_mod_row(i, OUT_TILE, n_b, n_seq), 0, 0)),
                  cst((1, D_MODEL)), cst((D_MODEL, LANES)), cst((D_MODEL, LANES)), cst((1, LANES))],
        out_specs=[big, big, pl.BlockSpec((OUT_TILE, LANES), lambda i: (i, 0))],
        out_shape=[jax.ShapeDtypeStruct((rows, D_MODEL), F32),
                   jax.ShapeDtypeStruct((rows, D_MODEL), BF16),
                   jax.ShapeDtypeStruct((rows, LANES), F32)],
        compiler_params=_cparams(("arbitrary",)),
        name="outproj_norm_router",
    )(x, ys[0], ys[1], ys[2], ys[3], w_out, mod3, g2.reshape(1, D_MODEL), wr_hi, wr_lo, br)


def _expert_body(be_ref, nv_ref, x_ref, w1_ref, w3_ref, w2_ref, o_ref):
    i = pl.program_id(0)

    @pl.when(i < nv_ref[0])
    def _():
        x = x_ref[...]
        h1 = _dot(x, w1_ref[0])
        hid = (h1 * jax.nn.sigmoid(h1)) * _dot(x, w3_ref[0])
        o_ref[...] = _dot(hid.astype(BF16), w2_ref[0])

    @pl.when(i >= nv_ref[0])
    def _():
        o_ref[...] = jnp.zeros_like(o_ref)


def _experts(xb, block_e, n_valid, w1, w3, w2):
    n_blocks = xb.shape[0] // MOE_BM
    return pl.pallas_call(
        _expert_body,
        grid_spec=pltpu.PrefetchScalarGridSpec(
            num_scalar_prefetch=2,
            grid=(n_blocks,),
            in_specs=[pl.BlockSpec((MOE_BM, D_MODEL), lambda i, be, nv: (i, 0)),
                      pl.BlockSpec((1, D_MODEL, D_EXPERT), lambda i, be, nv: (be[i], 0, 0)),
                      pl.BlockSpec((1, D_MODEL, D_EXPERT), lambda i, be, nv: (be[i], 0, 0)),
                      pl.BlockSpec((1, D_EXPERT, D_MODEL), lambda i, be, nv: (be[i], 0, 0))],
            out_specs=pl.BlockSpec((MOE_BM, D_MODEL), lambda i, be, nv: (i, 0))),
        out_shape=jax.ShapeDtypeStruct((xb.shape[0], D_MODEL), F32),
        compiler_params=_cparams(("arbitrary",)),
        name="moe_experts",
    )(block_e, n_valid, xb, w1, w3, w2)


def _combine_body(x_ref, y0_ref, y1_ref, rt_ref, mod_ref, o_ref):
    rt = rt_ref[...]
    mo = rt[:, RT_W:RT_W + 1] * y0_ref[...] + rt[:, RT_W + 1:RT_W + 2] * y1_ref[...]
    o_ref[...] = x_ref[...] + mod_ref[0, 5:6, :] * mo


def _combine(x, y0, y1, route, mod3, n_b, n_seq, n_out_rows):
    big = pl.BlockSpec((ROW_TILE, D_MODEL), lambda i: (i, 0))
    return pl.pallas_call(
        _combine_body,
        grid=(n_out_rows // ROW_TILE,),
        in_specs=[big, big, big, pl.BlockSpec((ROW_TILE, LANES), lambda i: (i, 0)),
                  pl.BlockSpec((1, 6, D_MODEL), lambda i: (_mod_row(i, ROW_TILE, n_b, n_seq), 0, 0))],
        out_specs=big,
        out_shape=jax.ShapeDtypeStruct((n_out_rows, D_MODEL), F32),
        compiler_params=_cparams(("arbitrary",)),
        name="moe_combine",
    )(x, y0, y1, route, mod3)


RT_E, RT_RANK, RT_W = 0, 2, 4
LOGIT_E0 = N_EGROUPS


def _route_body(lg_ref, rt_ref, cnt_ref, run_ref):
    tm = lg_ref.shape[0]

    @pl.when(pl.program_id(0) == 0)
    def _():
        run_ref[...] = jnp.zeros_like(run_ref)

    x = lg_ref[...]
    lane = lax.broadcasted_iota(jnp.int32, (tm, LANES), 1).astype(F32)
    neg = jnp.float32(-jnp.inf)

    def first_max(v):
        m = jnp.max(v, axis=-1, keepdims=True)
        return m, jnp.min(jnp.where(v == m, lane, float(LANES)), axis=-1, keepdims=True)

    g = jnp.where(lane < N_EGROUPS, x, neg)
    mg, gsel = first_max(g)
    gate_g = 1.0 / jnp.sum(jnp.exp(g - mg), axis=-1, keepdims=True)
    lo = LOGIT_E0 + EXP_PER_GROUP * gsel
    e = jnp.where((lane >= lo) & (lane < lo + EXP_PER_GROUP), x, neg)
    m1, i1 = first_max(e)
    m2, i2 = first_max(jnp.where(lane == i1, neg, e))
    e2 = jnp.exp(m2 - m1)
    w1 = gate_g / (1.0 + e2)
    w2 = gate_g * e2 / (1.0 + e2)
    oh1 = lane == i1
    oh2 = lane == i2
    oh = (oh1 | oh2).astype(BF16)
    rr = lax.broadcasted_iota(jnp.int32, (tm, tm), 0)
    cc = lax.broadcasted_iota(jnp.int32, (tm, tm), 1)
    before = _dot((rr > cc).astype(BF16), oh) + run_ref[0:1, :]
    r1 = jnp.sum(jnp.where(oh1, before, 0.0), axis=-1, keepdims=True)
    r2 = jnp.sum(jnp.where(oh2, before, 0.0), axis=-1, keepdims=True)
    run_ref[...] = run_ref[...] + jnp.sum(oh.astype(F32), axis=0, keepdims=True)
    cnt_ref[...] = run_ref[...]
    rec = jnp.where(lane == RT_E, i1 - LOGIT_E0, 0.0)
    rec = jnp.where(lane == RT_E + 1, i2 - LOGIT_E0, rec)
    rec = jnp.where(lane == RT_RANK, r1, rec)
    rec = jnp.where(lane == RT_RANK + 1, r2, rec)
    rec = jnp.where(lane == RT_W, w1, rec)
    rt_ref[...] = jnp.where(lane == RT_W + 1, w2, rec)


def _route(logits):
    rows = logits.shape[0]
    blk = pl.BlockSpec((ROW_TILE, LANES), lambda i: (i, 0))
    return pl.pallas_call(
        _route_body,
        grid=(rows // ROW_TILE,),
        in_specs=[blk],
        out_specs=[blk, pl.BlockSpec((SUBLANES, LANES), lambda i: (0, 0))],
        out_shape=[jax.ShapeDtypeStruct((rows, LANES), F32), jax.ShapeDtypeStruct((SUBLANES, LANES), F32)],
        scratch_shapes=[pltpu.VMEM((SUBLANES, LANES), F32)],
        compiler_params=_cparams(("arbitrary",)),
        name="moe_route",
    )(logits)


def _moe(x, h2, logits, mod3, w1, w3, w2, n_b, n_seq, n_out_rows):
    rows = x.shape[0]
    route, cnt = _route(logits)
    counts = cnt[0, LOGIT_E0:LOGIT_E0 + N_EXPERTS].astype(jnp.int32)
    pcounts = ((counts + MOE_BM - 1) // MOE_BM) * MOE_BM
    pend = jnp.cumsum(pcounts)
    pstart = pend - pcounts
    eid = route[:, RT_E:RT_E + 2].astype(jnp.int32)
    dest = pstart[eid] + route[:, RT_RANK:RT_RANK + 2].astype(jnp.int32)
    n_blocks = -(-(2 * rows) // MOE_BM) + N_EXPERTS
    tok = jnp.broadcast_to(jnp.arange(rows, dtype=jnp.int32)[:, None], (rows, 2))
    slot_tok = jnp.zeros((n_blocks * MOE_BM,), jnp.int32).at[dest.reshape(-1)].set(tok.reshape(-1))
    xb = jnp.take(h2, slot_tok, axis=0)
    blk_start = jnp.arange(n_blocks) * MOE_BM
    block_e = jnp.minimum(jnp.searchsorted(pend, blk_start, side='right'), N_EXPERTS - 1).astype(jnp.int32)
    n_valid = (pend[-1] // MOE_BM).astype(jnp.int32).reshape(1)
    yb = _experts(xb, block_e, n_valid, w1, w3, w2)
    y0 = jnp.take(yb, dest[:n_out_rows, 0], axis=0)
    y1 = jnp.take(yb, dest[:n_out_rows, 1], axis=0)
    return _combine(x, y0, y1, route, mod3, n_b, n_seq, n_out_rows)


def _pad_rows(w, lo, n):
    pad = [(0, 0)] * (w.ndim - 2) + [(lo, n - lo - w.shape[-2]), (0, 0)]
    return jnp.pad(w.astype(F32), pad).astype(BF16)


def kernel(x, c, ctx, c_ctx, w_ada, b_ada, g_norm1, g_norm2, w_in, w_out, s5_lam_re, s5_lam_im, s5_log_step, s5_b_re, s5_b_im, s5_c_re, s5_c_im, s5_d, s5_w_glu, s5_b_glu, lru_conv_w, lru_conv_b, lru_lam, lru_wa, lru_ba, lru_wx, lru_bx, diff_gq, diff_gk, diff_lq1, diff_lk1, diff_lq2, diff_lk2, diff_subln, rw_mu, rw_w0, rw_w2, rw_a0, rw_a2, rw_g2, rw_kk, rw_ka, rw_rk, rw_lnx_g, rw_lnx_b, moe_w_rg, moe_b_rg, moe_w_re, moe_b_re, moe_w1, moe_w3, moe_w2):
    n_b, n_seq, _ = x.shape
    n_ctx = ctx.shape[1]
    depth = w_ada.shape[0]
    assert n_seq % SEQ_TILE == 0 and n_ctx % SEQ_TILE == 0 and n_seq % n_ctx == 0
    assert (n_b * n_ctx) % ROW_TILE == 0 and n_seq % ROW_TILE == 0 and n_seq % ATT_TK == 0
    xs = jnp.concatenate([x.reshape(n_b * n_seq, D_MODEL), ctx.reshape(n_b * n_ctx, D_MODEL)], axis=0)
    rows = xs.shape[0]
    cvec = jnp.zeros((SUBLANES, D_MODEL), F32).at[0].set(c_ctx).at[1:1 + n_b].set(c)
    rope = _rope_tables(n_seq)
    seg = jnp.arange(W_GROUP) // RWKV_N
    e_seg = (seg[:, None] == seg[None, :]).astype(BF16)
    nl_s, nc_s = n_seq // SEQ_TILE, n_ctx // SEQ_TILE
    nl_r, nc_r = n_seq // RW_CHUNK, n_ctx // RW_CHUNK

    for l in range(depth):
        lam_init = 0.8 - 0.6 * math.exp(-0.3 * l)
        mod = _modulation(cvec, w_ada[l], b_ada[l])
        mod3 = mod[:1 + n_b].reshape(1 + n_b, 6, D_MODEL)
        w_in_b = jnp.pad(w_in[l].astype(BF16), ((0, 0), (0, D_IN_PAD - D_IN)))
        za = _inproj(xs, mod3, g_norm1[l], w_in_b, n_b, n_seq)
        s5p = _s5_prepare(s5_lam_re[l], s5_lam_im[l], s5_log_step[l], s5_b_re[l], s5_b_im[l],
                          s5_c_re[l], s5_c_im[l])
        y5 = [_s5_scan(za, s5p, d, n_b, nl_s, nc_s) for d in range(2)]
        ya = _s5_finish(y5[0], y5[1], za, s5_d[l], s5_w_glu[l], s5_b_glu[l])
        lrp = dict(cw=lru_conv_w[l], cb=lru_conv_b[l].reshape(1, W_GROUP),
                   wa=_blockdiag(lru_wa[l]).astype(BF16), wx=_blockdiag(lru_wx[l]).astype(BF16),
                   ba=lru_ba[l].reshape(2, 1, W_GROUP), bx=lru_bx[l].reshape(2, 1, W_GROUP),
                   sp=jax.nn.softplus(-lru_lam[l]).reshape(2, 1, W_GROUP))
        hl = [_lru_scan(za, lrp, d, n_b, nl_s, nc_s) for d in range(2)]
        yb_ = _lru_finish(hl[0], hl[1], za)
        lam = jnp.exp(jnp.sum(diff_lq1[l] * diff_lk1[l])) - jnp.exp(jnp.sum(diff_lq2[l] * diff_lk2[l])) + lam_init
        lam_vec = jnp.full((1, LANES), lam, F32)
        qn, kn, vn = _qkv_prep(za, rope, diff_gq[l], diff_gk[l], e_seg, n_b, n_seq)
        o_lat, o_ctx = _attention(qn, kn, vn, lam_vec, diff_subln[l], lam_init, n_b, n_seq, n_ctx)
        yc = jnp.concatenate([o_lat, o_ctx], axis=0)
        lo_w, lo_a = RANK_G, RANK_G + RANK_W
        rwp = dict(mu=rw_mu[l].reshape(1, RWKV_COLS),
                   w2=_pad_rows(rw_w2[l], lo_w, LANES), a2=_pad_rows(rw_a2[l], lo_a, LANES),
                   g2=_pad_rows(rw_g2[l], 0, LANES), w0=rw_w0[l], a0=rw_a0[l],
                   kk=rw_kk[l].reshape(1, W_GROUP), ka=rw_ka[l].reshape(1, W_GROUP),
                   rk=rw_rk[l].reshape(1, W_GROUP))
        pre = _rwkv_prep(za, rwp, e_seg, n_b, nl_s, nc_s)
        yr = _rwkv_scan(pre, n_b, nl_r, nc_r)
        yd = _rwkv_finish(yr[0], yr[1], pre[9], pre[10], rw_lnx_g[l], rw_lnx_b[l], e_seg)
        wr = jnp.zeros((D_MODEL, LANES), F32).at[:, :N_EGROUPS].set(moe_w_rg[l])
        wr = wr.at[:, N_EGROUPS:N_EGROUPS + N_EXPERTS].set(moe_w_re[l])
        wr_hi, wr_lo = _split2(wr)
        br = jnp.zeros((1, LANES), F32).at[0, :N_EGROUPS].set(moe_b_rg[l])
        br = br.at[0, N_EGROUPS:N_EGROUPS + N_EXPERTS].set(moe_b_re[l])
        xs, h2, logits = _outproj(xs, (ya, yb_, yc, yd), w_out[l].astype(BF16), mod3, g_norm2[l],
                                  wr_hi, wr_lo, br, n_b, n_seq)
        n_out_rows = rows if l < depth - 1 else n_b * n_seq
        xs = _moe(xs, h2, logits, mod3, moe_w1[l].astype(BF16), moe_w3[l].astype(BF16), moe_w2[l].astype(BF16),
                  n_b, n_seq, n_out_rows)
    return xs.reshape(n_b, n_seq, D_MODEL)
```

```python
import functools
import math

import jax
import jax.numpy as jnp
from jax import lax
from jax.experimental import pallas as pl
from jax.experimental.pallas import tpu as pltpu

F32 = jnp.float32
BF16 = jnp.bfloat16

D_MODEL = 2048
W_GROUP = 512
S5_CH, S5_GROUPS, S5_STATE = 16, 32, 64
S5_NS = S5_GROUPS * S5_STATE
LRU_C = 8.0
DIFF_HEADS, DIFF_D = 4, 64
ROPE_AX = 16
ROPE_BASE = 10000.0
RWKV_N, RWKV_HEADS = 64, 8
RANK_G, RANK_W, RANK_A = 64, 32, 32
RWKV_COLS = 3 * W_GROUP + RANK_G + RANK_W + RANK_A
N_EGROUPS, EXP_PER_GROUP, N_EXPERTS = 4, 8, 32
D_EXPERT = 1024
NORM_EPS = 1e-6
GN_EPS = 64e-5
GRID_W = 64

SUBLANES = 8
LANES = 128
ROW_TILE = 512
SEQ_TILE = 256
RW_CHUNK = 64
ATT_TQ = 256
ATT_UNROLL = 8
ATT_TK = 512
ATT_RB = 128
MOE_BM = 256
OUT_TILE = 256
IN_TN = 1024
D_IN = 6 * W_GROUP + RWKV_COLS
D_IN_PAD = -(-D_IN // IN_TN) * IN_TN
COL_R, COL_K, COL_V = 6, 7, 8
COL_LOW = (9 * W_GROUP) // LANES
VMEM_LIMIT = 56 * 1024 * 1024


def _cparams(sem):
    return pltpu.CompilerParams(dimension_semantics=sem, vmem_limit_bytes=VMEM_LIMIT)


def _split2(x):
    hi = x.astype(BF16)
    lo = (x - hi.astype(F32)).astype(BF16)
    return hi, lo


def _dot(a, b):
    return jnp.dot(a, b, preferred_element_type=F32)


def _dot_nt(a, b):
    return lax.dot_general(a, b, (((1,), (1,)), ((), ())), preferred_element_type=F32)


def _dot_tn(a, b):
    return lax.dot_general(a, b, (((0,), (0,)), ((), ())), preferred_element_type=F32)


def _dot3(a, b_hi, b_lo):
    a_hi, a_lo = _split2(a)
    return _dot(a_hi, b_hi) + _dot(a_hi, b_lo) + _dot(a_lo, b_hi)


def _segsum(x, e_ref):
    hi, lo = _split2(x)
    e = e_ref[...]
    return _dot(hi, e) + _dot(lo, e)


def _seq_block(rev, b, i, n_c, n_l, n_b):
    if rev:
        ctx = n_b * n_l + b * n_c + (n_c - 1 - i)
        lat = b * n_l + (n_l - 1 - (i - n_c))
    else:
        ctx = n_b * n_l + b * n_c + i
        lat = b * n_l + (i - n_c)
    return jnp.where(i < n_c, ctx, lat)


def _seg_pos(t, n_b, n_l, n_c):
    is_lat = t < n_b * n_l
    p = jnp.where(is_lat, t % n_l, (t - n_b * n_l) % n_c)
    n = jnp.where(is_lat, n_l, n_c)
    return p, n


def _mod_body(c_ref, w_ref, b_ref, o_ref):
    c = c_ref[...]
    s = c * jax.nn.sigmoid(c)
    w_hi, w_lo = _split2(w_ref[0])
    o_ref[...] = _dot3(s, w_hi, w_lo) + b_ref[...]


def _modulation(cvec, layer, w_ada, b_ada):
    d6 = w_ada.shape[2]
    tn = 1536
    return pl.pallas_call(
        _mod_body,
        grid=(d6 // tn,),
        in_specs=[pl.BlockSpec((SUBLANES, D_MODEL), lambda j: (0, 0)),
                  pl.BlockSpec((1, D_MODEL, tn), lambda j: (layer, 0, j)),
                  pl.BlockSpec((1, tn), lambda j: (0, j))],
        out_specs=pl.BlockSpec((SUBLANES, tn), lambda j: (0, j)),
        out_shape=jax.ShapeDtypeStruct((SUBLANES, d6), F32),
        compiler_params=_cparams(("arbitrary",)),
        name="adaln_mod",
    )(cvec, w_ada, b_ada.reshape(1, d6))


def _mod_row(i, tile, n_b, n_seq):
    return jnp.where(i < n_b * n_seq // tile, 1 + i // (n_seq // tile), 0)


def _inproj_body(x_ref, mod_ref, g_ref, w_ref, z_ref, h_ref):
    @pl.when(pl.program_id(1) == 0)
    def _():
        x = x_ref[...]
        ms = jnp.mean(x * x, axis=-1, keepdims=True)
        xn = x * lax.rsqrt(ms + NORM_EPS) * g_ref[...]
        h_ref[...] = (xn * (1.0 + mod_ref[0, 1:2, :]) + mod_ref[0, 0:1, :]).astype(BF16)

    z_ref[...] = _dot(h_ref[...], w_ref[...])


def _inproj(x, mod3, g1, w, n_b, n_seq):
    rows = x.shape[0]
    tn = IN_TN
    return pl.pallas_call(
        _inproj_body,
        grid=(rows // ROW_TILE, w.shape[1] // tn),
        in_specs=[pl.BlockSpec((ROW_TILE, D_MODEL), lambda i, j: (i, 0)),
                  pl.BlockSpec((1, 6, D_MODEL), lambda i, j: (_mod_row(i, ROW_TILE, n_b, n_seq), 0, 0)),
                  pl.BlockSpec((1, D_MODEL), lambda i, j: (0, 0)),
                  pl.BlockSpec((D_MODEL, tn), lambda i, j: (0, j))],
        out_specs=pl.BlockSpec((ROW_TILE, tn), lambda i, j: (i, j)),
        out_shape=jax.ShapeDtypeStruct((rows, w.shape[1]), F32),
        scratch_shapes=[pltpu.VMEM((ROW_TILE, D_MODEL), BF16)],
        compiler_params=_cparams(("arbitrary", "arbitrary")),
        name="norm_inproj",
    )(x, mod3, g1.reshape(1, D_MODEL), w)


def _s5_body(rev, u_ref, bre_ref, bim_ref, t1r_ref, t1i_ref, p2r_ref, p2i_ref, cre_ref, cim_ref,
             y_ref, sr_ref, si_ref, car_ref):
    g_n = SEQ_TILE // SUBLANES

    @pl.when(pl.program_id(1) == 0)
    def _():
        car_ref[...] = jnp.zeros_like(car_ref)

    u = u_ref[...].astype(BF16)
    xr = _dot(u, bre_ref[0]).reshape(g_n, SUBLANES, S5_NS)
    xi = _dot(u, bim_ref[0]).reshape(g_n, SUBLANES, S5_NS)
    for lvl, s in enumerate((1, 2, 4)):
        sh = (SUBLANES - s) if rev else s
        pr = pltpu.roll(xr, sh, axis=1)
        pi = pltpu.roll(xi, sh, axis=1)
        ar = t1r_ref[0, lvl][None]
        ai = t1i_ref[0, lvl][None]
        xr, xi = xr + ar * pr - ai * pi, xi + ar * pi + ai * pr
    sr_ref[...] = xr
    si_ref[...] = xi
    p2r = p2r_ref[0]
    p2i = p2i_ref[0]
    row = 0 if rev else SUBLANES - 1

    def step(g, _):
        gg = (g_n - 1 - g) if rev else g
        cr = car_ref[0]
        ci = car_ref[1]
        nr = sr_ref[gg] + p2r * cr - p2i * ci
        ni = si_ref[gg] + p2r * ci + p2i * cr
        sr_ref[gg] = nr
        si_ref[gg] = ni
        car_ref[0] = jnp.broadcast_to(nr[row:row + 1, :], (SUBLANES, S5_NS))
        car_ref[1] = jnp.broadcast_to(ni[row:row + 1, :], (SUBLANES, S5_NS))
        return 0

    lax.fori_loop(0, g_n, step, 0)
    s_r = sr_ref[...].reshape(SEQ_TILE, S5_NS).astype(BF16)
    s_i = si_ref[...].reshape(SEQ_TILE, S5_NS).astype(BF16)
    y_ref[...] = _dot(s_r, cre_ref[0]) - _dot(s_i, cim_ref[0])


def _s5_scan(za, p, d, n_b, n_l, n_c):
    rev = d == 1
    rows = za.shape[0]
    seq = lambda b, i: (_seq_block(rev, b, i, n_c, n_l, n_b), 0)
    cst3 = lambda b, i: (d, 0, 0)
    cst4 = lambda b, i: (d, 0, 0, 0)
    return pl.pallas_call(
        functools.partial(_s5_body, rev),
        grid=(n_b, n_c + n_l),
        in_specs=[pl.BlockSpec((SEQ_TILE, W_GROUP), seq),
                  pl.BlockSpec((1, W_GROUP, S5_NS), cst3),
                  pl.BlockSpec((1, W_GROUP, S5_NS), cst3),
                  pl.BlockSpec((1, 3, SUBLANES, S5_NS), cst4),
                  pl.BlockSpec((1, 3, SUBLANES, S5_NS), cst4),
                  pl.BlockSpec((1, SUBLANES, S5_NS), cst3),
                  pl.BlockSpec((1, SUBLANES, S5_NS), cst3),
                  pl.BlockSpec((1, S5_NS, W_GROUP), cst3),
                  pl.BlockSpec((1, S5_NS, W_GROUP), cst3)],
        out_specs=pl.BlockSpec((SEQ_TILE, W_GROUP), seq),
        out_shape=jax.ShapeDtypeStruct((rows, W_GROUP), F32),
        scratch_shapes=[pltpu.VMEM((SEQ_TILE // SUBLANES, SUBLANES, S5_NS), F32),
                        pltpu.VMEM((SEQ_TILE // SUBLANES, SUBLANES, S5_NS), F32),
                        pltpu.VMEM((2, SUBLANES, S5_NS), F32)],
        compiler_params=_cparams(("arbitrary", "arbitrary")),
        name="s5_scan_rev" if rev else "s5_scan_fwd",
    )(za, p["bre"], p["bim"], p["t1r"], p["t1i"], p["p2r"], p["p2i"], p["cre"], p["cim"])


def _s5_prepare(lam_re, lam_im, log_step, b_re, b_im, c_re, c_im):
    lr, li = lam_re.astype(F32), lam_im.astype(F32)
    dt = jnp.exp(log_step.astype(F32))[..., None]
    mag = jnp.exp(lr * dt)
    ar, ai = mag * jnp.cos(li * dt), mag * jnp.sin(li * dt)
    den = lr * lr + li * li
    cr = ((ar - 1.0) * lr + ai * li) / den
    ci = (ai * lr - (ar - 1.0) * li) / den
    bbr = cr[..., None] * b_re - ci[..., None] * b_im
    bbi = cr[..., None] * b_im + ci[..., None] * b_re
    eye = jnp.eye(S5_GROUPS, dtype=F32)
    bd = lambda t: jnp.einsum('dgph,gk->dghkp', t, eye).reshape(2, W_GROUP, S5_NS).astype(BF16)
    cd = lambda t: jnp.einsum('dghp,gk->dgpkh', t.astype(F32), eye).reshape(2, S5_NS, W_GROUP).astype(BF16)
    ar = ar.reshape(2, S5_NS)
    ai = ai.reshape(2, S5_NS)
    pw_r, pw_i = [ar], [ai]
    for _ in range(SUBLANES - 1):
        pw_r.append(pw_r[-1] * ar - pw_i[-1] * ai)
        pw_i.append(pw_r[-2] * ai + pw_i[-1] * ar)
    k = jnp.arange(SUBLANES)[None, :, None]
    t1r, t1i, p2r, p2i = [], [], [], []
    for d in range(2):
        lr_, li_ = [], []
        for s in (1, 2, 4):
            m = (k + s <= SUBLANES - 1) if d == 1 else (k >= s)
            lr_.append(jnp.where(m, pw_r[s - 1][d][None, None, :], 0.0)[0])
            li_.append(jnp.where(m, pw_i[s - 1][d][None, None, :], 0.0)[0])
        t1r.append(jnp.stack(lr_))
        t1i.append(jnp.stack(li_))
        order = range(SUBLANES - 1, -1, -1) if d == 1 else range(SUBLANES)
        p2r.append(jnp.stack([pw_r[j][d] for j in order]))
        p2i.append(jnp.stack([pw_i[j][d] for j in order]))
    return dict(bre=bd(bbr), bim=bd(bbi), cre=cd(c_re), cim=cd(c_im),
                t1r=jnp.stack(t1r), t1i=jnp.stack(t1i), p2r=jnp.stack(p2r), p2i=jnp.stack(p2i))


def _s5_finish_body(yf_ref, yb_ref, u_ref, d_ref, w_ref, b_ref, o_ref):
    y = jax.nn.gelu(yf_ref[...] + yb_ref[...] + d_ref[...] * u_ref[...])
    gate = jax.nn.sigmoid(_dot(y.astype(BF16), w_ref[...]) + b_ref[...])
    o_ref[...] = (y * gate).astype(BF16)


def _s5_finish(yf, yb, za, d_skip, w_glu, b_glu):
    rows = yf.shape[0]
    blk = pl.BlockSpec((ROW_TILE, W_GROUP), lambda i: (i, 0))
    vec = pl.BlockSpec((1, W_GROUP), lambda i: (0, 0))
    return pl.pallas_call(
        _s5_finish_body,
        grid=(rows // ROW_TILE,),
        in_specs=[blk, blk, blk, vec, pl.BlockSpec((W_GROUP, W_GROUP), lambda i: (0, 0)), vec],
        out_specs=blk,
        out_shape=jax.ShapeDtypeStruct((rows, W_GROUP), BF16),
        compiler_params=_cparams(("arbitrary",)),
        name="s5_finish",
    )(yf, yb, za, d_skip.reshape(1, W_GROUP), w_glu.astype(BF16), b_glu.reshape(1, W_GROUP))


HALO = SUBLANES


def _halo_maps(tile_fn, rows, col):
    per = SEQ_TILE // HALO
    last = rows // HALO - 1
    prv = lambda *ids: (jnp.maximum(tile_fn(*ids) * per - 1, 0), col)
    nxt = lambda *ids: (jnp.minimum((tile_fn(*ids) + 1) * per, last), col)
    return prv, nxt


def _lru_body(rev, n_b, n_l, n_c, xp_ref, xc_ref, xn_ref, cw_ref, cb_ref, wa_ref, ba_ref, wx_ref, bx_ref,
              sp_ref, h_ref, buf_ref, a_ref, b_ref, car_ref):
    g_n = SEQ_TILE // SUBLANES
    b = pl.program_id(0)
    i = pl.program_id(1)

    @pl.when(i == 0)
    def _():
        car_ref[...] = jnp.zeros_like(car_ref)

    p, n = _seg_pos(_seq_block(rev, b, i, n_c, n_l, n_b), n_b, n_l, n_c)
    prev_ok = (p > 0).astype(F32)
    next_ok = (p < n - 1).astype(F32)
    buf_ref[0:HALO, :] = xp_ref[...] * prev_ok
    buf_ref[HALO:HALO + SEQ_TILE, :] = xc_ref[...]
    buf_ref[HALO + SEQ_TILE:, :] = xn_ref[...] * next_ok
    v = cb_ref[...] + cw_ref[2:3, :] * xc_ref[...]
    v = v + cw_ref[0:1, :] * buf_ref[HALO - 2:HALO - 2 + SEQ_TILE, :]
    v = v + cw_ref[1:2, :] * buf_ref[HALO - 1:HALO - 1 + SEQ_TILE, :]
    v = v + cw_ref[3:4, :] * buf_ref[HALO + 1:HALO + 1 + SEQ_TILE, :]
    vb = v.astype(BF16)
    r = jax.nn.sigmoid(_dot(vb, wa_ref[0]) + ba_ref[0])
    ig = jax.nn.sigmoid(_dot(vb, wx_ref[0]) + bx_ref[0])
    a = jnp.exp(-LRU_C * r * sp_ref[0])
    bb = jnp.sqrt(1.0 - a * a) * (ig * v)
    a3 = a.reshape(g_n, SUBLANES, W_GROUP)
    b3 = bb.reshape(g_n, SUBLANES, W_GROUP)
    k = lax.broadcasted_iota(jnp.int32, (g_n, SUBLANES, W_GROUP), 1)
    for s in (1, 2, 4):
        sh = (SUBLANES - s) if rev else s
        m = (k + s <= SUBLANES - 1) if rev else (k >= s)
        a_s = pltpu.roll(a3, sh, axis=1)
        b_s = pltpu.roll(b3, sh, axis=1)
        b3 = jnp.where(m, a3 * b_s + b3, b3)
        a3 = jnp.where(m, a3 * a_s, a3)
    a_ref[...] = a3
    b_ref[...] = b3
    row = 0 if rev else SUBLANES - 1

    def step(g, _):
        gg = (g_n - 1 - g) if rev else g
        hh = b_ref[gg] + a_ref[gg] * car_ref[...]
        b_ref[gg] = hh
        car_ref[...] = jnp.broadcast_to(hh[row:row + 1, :], (SUBLANES, W_GROUP))
        return 0

    lax.fori_loop(0, g_n, step, 0)
    h_ref[...] = b_ref[...].reshape(SEQ_TILE, W_GROUP)


def _lru_scan(za, p, d, n_b, n_l, n_c):
    rev = d == 1
    rows = za.shape[0]
    tile = lambda b, i: _seq_block(rev, b, i, n_c, n_l, n_b)
    cur = lambda b, i: (tile(b, i), 1)
    prv, nxt = _halo_maps(tile, rows, 1)
    out = lambda b, i: (tile(b, i), 0)
    vec = pl.BlockSpec((1, W_GROUP), lambda b, i: (0, 0))
    dvec = pl.BlockSpec((1, 1, W_GROUP), lambda b, i: (d, 0, 0))
    dmat = pl.BlockSpec((1, W_GROUP, W_GROUP), lambda b, i: (d, 0, 0))
    blk = lambda f: pl.BlockSpec((SEQ_TILE, W_GROUP), f)
    halo = lambda f: pl.BlockSpec((HALO, W_GROUP), f)
    g_n = SEQ_TILE // SUBLANES
    return pl.pallas_call(
        functools.partial(_lru_body, rev, n_b, n_l, n_c),
        grid=(n_b, n_c + n_l),
        in_specs=[halo(prv), blk(cur), halo(nxt),
                  pl.BlockSpec((4, W_GROUP), lambda b, i: (0, 0)), vec,
                  dmat, dvec, dmat, dvec, dvec],
        out_specs=blk(out),
        out_shape=jax.ShapeDtypeStruct((rows, W_GROUP), F32),
        scratch_shapes=[pltpu.VMEM((SEQ_TILE + 2 * HALO, W_GROUP), F32),
                        pltpu.VMEM((g_n, SUBLANES, W_GROUP), F32),
                        pltpu.VMEM((g_n, SUBLANES, W_GROUP), F32),
                        pltpu.VMEM((SUBLANES, W_GROUP), F32)],
        compiler_params=_cparams(("arbitrary", "arbitrary")),
        name="lru_scan_rev" if rev else "lru_scan_fwd",
    )(za, za, za, p["cw"], p["cb"], p["wa"], p["ba"], p["wx"], p["bx"], p["sp"])


def _blockdiag(w):
    nb, c = w.shape[1], w.shape[2]
    eye = jnp.eye(nb, dtype=F32)
    return jnp.einsum('dncf,nm->dncmf', w.astype(F32), eye).reshape(2, nb * c, nb * c)


def _lru_finish_body(hf_ref, hb_ref, g_ref, o_ref):
    o_ref[...] = ((hf_ref[...] + hb_ref[...]) * jax.nn.gelu(g_ref[...])).astype(BF16)


def _lru_finish(hf, hb, za):
    rows = hf.shape[0]
    blk = pl.BlockSpec((ROW_TILE, W_GROUP), lambda i: (i, 0))
    return pl.pallas_call(
        _lru_finish_body,
        grid=(rows // ROW_TILE,),
        in_specs=[blk, blk, pl.BlockSpec((ROW_TILE, W_GROUP), lambda i: (i, 2))],
        out_specs=blk,
        out_shape=jax.ShapeDtypeStruct((rows, W_GROUP), BF16),
        compiler_params=_cparams(("arbitrary",)),
        name="lru_finish",
    )(hf, hb, za)


def _qkv_prep_body(n_lat_tiles, q_ref, k_ref, v_ref, cos_ref, s1_ref, s2_ref, gq_ref, gk_ref, e_ref,
                   qo_ref, ko_ref, vo_ref):
    reps = W_GROUP // LANES
    is_ctx = pl.program_id(0) >= n_lat_tiles
    cos = jnp.concatenate([jnp.where(is_ctx, 1.0, cos_ref[...])] * reps, axis=1)
    s1 = jnp.concatenate([jnp.where(is_ctx, 0.0, s1_ref[...])] * reps, axis=1)
    s2 = jnp.concatenate([jnp.where(is_ctx, 0.0, s2_ref[...])] * reps, axis=1)

    def prep(x, g):
        ms = _segsum(x * x, e_ref) * (1.0 / DIFF_D)
        x = x * lax.rsqrt(ms + NORM_EPS) * g
        return x * cos + pltpu.roll(x, ROPE_AX, axis=1) * s1 + pltpu.roll(x, W_GROUP - ROPE_AX, axis=1) * s2

    qo_ref[...] = (prep(q_ref[...], gq_ref[...]) * (DIFF_D ** -0.5)).astype(BF16)
    ko_ref[...] = prep(k_ref[...], gk_ref[...]).astype(BF16)
    vo_ref[...] = v_ref[...].astype(BF16)


def _qkv_prep(za, rope, gq, gk, e_seg, n_b, n_seq):
    rows = za.shape[0]
    per_batch = n_seq // ROW_TILE
    col = lambda c: pl.BlockSpec((ROW_TILE, W_GROUP), lambda i: (i, c))
    tab = pl.BlockSpec((ROW_TILE, LANES), lambda i: (i % per_batch, 0))
    vec = pl.BlockSpec((1, W_GROUP), lambda i: (0, 0))
    out = pl.BlockSpec((ROW_TILE, W_GROUP), lambda i: (i, 0))
    reps = W_GROUP // DIFF_D
    return pl.pallas_call(
        functools.partial(_qkv_prep_body, n_b * per_batch),
        grid=(rows // ROW_TILE,),
        in_specs=[col(3), col(4), col(5), tab, tab, tab, vec, vec,
                  pl.BlockSpec((W_GROUP, W_GROUP), lambda i: (0, 0))],
        out_specs=[out, out, out],
        out_shape=[jax.ShapeDtypeStruct((rows, W_GROUP), BF16)] * 3,
        compiler_params=_cparams(("arbitrary",)),
        name="attn_qkv_prep",
    )(za, za, za, rope[0], rope[1], rope[2], jnp.tile(gq, reps).reshape(1, W_GROUP),
      jnp.tile(gk, reps).reshape(1, W_GROUP), e_seg)


def _attn_body(seg_lens, out_scale, lam_ref, sub_ref, q_ref, *rest):
    o_ref, q2_ref, m_ref, acc_ref, sa_ref, sb_ref = rest[-6:]
    kv = rest[:-6]
    tq = q_ref.shape[0]
    q = q_ref[...]
    lane = lax.broadcasted_iota(jnp.int32, (tq, LANES), 1)
    zero = jnp.zeros_like(q)
    q2 = jnp.concatenate([jnp.where(lane < DIFF_D, q, zero), jnp.where(lane < DIFF_D, zero, q)], axis=0)
    q2_ref[...] = q2
    m_ref[...] = jnp.full_like(m_ref, -jnp.inf)
    acc_ref[...] = jnp.zeros_like(acc_ref)
    n_rb = (2 * tq) // ATT_RB

    rows = [slice(rb * ATT_RB, (rb + 1) * ATT_RB) for rb in range(n_rb)]

    def scores(kc, s_ref):
        for rs in rows:
            s_ref[rs, 0:kc.shape[0]] = _dot_nt(q2_ref[rs, :], kc)

    def softmax_pv(s_ref, vc):
        tk = vc.shape[0]
        v_ext = jnp.concatenate([vc, jnp.ones_like(vc)], axis=1)
        for rs in rows:
            s = s_ref[rs, 0:tk]
            m_old = m_ref[rs, :]
            m_new = jnp.maximum(m_old, jnp.max(s, axis=-1, keepdims=True))
            alpha = jnp.exp(m_old - m_new)
            pm = jnp.exp((s - jnp.concatenate([m_new] * (tk // LANES), axis=1)).astype(BF16))
            acc_ref[rs, :] = jnp.concatenate([alpha, alpha], axis=1) * acc_ref[rs, :] + _dot(pm, v_ext)
            m_ref[rs, :] = m_new

    for si, n_k in enumerate(seg_lens):
        k_ref, v_ref = kv[2 * si], kv[2 * si + 1]
        tk = min(ATT_TK, n_k)
        n_ch = n_k // tk
        if n_ch < 2:
            for j in range(n_ch):
                scores(k_ref[j * tk:(j + 1) * tk, :], sa_ref)
                softmax_pv(sa_ref, v_ref[j * tk:(j + 1) * tk, :])
            continue
        unroll = min(ATT_UNROLL, n_ch)
        assert unroll % 2 == 0 and n_ch % unroll == 0
        chunk = lambda ref, j, tk=tk: ref[pl.ds(pl.multiple_of(j * tk, tk), tk), :]
        scores(chunk(k_ref, 0), sa_ref)

        def body(jj, _, k_ref=k_ref, v_ref=v_ref, n_ch=n_ch, unroll=unroll):
            for u in range(unroll):
                j = unroll * jj + u
                cur, nxt = (sa_ref, sb_ref) if u % 2 == 0 else (sb_ref, sa_ref)
                scores(chunk(k_ref, jnp.minimum(j + 1, n_ch - 1)), nxt)
                softmax_pv(cur, chunk(v_ref, j))
            return 0

        lax.fori_loop(0, n_ch // unroll, body, 0)

    o = acc_ref[:, 0:LANES] / acc_ref[:, LANES:]
    o = o[:tq] - lam_ref[...] * o[tq:]
    ms = jnp.mean(o * o, axis=-1, keepdims=True)
    o_ref[...] = (o * lax.rsqrt(ms + NORM_EPS) * sub_ref[...] * out_scale).astype(BF16)


def _attention(qn, kn, vn, lam_vec, subln, lam_init, n_b, n_seq, n_ctx):
    vec = lambda nd: pl.BlockSpec((1, LANES), (lambda b, h, i: (0, 0)) if nd == 3 else (lambda b, h: (0, 0)))
    scratch = lambda tq: [pltpu.VMEM((2 * tq, LANES), BF16), pltpu.VMEM((2 * tq, LANES), F32),
                          pltpu.VMEM((2 * tq, 2 * LANES), F32),
                          pltpu.VMEM((2 * tq, ATT_TK), F32), pltpu.VMEM((2 * tq, ATT_TK), F32)]
    sub = subln.reshape(1, LANES)
    ctx_blk0 = n_b * n_seq // n_ctx
    qpb = n_seq // ATT_TQ
    lat_kv = pl.BlockSpec((n_seq, LANES), lambda b, h, i: (b, h))
    ctx_kv = pl.BlockSpec((n_ctx, LANES), lambda b, h, i: (ctx_blk0 + b, h))
    o_lat = pl.pallas_call(
        functools.partial(_attn_body, (n_seq, n_ctx), 1.0 - lam_init),
        grid=(n_b, DIFF_HEADS, qpb),
        in_specs=[vec(3), vec(3), pl.BlockSpec((ATT_TQ, LANES), lambda b, h, i: (b * qpb + i, h)),
                  lat_kv, lat_kv, ctx_kv, ctx_kv],
        out_specs=pl.BlockSpec((ATT_TQ, LANES), lambda b, h, i: (b * qpb + i, h)),
        out_shape=jax.ShapeDtypeStruct((n_b * n_seq, W_GROUP), BF16),
        scratch_shapes=scratch(ATT_TQ),
        compiler_params=_cparams(("arbitrary", "arbitrary", "arbitrary")),
        name="diff_attn_latent",
    )(lam_vec, sub, qn, kn, vn, kn, vn)
    ckv = pl.BlockSpec((n_ctx, LANES), lambda b, h: (ctx_blk0 + b, h))
    o_ctx = pl.pallas_call(
        functools.partial(_attn_body, (n_ctx,), 1.0 - lam_init),
        grid=(n_b, DIFF_HEADS),
        in_specs=[vec(2), vec(2), ckv, ckv, ckv],
        out_specs=pl.BlockSpec((n_ctx, LANES), lambda b, h: (b, h)),
        out_shape=jax.ShapeDtypeStruct((n_b * n_ctx, W_GROUP), BF16),
        scratch_shapes=scratch(n_ctx),
        compiler_params=_cparams(("arbitrary", "arbitrary")),
        name="diff_attn_context",
    )(lam_vec, sub, qn, kn, vn)
    return o_lat, o_ctx


def _rope_tables(n_seq):
    pos = jnp.arange(n_seq)
    row = (pos // GRID_W).astype(F32)
    col = (pos % GRID_W).astype(F32)
    inv_freq = ROPE_BASE ** (-jnp.arange(ROPE_AX, dtype=F32) / ROPE_AX)
    ang_r, ang_c = row[:, None] * inv_freq, col[:, None] * inv_freq
    z = jnp.zeros_like(ang_r)
    cos = jnp.concatenate([jnp.cos(ang_r)] * 2 + [jnp.cos(ang_c)] * 2, axis=1)
    s1 = jnp.concatenate([z, jnp.sin(ang_r), z, jnp.sin(ang_c)], axis=1)
    s2 = jnp.concatenate([-jnp.sin(ang_r), z, -jnp.sin(ang_c), z], axis=1)

    wide = lambda t: jnp.tile(t, (1, LANES // DIFF_D))
    return wide(cos), wide(s1), wide(s2)


def _rwkv_prep_body(n_b, n_l, n_c,
                    rp_ref, rc_ref, rn_ref, kp_ref, kc_ref, kn_ref, vp_ref, vc_ref, vn_ref, lp_ref, lc_ref, ln_ref,
                    mu_ref, w2_ref, a2_ref, g2_ref, w0_ref, a0_ref, kk_w_ref, ka_ref, rk_ref, e_ref,
                    r_ref, v_ref, kk_ref, lw0_ref, kd0_ref, b0_ref, lw1_ref, kd1_ref, b1_ref, bon_ref, g_ref,
                    buf_ref, bufl_ref):
    t = pl.program_id(0)
    p, n = _seg_pos(t, n_b, n_l, n_c)
    prev_ok = (p > 0).astype(F32)
    next_ok = (p < n - 1).astype(F32)

    def shifted(zp_ref, zc_ref, zn_ref, buf, mu):
        buf[0:HALO, :] = zp_ref[...] * prev_ok
        buf[HALO:HALO + SEQ_TILE, :] = zc_ref[...]
        buf[HALO + SEQ_TILE:, :] = zn_ref[...] * next_ok
        z = zc_ref[...]
        return z + mu * (0.5 * (buf[HALO - 1:HALO - 1 + SEQ_TILE, :] + buf[HALO + 1:HALO + 1 + SEQ_TILE, :]) - z)

    r = shifted(rp_ref, rc_ref, rn_ref, buf_ref, mu_ref[:, 0:W_GROUP])
    k = shifted(kp_ref, kc_ref, kn_ref, buf_ref, mu_ref[:, W_GROUP:2 * W_GROUP])
    v = shifted(vp_ref, vc_ref, vn_ref, buf_ref, mu_ref[:, 2 * W_GROUP:3 * W_GROUP])
    low = shifted(lp_ref, lc_ref, ln_ref, bufl_ref, mu_ref[:, 3 * W_GROUP:])
    kk = k * kk_w_ref[...]
    ss = _segsum(kk * kk, e_ref)
    kk = kk / jnp.maximum(jnp.sqrt(ss), 1e-12)
    tw = jnp.tanh(low).astype(BF16)
    lb = low.astype(BF16)
    r_ref[...] = r
    v_ref[...] = v
    kk_ref[...] = kk
    ksum = jnp.zeros_like(k)
    for d, (lw_ref, kd_ref, b_ref) in enumerate(((lw0_ref, kd0_ref, b0_ref), (lw1_ref, kd1_ref, b1_ref))):
        y = w0_ref[d:d + 1, :] + _dot(tw, w2_ref[d])
        y = -y
        softplus = jnp.maximum(y, 0.0) + jnp.log(1.0 + jnp.exp(-jnp.abs(y)))
        lw_ref[...] = -jnp.exp(-softplus - 0.5)
        ag = jax.nn.sigmoid(a0_ref[d:d + 1, :] + _dot(lb, a2_ref[d]))
        kd = k * (1.0 + (ag - 1.0) * ka_ref[...])
        kd_ref[...] = kd
        b_ref[...] = kk * ag
        ksum = ksum + kd
    bon_ref[...] = _segsum(r * ksum * rk_ref[...], e_ref) * v
    g_ref[...] = _dot(jax.nn.sigmoid(low).astype(BF16), g2_ref[...])


def _rwkv_prep(z, p, e_seg, n_b, n_l, n_c):
    rows = z.shape[0]
    vec = pl.BlockSpec((1, W_GROUP), lambda t: (0, 0))
    two = pl.BlockSpec((2, W_GROUP), lambda t: (0, 0))
    lowm = pl.BlockSpec((2, LANES, W_GROUP), lambda t: (0, 0, 0))
    out = pl.BlockSpec((SEQ_TILE, W_GROUP), lambda t: (t, 0))
    z_specs = []
    for col, width in ((COL_R, W_GROUP), (COL_K, W_GROUP), (COL_V, W_GROUP), (COL_LOW, LANES)):
        prv, nxt = _halo_maps(lambda t: t, rows, col)
        z_specs += [pl.BlockSpec((HALO, width), prv),
                    pl.BlockSpec((SEQ_TILE, width), lambda t, col=col: (t, col)),
                    pl.BlockSpec((HALO, width), nxt)]
    return pl.pallas_call(
        functools.partial(_rwkv_prep_body, n_b, n_l, n_c),
        grid=(rows // SEQ_TILE,),
        in_specs=z_specs + [pl.BlockSpec((1, RWKV_COLS), lambda t: (0, 0)), lowm, lowm,
                            pl.BlockSpec((LANES, W_GROUP), lambda t: (0, 0)), two, two, vec, vec, vec,
                            pl.BlockSpec((W_GROUP, W_GROUP), lambda t: (0, 0))],
        out_specs=[out] * 11,
        out_shape=[jax.ShapeDtypeStruct((rows, W_GROUP), F32)] * 11,
        scratch_shapes=[pltpu.VMEM((SEQ_TILE + 2 * HALO, W_GROUP), F32),
                        pltpu.VMEM((SEQ_TILE + 2 * HALO, LANES), F32)],
        compiler_params=_cparams(("arbitrary",)),
        name="rwkv_prep",
    )(*([z] * 12), p["mu"], p["w2"], p["a2"], p["g2"], p["w0"], p["a0"], p["kk"], p["ka"], p["rk"], e_seg)


def _rwkv_scan_body(*refs):
    c = RW_CHUNK
    in_refs, y_refs, st_ref = refs[:12], refs[12:14], refs[14]
    n_pair = RWKV_HEADS // 2

    @pl.when(pl.program_id(1) == 0)
    def _():
        st_ref[...] = jnp.zeros_like(st_ref)

    rt = lax.broadcasted_iota(jnp.int32, (c, c), 0)
    ct = lax.broadcasted_iota(jnp.int32, (c, c), 1)
    lane = lax.broadcasted_iota(jnp.int32, (c, LANES), 1)
    head0 = lane < RWKV_N
    rr = lax.broadcasted_iota(jnp.int32, (2 * c, 2 * c), 0)
    cc = lax.broadcasted_iota(jnp.int32, (2 * c, 2 * c), 1)
    same = (rr >= c) == (cc >= c)
    tt = rr & (c - 1)
    ss = cc & (c - 1)

    def stack(x):
        z = jnp.zeros_like(x)
        return jnp.concatenate([jnp.where(head0, x, z), jnp.where(head0, z, x)], axis=0).astype(BF16)

    chains = []
    for d in range(2):
        rev = d == 1
        r_ref, v_ref, kk_ref, lw_ref, kd_ref, b_ref = in_refs[6 * d:6 * d + 6]
        tri = ((rt <= ct) if rev else (rt >= ct)).astype(BF16)
        lw = lw_ref[...]
        hi = lw.astype(BF16)
        r1 = lw - hi.astype(F32)
        mid = r1.astype(BF16)
        lo = (r1 - mid.astype(F32)).astype(BF16)
        cum = _dot(tri, hi) + _dot(tri, mid) + _dot(tri, lo)
        last = 0 if rev else c - 1
        tot = cum[last:last + 1, :]
        g_end = jnp.exp(tot - cum)
        g_inv = jnp.exp(-cum)
        full = dict(a=-kk_ref[...] * jnp.exp(cum - lw), r=r_ref[...] * jnp.exp(cum), k=kd_ref[...] * g_inv,
                    b=b_ref[...] * g_inv, kg=kd_ref[...] * g_end, bg=b_ref[...] * g_end, v=v_ref[...])
        g_tot = jnp.exp(tot)
        strict = same & ((tt < ss) if rev else (tt > ss))
        incl = same & ((tt <= ss) if rev else (tt >= ss))
        for p in range(n_pair):
            sl = slice(p * LANES, (p + 1) * LANES)
            ch = {k: stack(x[:, sl]) for k, x in full.items()}
            ch.update(d=d, p=p, sl=sl, strict=strict, incl=incl, g_tot=g_tot[:, sl])
            chains.append(ch)

    c2 = 2 * c
    mask2 = lambda m, x: jnp.where(jnp.concatenate([m, m], axis=1), x, 0.0).astype(BF16)
    for ch in chains:
        kb = jnp.concatenate([ch["b"], ch["k"]], axis=0)
        sc = mask2(ch["strict"], _dot_nt(ch["a"], kb))
        ch["low"], ch["a_k"] = sc[:, :c2], sc[:, c2:]
        ch["r_bk"] = mask2(ch["incl"], _dot_nt(ch["r"], kb))
    for ch in chains:
        ch["st"] = st_ref[ch["d"], ch["p"]]
        ch["stb"] = ch["st"].astype(BF16)
        ch["u"] = _dot_nt(ch["a"], ch["stb"]) + _dot(ch["a_k"], ch["v"])
    n_sq = int(math.log2(c)) - 1
    for it in range(n_sq + 1):
        for ch in chains:
            if it < n_sq:
                both = _dot(ch["low"], jnp.concatenate([ch["low"], ch["u"].astype(BF16)], axis=1))
                ch["low"], ch["u"] = both[:, :c2].astype(BF16), ch["u"] + both[:, c2:]
            else:
                ch["u"] = ch["u"] + _dot(ch["low"], ch["u"].astype(BF16))
    for ch in chains:
        ub = ch["u"].astype(BF16)
        uv = jnp.concatenate([ub, ch["v"]], axis=0)
        y = _dot_nt(ch["r"], ch["stb"]) + _dot(ch["r_bk"], uv)
        y_refs[ch["d"]][:, ch["sl"]] = y[:c] + y[c:]
        st_ref[ch["d"], ch["p"]] = ch["st"] * ch["g_tot"] + _dot_tn(uv, jnp.concatenate([ch["bg"], ch["kg"]], axis=0))


def _rwkv_scan(pre, n_b, n_l, n_c):
    rows = pre[0].shape[0]
    blk = lambda rev: pl.BlockSpec((RW_CHUNK, W_GROUP),
                                   lambda b, i, rev=rev: (_seq_block(rev, b, i, n_c, n_l, n_b), 0))
    args, specs = [], []
    for d in range(2):
        args += [pre[0], pre[1], pre[2], pre[3 + 3 * d], pre[4 + 3 * d], pre[5 + 3 * d]]
        specs += [blk(d == 1)] * 6
    return pl.pallas_call(
        _rwkv_scan_body,
        grid=(n_b, n_c + n_l),
        in_specs=specs,
        out_specs=[blk(False), blk(True)],
        out_shape=[jax.ShapeDtypeStruct((rows, W_GROUP), F32)] * 2,
        scratch_shapes=[pltpu.VMEM((2, RWKV_HEADS // 2, LANES, LANES), F32)],
        compiler_params=_cparams(("arbitrary", "arbitrary")),
        name="rwkv_scan",
    )(*args)


def _rwkv_finish_body(yf_ref, yb_ref, bon_ref, g_ref, lg_ref, lb_ref, e_ref, o_ref):
    y = yf_ref[...] + yb_ref[...]
    dlt = y - _segsum(y, e_ref) * (1.0 / RWKV_N)
    var = _segsum(dlt * dlt, e_ref) * (1.0 / RWKV_N)
    yn = dlt * lax.rsqrt(var + GN_EPS) * lg_ref[...] + lb_ref[...]
    o_ref[...] = ((yn + bon_ref[...]) * g_ref[...]).astype(BF16)


def _rwkv_finish(yf, yb, bonus, gate, lnx_g, lnx_b, e_seg):
    rows = yf.shape[0]
    blk = pl.BlockSpec((ROW_TILE, W_GROUP), lambda i: (i, 0))
    vec = pl.BlockSpec((1, W_GROUP), lambda i: (0, 0))
    return pl.pallas_call(
        _rwkv_finish_body,
        grid=(rows // ROW_TILE,),
        in_specs=[blk, blk, blk, blk, vec, vec, pl.BlockSpec((W_GROUP, W_GROUP), lambda i: (0, 0))],
        out_specs=blk,
        out_shape=jax.ShapeDtypeStruct((rows, W_GROUP), BF16),
        compiler_params=_cparams(("arbitrary",)),
        name="rwkv_finish",
    )(yf, yb, bonus, gate, lnx_g.reshape(1, W_GROUP), lnx_b.reshape(1, W_GROUP), e_seg)


def _outproj_body(x_ref, ya_ref, yb_ref, yc_ref, yd_ref, w_ref, mod_ref, g2_ref, wrh_ref, wrl_ref, br_ref,
                  xo_ref, h2_ref, lg_ref):
    mix = _dot(ya_ref[...], w_ref[0:W_GROUP, :])
    mix = mix + _dot(yb_ref[...], w_ref[W_GROUP:2 * W_GROUP, :])
    mix = mix + _dot(yc_ref[...], w_ref[2 * W_GROUP:3 * W_GROUP, :])
    mix = mix + _dot(yd_ref[...], w_ref[3 * W_GROUP:, :])
    x = x_ref[...] + mod_ref[0, 2:3, :] * mix
    xo_ref[...] = x
    ms = jnp.mean(x * x, axis=-1, keepdims=True)
    h2 = x * lax.rsqrt(ms + NORM_EPS) * g2_ref[...]
    h2 = h2 * (1.0 + mod_ref[0, 4:5, :]) + mod_ref[0, 3:4, :]
    h2_ref[...] = h2.astype(BF16)
    lg_ref[...] = _dot3(h2, wrh_ref[...], wrl_ref[...]) + br_ref[...]


def _outproj(x, ys, w_out, mod3, g2, wr_hi, wr_lo, br, n_b, n_seq):
    rows = x.shape[0]
    big = pl.BlockSpec((OUT_TILE, D_MODEL), lambda i: (i, 0))
    yb = pl.BlockSpec((OUT_TILE, W_GROUP), lambda i: (i, 0))
    cst = lambda shape: pl.BlockSpec(shape, lambda i: (0, 0))
    return pl.pallas_call(
        _outproj_body,
        grid=(rows // OUT_TILE,),
        in_specs=[big, yb, yb, yb, yb, cst((D_MODEL, D_MODEL)),
                  pl.BlockSpec((1, 6, D_MODEL), lambda i: (_mod_row(i, OUT_TILE, n_b, n_seq), 0, 0)),
                  cst((1, D_MODEL)), cst((D_MODEL, LANES)), cst((D_MODEL, LANES)), cst((1, LANES))],
        out_specs=[big, big, pl.BlockSpec((OUT_TILE, LANES), lambda i: (i, 0))],
        out_shape=[jax.ShapeDtypeStruct((rows, D_MODEL), F32),
                   jax.ShapeDtypeStruct((rows, D_MODEL), BF16),
                   jax.ShapeDtypeStruct((rows, LANES), F32)],
        compiler_params=_cparams(("arbitrary",)),
        name="outproj_norm_router",
    )(x, ys[0], ys[1], ys[2], ys[3], w_out, mod3, g2.reshape(1, D_MODEL), wr_hi, wr_lo, br)


def _expert_body(be_ref, nv_ref, new_ref, x_ref, w1_ref, w3_ref, w2_ref, o_ref, w1b_ref, w3b_ref, w2b_ref):
    i = pl.program_id(0)

    @pl.when(new_ref[i] == 1)
    def _():
        w1b_ref[...] = w1_ref[0, 0].astype(BF16)
        w3b_ref[...] = w3_ref[0, 0].astype(BF16)
        w2b_ref[...] = w2_ref[0, 0].astype(BF16)

    @pl.when(i < nv_ref[0])
    def _():
        x = x_ref[...]
        h1 = _dot(x, w1b_ref[...])
        hid = (h1 * jax.nn.sigmoid(h1)) * _dot(x, w3b_ref[...])
        o_ref[...] = _dot(hid.astype(BF16), w2b_ref[...])

    @pl.when(i >= nv_ref[0])
    def _():
        o_ref[...] = jnp.zeros_like(o_ref)


def _experts(xb, block_e, n_valid, layer, w1, w3, w2):
    n_blocks = xb.shape[0] // MOE_BM
    is_new = jnp.concatenate([jnp.ones((1,), jnp.int32), (block_e[1:] != block_e[:-1]).astype(jnp.int32)])
    wspec = lambda shape: pl.BlockSpec((1, 1) + shape, lambda i, be, nv, nw: (layer, be[i], 0, 0),
                                       pipeline_mode=pl.Buffered(1))
    return pl.pallas_call(
        _expert_body,
        grid_spec=pltpu.PrefetchScalarGridSpec(
            num_scalar_prefetch=3,
            grid=(n_blocks,),
            in_specs=[pl.BlockSpec((MOE_BM, D_MODEL), lambda i, be, nv, nw: (i, 0)),
                      wspec((D_MODEL, D_EXPERT)), wspec((D_MODEL, D_EXPERT)), wspec((D_EXPERT, D_MODEL))],
            out_specs=pl.BlockSpec((MOE_BM, D_MODEL), lambda i, be, nv, nw: (i, 0)),
            scratch_shapes=[pltpu.VMEM((D_MODEL, D_EXPERT), BF16), pltpu.VMEM((D_MODEL, D_EXPERT), BF16),
                            pltpu.VMEM((D_EXPERT, D_MODEL), BF16)]),
        out_shape=jax.ShapeDtypeStruct((xb.shape[0], D_MODEL), F32),
        compiler_params=_cparams(("arbitrary",)),
        name="moe_experts",
    )(block_e, n_valid, is_new, xb, w1, w3, w2)


def _combine_body(x_ref, y0_ref, y1_ref, rt_ref, mod_ref, o_ref):
    rt = rt_ref[...]
    mo = rt[:, RT_W:RT_W + 1] * y0_ref[...] + rt[:, RT_W + 1:RT_W + 2] * y1_ref[...]
    o_ref[...] = x_ref[...] + mod_ref[0, 5:6, :] * mo


def _combine(x, y0, y1, route, mod3, n_b, n_seq, n_out_rows):
    big = pl.BlockSpec((ROW_TILE, D_MODEL), lambda i: (i, 0))
    return pl.pallas_call(
        _combine_body,
        grid=(n_out_rows // ROW_TILE,),
        in_specs=[big, big, big, pl.BlockSpec((ROW_TILE, LANES), lambda i: (i, 0)),
                  pl.BlockSpec((1, 6, D_MODEL), lambda i: (_mod_row(i, ROW_TILE, n_b, n_seq), 0, 0))],
        out_specs=big,
        out_shape=jax.ShapeDtypeStruct((n_out_rows, D_MODEL), F32),
        compiler_params=_cparams(("arbitrary",)),
        name="moe_combine",
    )(x, y0, y1, route, mod3)


RT_E, RT_RANK, RT_W = 0, 2, 4
LOGIT_E0 = N_EGROUPS


def _route_body(lg_ref, rt_ref, cnt_ref, run_ref):
    tm = lg_ref.shape[0]

    @pl.when(pl.program_id(0) == 0)
    def _():
        run_ref[...] = jnp.zeros_like(run_ref)

    x = lg_ref[...]
    lane = lax.broadcasted_iota(jnp.int32, (tm, LANES), 1).astype(F32)
    neg = jnp.float32(-jnp.inf)

    def first_max(v):
        m = jnp.max(v, axis=-1, keepdims=True)
        return m, jnp.min(jnp.where(v == m, lane, float(LANES)), axis=-1, keepdims=True)

    g = jnp.where(lane < N_EGROUPS, x, neg)
    mg, gsel = first_max(g)
    gate_g = 1.0 / jnp.sum(jnp.exp(g - mg), axis=-1, keepdims=True)
    lo = LOGIT_E0 + EXP_PER_GROUP * gsel
    e = jnp.where((lane >= lo) & (lane < lo + EXP_PER_GROUP), x, neg)
    m1, i1 = first_max(e)
    m2, i2 = first_max(jnp.where(lane == i1, neg, e))
    e2 = jnp.exp(m2 - m1)
    w1 = gate_g / (1.0 + e2)
    w2 = gate_g * e2 / (1.0 + e2)
    oh1 = lane == i1
    oh2 = lane == i2
    oh = (oh1 | oh2).astype(BF16)
    rr = lax.broadcasted_iota(jnp.int32, (tm, tm), 0)
    cc = lax.broadcasted_iota(jnp.int32, (tm, tm), 1)
    before = _dot((rr > cc).astype(BF16), oh) + run_ref[0:1, :]
    r1 = jnp.sum(jnp.where(oh1, before, 0.0), axis=-1, keepdims=True)
    r2 = jnp.sum(jnp.where(oh2, before, 0.0), axis=-1, keepdims=True)
    run_ref[...] = run_ref[...] + jnp.sum(oh.astype(F32), axis=0, keepdims=True)
    cnt_ref[...] = run_ref[...]
    rec = jnp.where(lane == RT_E, i1 - LOGIT_E0, 0.0)
    rec = jnp.where(lane == RT_E + 1, i2 - LOGIT_E0, rec)
    rec = jnp.where(lane == RT_RANK, r1, rec)
    rec = jnp.where(lane == RT_RANK + 1, r2, rec)
    rec = jnp.where(lane == RT_W, w1, rec)
    rt_ref[...] = jnp.where(lane == RT_W + 1, w2, rec)


def _route(logits):
    rows = logits.shape[0]
    blk = pl.BlockSpec((ROW_TILE, LANES), lambda i: (i, 0))
    return pl.pallas_call(
        _route_body,
        grid=(rows // ROW_TILE,),
        in_specs=[blk],
        out_specs=[blk, pl.BlockSpec((SUBLANES, LANES), lambda i: (0, 0))],
        out_shape=[jax.ShapeDtypeStruct((rows, LANES), F32), jax.ShapeDtypeStruct((SUBLANES, LANES), F32)],
        scratch_shapes=[pltpu.VMEM((SUBLANES, LANES), F32)],
        compiler_params=_cparams(("arbitrary",)),
        name="moe_route",
    )(logits)


def _moe(x, h2, logits, mod3, layer, w1, w3, w2, n_b, n_seq, n_out_rows):
    rows = x.shape[0]
    route, cnt = _route(logits)
    counts = cnt[0, LOGIT_E0:LOGIT_E0 + N_EXPERTS].astype(jnp.int32)
    pcounts = ((counts + MOE_BM - 1) // MOE_BM) * MOE_BM
    pend = jnp.cumsum(pcounts)
    pstart = pend - pcounts
    eid = route[:, RT_E:RT_E + 2].astype(jnp.int32)
    dest = pstart[eid] + route[:, RT_RANK:RT_RANK + 2].astype(jnp.int32)
    n_blocks = -(-(2 * rows) // MOE_BM) + N_EXPERTS
    tok = jnp.broadcast_to(jnp.arange(rows, dtype=jnp.int32)[:, None], (rows, 2))
    slot_tok = jnp.zeros((n_blocks * MOE_BM,), jnp.int32).at[dest.reshape(-1)].set(tok.reshape(-1))
    take_rows = lambda a, idx: a.at[idx].get(mode="promise_in_bounds")
    xb = take_rows(h2, slot_tok)
    blk_start = jnp.arange(n_blocks) * MOE_BM
    block_e = jnp.sum(blk_start[:, None] >= pend[None, :], axis=1)
    block_e = jnp.minimum(block_e, N_EXPERTS - 1).astype(jnp.int32)
    n_valid = (pend[-1] // MOE_BM).astype(jnp.int32).reshape(1)
    yb = _experts(xb, block_e, n_valid, layer, w1, w3, w2)
    y0 = take_rows(yb, dest[:n_out_rows, 0])
    y1 = take_rows(yb, dest[:n_out_rows, 1])
    return _combine(x, y0, y1, route, mod3, n_b, n_seq, n_out_rows)


def _pad_rows(w, lo, n):
    pad = [(0, 0)] * (w.ndim - 2) + [(lo, n - lo - w.shape[-2]), (0, 0)]
    return jnp.pad(w.astype(F32), pad).astype(BF16)


def kernel(x, c, ctx, c_ctx, w_ada, b_ada, g_norm1, g_norm2, w_in, w_out, s5_lam_re, s5_lam_im, s5_log_step, s5_b_re, s5_b_im, s5_c_re, s5_c_im, s5_d, s5_w_glu, s5_b_glu, lru_conv_w, lru_conv_b, lru_lam, lru_wa, lru_ba, lru_wx, lru_bx, diff_gq, diff_gk, diff_lq1, diff_lk1, diff_lq2, diff_lk2, diff_subln, rw_mu, rw_w0, rw_w2, rw_a0, rw_a2, rw_g2, rw_kk, rw_ka, rw_rk, rw_lnx_g, rw_lnx_b, moe_w_rg, moe_b_rg, moe_w_re, moe_b_re, moe_w1, moe_w3, moe_w2):
    n_b, n_seq, _ = x.shape
    n_ctx = ctx.shape[1]
    depth = w_ada.shape[0]
    assert n_seq % SEQ_TILE == 0 and n_ctx % SEQ_TILE == 0 and n_seq % n_ctx == 0
    assert (n_b * n_ctx) % ROW_TILE == 0 and n_seq % ROW_TILE == 0 and n_seq % ATT_TK == 0
    xs = jnp.concatenate([x.reshape(n_b * n_seq, D_MODEL), ctx.reshape(n_b * n_ctx, D_MODEL)], axis=0)
    rows = xs.shape[0]
    cvec = jnp.zeros((SUBLANES, D_MODEL), F32).at[0].set(c_ctx).at[1:1 + n_b].set(c)
    rope = _rope_tables(n_seq)
    seg = jnp.arange(W_GROUP) // RWKV_N
    e_seg = (seg[:, None] == seg[None, :]).astype(BF16)
    nl_s, nc_s = n_seq // SEQ_TILE, n_ctx // SEQ_TILE
    nl_r, nc_r = n_seq // RW_CHUNK, n_ctx // RW_CHUNK

    for l in range(depth):
        lam_init = 0.8 - 0.6 * math.exp(-0.3 * l)
        mod = _modulation(cvec, l, w_ada, b_ada[l])
        mod3 = mod[:1 + n_b].reshape(1 + n_b, 6, D_MODEL)
        w_in_b = jnp.pad(w_in[l].astype(BF16), ((0, 0), (0, D_IN_PAD - D_IN)))
        za = _inproj(xs, mod3, g_norm1[l], w_in_b, n_b, n_seq)
        s5p = _s5_prepare(s5_lam_re[l], s5_lam_im[l], s5_log_step[l], s5_b_re[l], s5_b_im[l],
                          s5_c_re[l], s5_c_im[l])
        y5 = [_s5_scan(za, s5p, d, n_b, nl_s, nc_s) for d in range(2)]
        ya = _s5_finish(y5[0], y5[1], za, s5_d[l], s5_w_glu[l], s5_b_glu[l])
        lrp = dict(cw=lru_conv_w[l], cb=lru_conv_b[l].reshape(1, W_GROUP),
                   wa=_blockdiag(lru_wa[l]).astype(BF16), wx=_blockdiag(lru_wx[l]).astype(BF16),
                   ba=lru_ba[l].reshape(2, 1, W_GROUP), bx=lru_bx[l].reshape(2, 1, W_GROUP),
                   sp=jax.nn.softplus(-lru_lam[l]).reshape(2, 1, W_GROUP))
        hl = [_lru_scan(za, lrp, d, n_b, nl_s, nc_s) for d in range(2)]
        yb_ = _lru_finish(hl[0], hl[1], za)
        lam = jnp.exp(jnp.sum(diff_lq1[l] * diff_lk1[l])) - jnp.exp(jnp.sum(diff_lq2[l] * diff_lk2[l])) + lam_init
        lam_vec = jnp.full((1, LANES), lam, F32)
        qn, kn, vn = _qkv_prep(za, rope, diff_gq[l], diff_gk[l], e_seg, n_b, n_seq)
        o_lat, o_ctx = _attention(qn, kn, vn, lam_vec, diff_subln[l], lam_init, n_b, n_seq, n_ctx)
        yc = jnp.concatenate([o_lat, o_ctx], axis=0)
        lo_w, lo_a = RANK_G, RANK_G + RANK_W
        rwp = dict(mu=rw_mu[l].reshape(1, RWKV_COLS),
                   w2=_pad_rows(rw_w2[l], lo_w, LANES), a2=_pad_rows(rw_a2[l], lo_a, LANES),
                   g2=_pad_rows(rw_g2[l], 0, LANES), w0=rw_w0[l], a0=rw_a0[l],
                   kk=rw_kk[l].reshape(1, W_GROUP), ka=rw_ka[l].reshape(1, W_GROUP),
                   rk=rw_rk[l].reshape(1, W_GROUP))
        pre = _rwkv_prep(za, rwp, e_seg, n_b, nl_s, nc_s)
        yr = _rwkv_scan(pre, n_b, nl_r, nc_r)
        yd = _rwkv_finish(yr[0], yr[1], pre[9], pre[10], rw_lnx_g[l], rw_lnx_b[l], e_seg)
        wr = jnp.zeros((D_MODEL, LANES), F32).at[:, :N_EGROUPS].set(moe_w_rg[l])
        wr = wr.at[:, N_EGROUPS:N_EGROUPS + N_EXPERTS].set(moe_w_re[l])
        wr_hi, wr_lo = _split2(wr)
        br = jnp.zeros((1, LANES), F32).at[0, :N_EGROUPS].set(moe_b_rg[l])
        br = br.at[0, N_EGROUPS:N_EGROUPS + N_EXPERTS].set(moe_b_re[l])
        xs, h2, logits = _outproj(xs, (ya, yb_, yc, yd), w_out[l].astype(BF16), mod3, g_norm2[l],
                                  wr_hi, wr_lo, br, n_b, n_seq)
        n_out_rows = rows if l < depth - 1 else n_b * n_seq
        xs = _moe(xs, h2, logits, mod3, l, moe_w1, moe_w3, moe_w2, n_b, n_seq, n_out_rows)
    return xs.reshape(n_b, n_seq, D_MODEL)
```

```python
import functools
import math

import jax
import jax.numpy as jnp
from jax import lax
from jax.experimental import pallas as pl
from jax.experimental.pallas import tpu as pltpu

F32 = jnp.float32
BF16 = jnp.bfloat16

D_MODEL = 2048
W_GROUP = 512
S5_CH, S5_GROUPS, S5_STATE = 16, 32, 64
S5_NS = S5_GROUPS * S5_STATE
LRU_C = 8.0
DIFF_HEADS, DIFF_D = 4, 64
ROPE_AX = 16
ROPE_BASE = 10000.0
RWKV_N, RWKV_HEADS = 64, 8
RANK_G, RANK_W, RANK_A = 64, 32, 32
RWKV_COLS = 3 * W_GROUP + RANK_G + RANK_W + RANK_A
N_EGROUPS, EXP_PER_GROUP, N_EXPERTS = 4, 8, 32
D_EXPERT = 1024
NORM_EPS = 1e-6
GN_EPS = 64e-5
GRID_W = 64

SUBLANES = 8
LANES = 128
ROW_TILE = 512
SEQ_TILE = 256
RW_CHUNK = 64
ATT_TQ = 256
ATT_UNROLL = 8
ATT_TK = 512
ATT_RB = 128
MOE_BM = 256
OUT_TILE = 256
IN_TN = 1024
D_IN = 6 * W_GROUP + RWKV_COLS
D_IN_PAD = -(-D_IN // IN_TN) * IN_TN
COL_R, COL_K, COL_V = 6, 7, 8
COL_LOW = (9 * W_GROUP) // LANES
VMEM_LIMIT = 56 * 1024 * 1024


def _cparams(sem):
    return pltpu.CompilerParams(dimension_semantics=sem, vmem_limit_bytes=VMEM_LIMIT)


def _split2(x):
    hi = x.astype(BF16)
    lo = (x - hi.astype(F32)).astype(BF16)
    return hi, lo


def _dot(a, b):
    return jnp.dot(a, b, preferred_element_type=F32)


def _dot_nt(a, b):
    return lax.dot_general(a, b, (((1,), (1,)), ((), ())), preferred_element_type=F32)


def _dot_tn(a, b):
    return lax.dot_general(a, b, (((0,), (0,)), ((), ())), preferred_element_type=F32)


def _dot3(a, b_hi, b_lo):
    a_hi, a_lo = _split2(a)
    return _dot(a_hi, b_hi) + _dot(a_hi, b_lo) + _dot(a_lo, b_hi)


def _segsum(x, e_ref):
    hi, lo = _split2(x)
    e = e_ref[...]
    return _dot(hi, e) + _dot(lo, e)


def _seq_block(rev, b, i, n_c, n_l, n_b):
    if rev:
        ctx = n_b * n_l + b * n_c + (n_c - 1 - i)
        lat = b * n_l + (n_l - 1 - (i - n_c))
    else:
        ctx = n_b * n_l + b * n_c + i
        lat = b * n_l + (i - n_c)
    return jnp.where(i < n_c, ctx, lat)


def _seg_pos(t, n_b, n_l, n_c):
    is_lat = t < n_b * n_l
    p = jnp.where(is_lat, t % n_l, (t - n_b * n_l) % n_c)
    n = jnp.where(is_lat, n_l, n_c)
    return p, n


def _mod_body(c_ref, w_ref, b_ref, o_ref):
    c = c_ref[...]
    s = c * jax.nn.sigmoid(c)
    w_hi, w_lo = _split2(w_ref[0])
    o_ref[...] = _dot3(s, w_hi, w_lo) + b_ref[...]


def _modulation(cvec, layer, w_ada, b_ada):
    d6 = w_ada.shape[2]
    tn = 1536
    return pl.pallas_call(
        _mod_body,
        grid=(d6 // tn,),
        in_specs=[pl.BlockSpec((SUBLANES, D_MODEL), lambda j: (0, 0)),
                  pl.BlockSpec((1, D_MODEL, tn), lambda j: (layer, 0, j)),
                  pl.BlockSpec((1, tn), lambda j: (0, j))],
        out_specs=pl.BlockSpec((SUBLANES, tn), lambda j: (0, j)),
        out_shape=jax.ShapeDtypeStruct((SUBLANES, d6), F32),
        compiler_params=_cparams(("arbitrary",)),
        name="adaln_mod",
    )(cvec, w_ada, b_ada.reshape(1, d6))


def _mod_row(i, tile, n_b, n_seq):
    return jnp.where(i < n_b * n_seq // tile, 1 + i // (n_seq // tile), 0)


def _inproj_body(x_ref, mod_ref, g_ref, w_ref, z_ref, h_ref):
    @pl.when(pl.program_id(1) == 0)
    def _():
        x = x_ref[...]
        ms = jnp.mean(x * x, axis=-1, keepdims=True)
        xn = x * lax.rsqrt(ms + NORM_EPS) * g_ref[...]
        h_ref[...] = (xn * (1.0 + mod_ref[0, 1:2, :]) + mod_ref[0, 0:1, :]).astype(BF16)

    z_ref[...] = _dot(h_ref[...], w_ref[...])


def _inproj(x, mod3, g1, w, n_b, n_seq):
    rows = x.shape[0]
    tn = IN_TN
    return pl.pallas_call(
        _inproj_body,
        grid=(rows // ROW_TILE, w.shape[1] // tn),
        in_specs=[pl.BlockSpec((ROW_TILE, D_MODEL), lambda i, j: (i, 0)),
                  pl.BlockSpec((1, 6, D_MODEL), lambda i, j: (_mod_row(i, ROW_TILE, n_b, n_seq), 0, 0)),
                  pl.BlockSpec((1, D_MODEL), lambda i, j: (0, 0)),
                  pl.BlockSpec((D_MODEL, tn), lambda i, j: (0, j))],
        out_specs=pl.BlockSpec((ROW_TILE, tn), lambda i, j: (i, j)),
        out_shape=jax.ShapeDtypeStruct((rows, w.shape[1]), F32),
        scratch_shapes=[pltpu.VMEM((ROW_TILE, D_MODEL), BF16)],
        compiler_params=_cparams(("arbitrary", "arbitrary")),
        name="norm_inproj",
    )(x, mod3, g1.reshape(1, D_MODEL), w)


S5_SEG = SEQ_TILE // SUBLANES
S5_HALF = 2


def _s5_body(rev, u_ref, bre_ref, bim_ref, dec_ref, dseg_ref, t3_ref, pw_ref, cre_ref, cim_ref,
             y_ref, sr_ref, si_ref, car_ref, loc_ref):
    n_k = S5_SEG
    hw, hs = W_GROUP // S5_HALF, S5_NS // S5_HALF

    @pl.when(pl.program_id(1) == 0)
    def _():
        car_ref[...] = jnp.zeros_like(car_ref)

    ri = lax.broadcasted_iota(jnp.int32, (SEQ_TILE, SEQ_TILE), 0)
    ci = lax.broadcasted_iota(jnp.int32, (SEQ_TILE, SEQ_TILE), 1)
    lg8, lgk = int(math.log2(SUBLANES)), int(math.log2(n_k))
    perm = (ci == ((ri & (SUBLANES - 1)) << lgk) + (ri >> lg8)).astype(BF16)
    up = _dot(perm, u_ref[...].astype(BF16)).astype(BF16)
    for h in range(S5_HALF):
        uh = up[:, h * hw:(h + 1) * hw]
        sr_ref[:, :, h * hs:(h + 1) * hs] = _dot(uh, bre_ref[0, h]).reshape(n_k, SUBLANES, hs)
        si_ref[:, :, h * hs:(h + 1) * hs] = _dot(uh, bim_ref[0, h]).reshape(n_k, SUBLANES, hs)

    loc_ref[...] = jnp.zeros_like(loc_ref)
    ar, ai = dec_ref[0, 0], dec_ref[0, 1]

    def step(k, _):
        kk = (n_k - 1 - k) if rev else k
        cr, ci = loc_ref[0], loc_ref[1]
        nr = ar * cr - ai * ci + sr_ref[kk]
        ni = ar * ci + ai * cr + si_ref[kk]
        sr_ref[kk] = nr
        si_ref[kk] = ni
        loc_ref[0] = nr
        loc_ref[1] = ni
        return 0

    lax.fori_loop(0, n_k, step, 0)

    sub = lax.broadcasted_iota(jnp.int32, (SUBLANES, S5_NS), 0)
    first = (SUBLANES - 1) if rev else 0
    sh1 = (SUBLANES - 1) if rev else 1
    lr, li = loc_ref[0], loc_ref[1]
    er = jnp.where(sub == first, car_ref[0], pltpu.roll(lr, sh1, axis=0))
    ei = jnp.where(sub == first, car_ref[1], pltpu.roll(li, sh1, axis=0))
    for lvl, s in enumerate((1, 2, 4)):
        sh = (SUBLANES - s) if rev else s
        pr, pi = pltpu.roll(er, sh, axis=0), pltpu.roll(ei, sh, axis=0)
        tr, ti = t3_ref[0, 0, lvl], t3_ref[0, 1, lvl]
        er, ei = er + tr * pr - ti * pi, ei + tr * pi + ti * pr
    dr, di = dseg_ref[0, 0], dseg_ref[0, 1]
    last = 0 if rev else SUBLANES - 1
    fr = lr + dr * er - di * ei
    fi = li + dr * ei + di * er
    car_ref[0] = jnp.broadcast_to(fr[last:last + 1, :], (SUBLANES, S5_NS))
    car_ref[1] = jnp.broadcast_to(fi[last:last + 1, :], (SUBLANES, S5_NS))

    pr, pi = pw_ref[0, 0], pw_ref[0, 1]
    s_r = (sr_ref[...] + pr * er[None] - pi * ei[None]).reshape(SEQ_TILE, S5_NS).astype(BF16)
    s_i = (si_ref[...] + pr * ei[None] + pi * er[None]).reshape(SEQ_TILE, S5_NS).astype(BF16)
    ys = []
    for h in range(S5_HALF):
        cols = slice(h * hs, (h + 1) * hs)
        ys.append(_dot(s_r[:, cols], cre_ref[0, h]) - _dot(s_i[:, cols], cim_ref[0, h]))
    y_hi, y_lo = _split2(jnp.concatenate(ys, axis=1))
    inv = (ci == ((ri & (n_k - 1)) << lg8) + (ri >> lgk)).astype(BF16)
    y_ref[...] = _dot(inv, y_hi) + _dot(inv, y_lo)


def _s5_scan(za, p, d, n_b, n_l, n_c):
    rev = d == 1
    rows = za.shape[0]
    seq = lambda b, i: (_seq_block(rev, b, i, n_c, n_l, n_b), 0)
    cst = lambda nd: (lambda b, i: (d,) + (0,) * (nd - 1))
    hw, hs = W_GROUP // S5_HALF, S5_NS // S5_HALF
    state = pltpu.VMEM((S5_SEG, SUBLANES, S5_NS), F32)
    pair = pltpu.VMEM((2, SUBLANES, S5_NS), F32)
    return pl.pallas_call(
        functools.partial(_s5_body, rev),
        grid=(n_b, n_c + n_l),
        in_specs=[pl.BlockSpec((SEQ_TILE, W_GROUP), seq),
                  pl.BlockSpec((1, S5_HALF, hw, hs), cst(4)),
                  pl.BlockSpec((1, S5_HALF, hw, hs), cst(4)),
                  pl.BlockSpec((1, 2, SUBLANES, S5_NS), cst(4)),
                  pl.BlockSpec((1, 2, SUBLANES, S5_NS), cst(4)),
                  pl.BlockSpec((1, 2, 3, SUBLANES, S5_NS), cst(5)),
                  pl.BlockSpec((1, 2, S5_SEG, SUBLANES, S5_NS), cst(5)),
                  pl.BlockSpec((1, S5_HALF, hs, hw), cst(4)),
                  pl.BlockSpec((1, S5_HALF, hs, hw), cst(4))],
        out_specs=pl.BlockSpec((SEQ_TILE, W_GROUP), seq),
        out_shape=jax.ShapeDtypeStruct((rows, W_GROUP), F32),
        scratch_shapes=[state, state, pair, pair],
        compiler_params=_cparams(("arbitrary", "arbitrary")),
        name="s5_scan_rev" if rev else "s5_scan_fwd",
    )(za, p["bre"], p["bim"], p["dec"], p["dseg"], p["t3"], p["pw"], p["cre"], p["cim"])


def _s5_prepare(lam_re, lam_im, log_step, b_re, b_im, c_re, c_im):
    lr, li = lam_re.astype(F32), lam_im.astype(F32)
    dt = jnp.exp(log_step.astype(F32))[..., None]
    mag = jnp.exp(lr * dt)
    ar, ai = mag * jnp.cos(li * dt), mag * jnp.sin(li * dt)
    den = lr * lr + li * li
    cr = ((ar - 1.0) * lr + ai * li) / den
    ci = (ai * lr - (ar - 1.0) * li) / den
    bbr = cr[..., None] * b_re - ci[..., None] * b_im
    bbi = cr[..., None] * b_im + ci[..., None] * b_re
    gh = S5_GROUPS // S5_HALF
    eye = jnp.eye(gh, dtype=F32)

    def bd(t):
        t = t.reshape(2, S5_HALF, gh, S5_STATE, S5_CH)
        return jnp.einsum('dxgph,gk->dxghkp', t, eye).reshape(2, S5_HALF, gh * S5_CH, gh * S5_STATE).astype(BF16)

    def cd(t):
        t = t.astype(F32).reshape(2, S5_HALF, gh, S5_CH, S5_STATE)
        return jnp.einsum('dxghp,gk->dxgpkh', t, eye).reshape(2, S5_HALF, gh * S5_STATE, gh * S5_CH).astype(BF16)

    cmul = lambda x, y: (x[0] * y[0] - x[1] * y[1], x[0] * y[1] + x[1] * y[0])
    dec = (ar.reshape(2, S5_NS), ai.reshape(2, S5_NS))
    pw = [dec]
    for _ in range(S5_SEG - 1):
        pw.append(cmul(pw[-1], dec))
    seg = [pw[-1]]
    for _ in range(2):
        seg.append(cmul(seg[-1], seg[-1]))
    wide = lambda t: jnp.broadcast_to(t[:, None, :], (2, SUBLANES, S5_NS))
    pair = lambda z: jnp.stack([wide(z[0]), wide(z[1])], axis=1)
    k = jnp.arange(SUBLANES)
    t3 = []
    for j, s in enumerate((1, 2, 4)):
        m = jnp.stack([k >= s, k + s <= SUBLANES - 1])
        t3.append(jnp.where(m[:, None, :, None], pair(seg[j]), 0.0))
    t3 = jnp.stack(t3, axis=2)
    pw_f = jnp.stack([jnp.stack([wide(p[0])[0], wide(p[1])[0]]) for p in pw], axis=1)
    pw_b = jnp.stack([jnp.stack([wide(p[0])[1], wide(p[1])[1]]) for p in reversed(pw)], axis=1)
    return dict(bre=bd(bbr), bim=bd(bbi), cre=cd(c_re), cim=cd(c_im), dec=pair(dec), dseg=pair(seg[0]),
                t3=t3, pw=jnp.stack([pw_f, pw_b]))


def _s5_finish_body(yf_ref, yb_ref, u_ref, d_ref, w_ref, b_ref, o_ref):
    y = jax.nn.gelu(yf_ref[...] + yb_ref[...] + d_ref[...] * u_ref[...])
    gate = jax.nn.sigmoid(_dot(y.astype(BF16), w_ref[...]) + b_ref[...])
    o_ref[...] = (y * gate).astype(BF16)


def _s5_finish(yf, yb, za, d_skip, w_glu, b_glu):
    rows = yf.shape[0]
    blk = pl.BlockSpec((ROW_TILE, W_GROUP), lambda i: (i, 0))
    vec = pl.BlockSpec((1, W_GROUP), lambda i: (0, 0))
    return pl.pallas_call(
        _s5_finish_body,
        grid=(rows // ROW_TILE,),
        in_specs=[blk, blk, blk, vec, pl.BlockSpec((W_GROUP, W_GROUP), lambda i: (0, 0)), vec],
        out_specs=blk,
        out_shape=jax.ShapeDtypeStruct((rows, W_GROUP), BF16),
        compiler_params=_cparams(("arbitrary",)),
        name="s5_finish",
    )(yf, yb, za, d_skip.reshape(1, W_GROUP), w_glu.astype(BF16), b_glu.reshape(1, W_GROUP))


HALO = SUBLANES


def _halo_maps(tile_fn, rows, col):
    per = SEQ_TILE // HALO
    last = rows // HALO - 1
    prv = lambda *ids: (jnp.maximum(tile_fn(*ids) * per - 1, 0), col)
    nxt = lambda *ids: (jnp.minimum((tile_fn(*ids) + 1) * per, last), col)
    return prv, nxt


def _lru_body(rev, n_b, n_l, n_c, xp_ref, xc_ref, xn_ref, cw_ref, cb_ref, wa_ref, ba_ref, wx_ref, bx_ref,
              sp_ref, h_ref, buf_ref, a_ref, b_ref, car_ref):
    g_n = SEQ_TILE // SUBLANES
    b = pl.program_id(0)
    i = pl.program_id(1)

    @pl.when(i == 0)
    def _():
        car_ref[...] = jnp.zeros_like(car_ref)

    p, n = _seg_pos(_seq_block(rev, b, i, n_c, n_l, n_b), n_b, n_l, n_c)
    prev_ok = (p > 0).astype(F32)
    next_ok = (p < n - 1).astype(F32)
    buf_ref[0:HALO, :] = xp_ref[...] * prev_ok
    buf_ref[HALO:HALO + SEQ_TILE, :] = xc_ref[...]
    buf_ref[HALO + SEQ_TILE:, :] = xn_ref[...] * next_ok
    v = cb_ref[...] + cw_ref[2:3, :] * xc_ref[...]
    v = v + cw_ref[0:1, :] * buf_ref[HALO - 2:HALO - 2 + SEQ_TILE, :]
    v = v + cw_ref[1:2, :] * buf_ref[HALO - 1:HALO - 1 + SEQ_TILE, :]
    v = v + cw_ref[3:4, :] * buf_ref[HALO + 1:HALO + 1 + SEQ_TILE, :]
    vb = v.astype(BF16)
    r = jax.nn.sigmoid(_dot(vb, wa_ref[0]) + ba_ref[0])
    ig = jax.nn.sigmoid(_dot(vb, wx_ref[0]) + bx_ref[0])
    a = jnp.exp(-LRU_C * r * sp_ref[0])
    bb = jnp.sqrt(1.0 - a * a) * (ig * v)
    a3 = a.reshape(g_n, SUBLANES, W_GROUP)
    b3 = bb.reshape(g_n, SUBLANES, W_GROUP)
    k = lax.broadcasted_iota(jnp.int32, (g_n, SUBLANES, W_GROUP), 1)
    for s in (1, 2, 4):
        sh = (SUBLANES - s) if rev else s
        m = (k + s <= SUBLANES - 1) if rev else (k >= s)
        a_s = pltpu.roll(a3, sh, axis=1)
        b_s = pltpu.roll(b3, sh, axis=1)
        b3 = jnp.where(m, a3 * b_s + b3, b3)
        a3 = jnp.where(m, a3 * a_s, a3)
    a_ref[...] = a3
    b_ref[...] = b3
    row = 0 if rev else SUBLANES - 1

    def step(g, _):
        gg = (g_n - 1 - g) if rev else g
        hh = b_ref[gg] + a_ref[gg] * car_ref[...]
        b_ref[gg] = hh
        car_ref[...] = jnp.broadcast_to(hh[row:row + 1, :], (SUBLANES, W_GROUP))
        return 0

    lax.fori_loop(0, g_n, step, 0)
    h_ref[...] = b_ref[...].reshape(SEQ_TILE, W_GROUP)


def _lru_scan(za, p, d, n_b, n_l, n_c):
    rev = d == 1
    rows = za.shape[0]
    tile = lambda b, i: _seq_block(rev, b, i, n_c, n_l, n_b)
    cur = lambda b, i: (tile(b, i), 1)
    prv, nxt = _halo_maps(tile, rows, 1)
    out = lambda b, i: (tile(b, i), 0)
    vec = pl.BlockSpec((1, W_GROUP), lambda b, i: (0, 0))
    dvec = pl.BlockSpec((1, 1, W_GROUP), lambda b, i: (d, 0, 0))
    dmat = pl.BlockSpec((1, W_GROUP, W_GROUP), lambda b, i: (d, 0, 0))
    blk = lambda f: pl.BlockSpec((SEQ_TILE, W_GROUP), f)
    halo = lambda f: pl.BlockSpec((HALO, W_GROUP), f)
    g_n = SEQ_TILE // SUBLANES
    return pl.pallas_call(
        functools.partial(_lru_body, rev, n_b, n_l, n_c),
        grid=(n_b, n_c + n_l),
        in_specs=[halo(prv), blk(cur), halo(nxt),
                  pl.BlockSpec((4, W_GROUP), lambda b, i: (0, 0)), vec,
                  dmat, dvec, dmat, dvec, dvec],
        out_specs=blk(out),
        out_shape=jax.ShapeDtypeStruct((rows, W_GROUP), F32),
        scratch_shapes=[pltpu.VMEM((SEQ_TILE + 2 * HALO, W_GROUP), F32),
                        pltpu.VMEM((g_n, SUBLANES, W_GROUP), F32),
                        pltpu.VMEM((g_n, SUBLANES, W_GROUP), F32),
                        pltpu.VMEM((SUBLANES, W_GROUP), F32)],
        compiler_params=_cparams(("arbitrary", "arbitrary")),
        name="lru_scan_rev" if rev else "lru_scan_fwd",
    )(za, za, za, p["cw"], p["cb"], p["wa"], p["ba"], p["wx"], p["bx"], p["sp"])


def _blockdiag(w):
    nb, c = w.shape[1], w.shape[2]
    eye = jnp.eye(nb, dtype=F32)
    return jnp.einsum('dncf,nm->dncmf', w.astype(F32), eye).reshape(2, nb * c, nb * c)


def _lru_finish_body(hf_ref, hb_ref, g_ref, o_ref):
    o_ref[...] = ((hf_ref[...] + hb_ref[...]) * jax.nn.gelu(g_ref[...])).astype(BF16)


def _lru_finish(hf, hb, za):
    rows = hf.shape[0]
    blk = pl.BlockSpec((ROW_TILE, W_GROUP), lambda i: (i, 0))
    return pl.pallas_call(
        _lru_finish_body,
        grid=(rows // ROW_TILE,),
        in_specs=[blk, blk, pl.BlockSpec((ROW_TILE, W_GROUP), lambda i: (i, 2))],
        out_specs=blk,
        out_shape=jax.ShapeDtypeStruct((rows, W_GROUP), BF16),
        compiler_params=_cparams(("arbitrary",)),
        name="lru_finish",
    )(hf, hb, za)


def _qkv_prep_body(n_lat_tiles, q_ref, k_ref, v_ref, cos_ref, s1_ref, s2_ref, gq_ref, gk_ref, e_ref,
                   qo_ref, ko_ref, vo_ref):
    reps = W_GROUP // LANES
    is_ctx = pl.program_id(0) >= n_lat_tiles
    cos = jnp.concatenate([jnp.where(is_ctx, 1.0, cos_ref[...])] * reps, axis=1)
    s1 = jnp.concatenate([jnp.where(is_ctx, 0.0, s1_ref[...])] * reps, axis=1)
    s2 = jnp.concatenate([jnp.where(is_ctx, 0.0, s2_ref[...])] * reps, axis=1)

    def prep(x, g):
        ms = _segsum(x * x, e_ref) * (1.0 / DIFF_D)
        x = x * lax.rsqrt(ms + NORM_EPS) * g
        return x * cos + pltpu.roll(x, ROPE_AX, axis=1) * s1 + pltpu.roll(x, W_GROUP - ROPE_AX, axis=1) * s2

    qo_ref[...] = (prep(q_ref[...], gq_ref[...]) * (DIFF_D ** -0.5)).astype(BF16)
    ko_ref[...] = prep(k_ref[...], gk_ref[...]).astype(BF16)
    vo_ref[...] = v_ref[...].astype(BF16)


def _qkv_prep(za, rope, gq, gk, e_seg, n_b, n_seq):
    rows = za.shape[0]
    per_batch = n_seq // ROW_TILE
    col = lambda c: pl.BlockSpec((ROW_TILE, W_GROUP), lambda i: (i, c))
    tab = pl.BlockSpec((ROW_TILE, LANES), lambda i: (i % per_batch, 0))
    vec = pl.BlockSpec((1, W_GROUP), lambda i: (0, 0))
    out = pl.BlockSpec((ROW_TILE, W_GROUP), lambda i: (i, 0))
    reps = W_GROUP // DIFF_D
    return pl.pallas_call(
        functools.partial(_qkv_prep_body, n_b * per_batch),
        grid=(rows // ROW_TILE,),
        in_specs=[col(3), col(4), col(5), tab, tab, tab, vec, vec,
                  pl.BlockSpec((W_GROUP, W_GROUP), lambda i: (0, 0))],
        out_specs=[out, out, out],
        out_shape=[jax.ShapeDtypeStruct((rows, W_GROUP), BF16)] * 3,
        compiler_params=_cparams(("arbitrary",)),
        name="attn_qkv_prep",
    )(za, za, za, rope[0], rope[1], rope[2], jnp.tile(gq, reps).reshape(1, W_GROUP),
      jnp.tile(gk, reps).reshape(1, W_GROUP), e_seg)


def _attn_body(seg_lens, out_scale, lam_ref, sub_ref, q_ref, *rest):
    o_ref, q2_ref, m_ref, acc_ref, sa_ref, sb_ref = rest[-6:]
    kv = rest[:-6]
    tq = q_ref.shape[0]
    q = q_ref[...]
    lane = lax.broadcasted_iota(jnp.int32, (tq, LANES), 1)
    zero = jnp.zeros_like(q)
    q2 = jnp.concatenate([jnp.where(lane < DIFF_D, q, zero), jnp.where(lane < DIFF_D, zero, q)], axis=0)
    q2_ref[...] = q2
    m_ref[...] = jnp.full_like(m_ref, -jnp.inf)
    acc_ref[...] = jnp.zeros_like(acc_ref)
    n_rb = (2 * tq) // ATT_RB

    rows = [slice(rb * ATT_RB, (rb + 1) * ATT_RB) for rb in range(n_rb)]

    def scores(kc, s_ref):
        for rs in rows:
            s_ref[rs, 0:kc.shape[0]] = _dot_nt(q2_ref[rs, :], kc)

    def softmax_pv(s_ref, vc):
        tk = vc.shape[0]
        v_ext = jnp.concatenate([vc, jnp.ones_like(vc)], axis=1)
        for rs in rows:
            s = s_ref[rs, 0:tk]
            m_old = m_ref[rs, :]
            m_new = jnp.maximum(m_old, jnp.max(s, axis=-1, keepdims=True))
            alpha = jnp.exp(m_old - m_new)
            pm = jnp.exp((s - jnp.concatenate([m_new] * (tk // LANES), axis=1)).astype(BF16))
            acc_ref[rs, :] = jnp.concatenate([alpha, alpha], axis=1) * acc_ref[rs, :] + _dot(pm, v_ext)
            m_ref[rs, :] = m_new

    for si, n_k in enumerate(seg_lens):
        k_ref, v_ref = kv[2 * si], kv[2 * si + 1]
        tk = min(ATT_TK, n_k)
        n_ch = n_k // tk
        if n_ch < 2:
            for j in range(n_ch):
                scores(k_ref[j * tk:(j + 1) * tk, :], sa_ref)
                softmax_pv(sa_ref, v_ref[j * tk:(j + 1) * tk, :])
            continue
        unroll = min(ATT_UNROLL, n_ch)
        assert unroll % 2 == 0 and n_ch % unroll == 0
        chunk = lambda ref, j, tk=tk: ref[pl.ds(pl.multiple_of(j * tk, tk), tk), :]
        scores(chunk(k_ref, 0), sa_ref)

        def body(jj, _, k_ref=k_ref, v_ref=v_ref, n_ch=n_ch, unroll=unroll):
            for u in range(unroll):
                j = unroll * jj + u
                cur, nxt = (sa_ref, sb_ref) if u % 2 == 0 else (sb_ref, sa_ref)
                scores(chunk(k_ref, jnp.minimum(j + 1, n_ch - 1)), nxt)
                softmax_pv(cur, chunk(v_ref, j))
            return 0

        lax.fori_loop(0, n_ch // unroll, body, 0)

    o = acc_ref[:, 0:LANES] / acc_ref[:, LANES:]
    o = o[:tq] - lam_ref[...] * o[tq:]
    ms = jnp.mean(o * o, axis=-1, keepdims=True)
    o_ref[...] = (o * lax.rsqrt(ms + NORM_EPS) * sub_ref[...] * out_scale).astype(BF16)


def _attention(qn, kn, vn, lam_vec, subln, lam_init, n_b, n_seq, n_ctx):
    vec = lambda nd: pl.BlockSpec((1, LANES), (lambda b, h, i: (0, 0)) if nd == 3 else (lambda b, h: (0, 0)))
    scratch = lambda tq: [pltpu.VMEM((2 * tq, LANES), BF16), pltpu.VMEM((2 * tq, LANES), F32),
                          pltpu.VMEM((2 * tq, 2 * LANES), F32),
                          pltpu.VMEM((2 * tq, ATT_TK), F32), pltpu.VMEM((2 * tq, ATT_TK), F32)]
    sub = subln.reshape(1, LANES)
    ctx_blk0 = n_b * n_seq // n_ctx
    qpb = n_seq // ATT_TQ
    lat_kv = pl.BlockSpec((n_seq, LANES), lambda b, h, i: (b, h))
    ctx_kv = pl.BlockSpec((n_ctx, LANES), lambda b, h, i: (ctx_blk0 + b, h))
    o_lat = pl.pallas_call(
        functools.partial(_attn_body, (n_seq, n_ctx), 1.0 - lam_init),
        grid=(n_b, DIFF_HEADS, qpb),
        in_specs=[vec(3), vec(3), pl.BlockSpec((ATT_TQ, LANES), lambda b, h, i: (b * qpb + i, h)),
                  lat_kv, lat_kv, ctx_kv, ctx_kv],
        out_specs=pl.BlockSpec((ATT_TQ, LANES), lambda b, h, i: (b * qpb + i, h)),
        out_shape=jax.ShapeDtypeStruct((n_b * n_seq, W_GROUP), BF16),
        scratch_shapes=scratch(ATT_TQ),
        compiler_params=_cparams(("arbitrary", "arbitrary", "arbitrary")),
        name="diff_attn_latent",
    )(lam_vec, sub, qn, kn, vn, kn, vn)
    ckv = pl.BlockSpec((n_ctx, LANES), lambda b, h: (ctx_blk0 + b, h))
    o_ctx = pl.pallas_call(
        functools.partial(_attn_body, (n_ctx,), 1.0 - lam_init),
        grid=(n_b, DIFF_HEADS),
        in_specs=[vec(2), vec(2), ckv, ckv, ckv],
        out_specs=pl.BlockSpec((n_ctx, LANES), lambda b, h: (b, h)),
        out_shape=jax.ShapeDtypeStruct((n_b * n_ctx, W_GROUP), BF16),
        scratch_shapes=scratch(n_ctx),
        compiler_params=_cparams(("arbitrary", "arbitrary")),
        name="diff_attn_context",
    )(lam_vec, sub, qn, kn, vn)
    return o_lat, o_ctx


def _rope_tables(n_seq):
    pos = jnp.arange(n_seq)
    row = (pos // GRID_W).astype(F32)
    col = (pos % GRID_W).astype(F32)
    inv_freq = ROPE_BASE ** (-jnp.arange(ROPE_AX, dtype=F32) / ROPE_AX)
    ang_r, ang_c = row[:, None] * inv_freq, col[:, None] * inv_freq
    z = jnp.zeros_like(ang_r)
    cos = jnp.concatenate([jnp.cos(ang_r)] * 2 + [jnp.cos(ang_c)] * 2, axis=1)
    s1 = jnp.concatenate([z, jnp.sin(ang_r), z, jnp.sin(ang_c)], axis=1)
    s2 = jnp.concatenate([-jnp.sin(ang_r), z, -jnp.sin(ang_c), z], axis=1)

    wide = lambda t: jnp.tile(t, (1, LANES // DIFF_D))
    return wide(cos), wide(s1), wide(s2)


def _rwkv_prep_body(n_b, n_l, n_c,
                    rp_ref, rc_ref, rn_ref, kp_ref, kc_ref, kn_ref, vp_ref, vc_ref, vn_ref, lp_ref, lc_ref, ln_ref,
                    mu_ref, w2_ref, a2_ref, g2_ref, w0_ref, a0_ref, kk_w_ref, ka_ref, rk_ref, e_ref,
                    r_ref, v_ref, kk_ref, lw0_ref, kd0_ref, b0_ref, lw1_ref, kd1_ref, b1_ref, bon_ref, g_ref,
                    buf_ref, bufl_ref):
    t = pl.program_id(0)
    p, n = _seg_pos(t, n_b, n_l, n_c)
    prev_ok = (p > 0).astype(F32)
    next_ok = (p < n - 1).astype(F32)

    def shifted(zp_ref, zc_ref, zn_ref, buf, mu):
        buf[0:HALO, :] = zp_ref[...] * prev_ok
        buf[HALO:HALO + SEQ_TILE, :] = zc_ref[...]
        buf[HALO + SEQ_TILE:, :] = zn_ref[...] * next_ok
        z = zc_ref[...]
        return z + mu * (0.5 * (buf[HALO - 1:HALO - 1 + SEQ_TILE, :] + buf[HALO + 1:HALO + 1 + SEQ_TILE, :]) - z)

    r = shifted(rp_ref, rc_ref, rn_ref, buf_ref, mu_ref[:, 0:W_GROUP])
    k = shifted(kp_ref, kc_ref, kn_ref, buf_ref, mu_ref[:, W_GROUP:2 * W_GROUP])
    v = shifted(vp_ref, vc_ref, vn_ref, buf_ref, mu_ref[:, 2 * W_GROUP:3 * W_GROUP])
    low = shifted(lp_ref, lc_ref, ln_ref, bufl_ref, mu_ref[:, 3 * W_GROUP:])
    kk = k * kk_w_ref[...]
    ss = _segsum(kk * kk, e_ref)
    kk = kk / jnp.maximum(jnp.sqrt(ss), 1e-12)
    tw = jnp.tanh(low).astype(BF16)
    lb = low.astype(BF16)
    r_ref[...] = r
    v_ref[...] = v
    kk_ref[...] = kk
    ksum = jnp.zeros_like(k)
    for d, (lw_ref, kd_ref, b_ref) in enumerate(((lw0_ref, kd0_ref, b0_ref), (lw1_ref, kd1_ref, b1_ref))):
        y = w0_ref[d:d + 1, :] + _dot(tw, w2_ref[d])
        y = -y
        softplus = jnp.maximum(y, 0.0) + jnp.log(1.0 + jnp.exp(-jnp.abs(y)))
        lw_ref[...] = -jnp.exp(-softplus - 0.5)
        ag = jax.nn.sigmoid(a0_ref[d:d + 1, :] + _dot(lb, a2_ref[d]))
        kd = k * (1.0 + (ag - 1.0) * ka_ref[...])
        kd_ref[...] = kd
        b_ref[...] = kk * ag
        ksum = ksum + kd
    bon_ref[...] = _segsum(r * ksum * rk_ref[...], e_ref) * v
    g_ref[...] = _dot(jax.nn.sigmoid(low).astype(BF16), g2_ref[...])


def _rwkv_prep(z, p, e_seg, n_b, n_l, n_c):
    rows = z.shape[0]
    vec = pl.BlockSpec((1, W_GROUP), lambda t: (0, 0))
    two = pl.BlockSpec((2, W_GROUP), lambda t: (0, 0))
    lowm = pl.BlockSpec((2, LANES, W_GROUP), lambda t: (0, 0, 0))
    out = pl.BlockSpec((SEQ_TILE, W_GROUP), lambda t: (t, 0))
    z_specs = []
    for col, width in ((COL_R, W_GROUP), (COL_K, W_GROUP), (COL_V, W_GROUP), (COL_LOW, LANES)):
        prv, nxt = _halo_maps(lambda t: t, rows, col)
        z_specs += [pl.BlockSpec((HALO, width), prv),
                    pl.BlockSpec((SEQ_TILE, width), lambda t, col=col: (t, col)),
                    pl.BlockSpec((HALO, width), nxt)]
    return pl.pallas_call(
        functools.partial(_rwkv_prep_body, n_b, n_l, n_c),
        grid=(rows // SEQ_TILE,),
        in_specs=z_specs + [pl.BlockSpec((1, RWKV_COLS), lambda t: (0, 0)), lowm, lowm,
                            pl.BlockSpec((LANES, W_GROUP), lambda t: (0, 0)), two, two, vec, vec, vec,
                            pl.BlockSpec((W_GROUP, W_GROUP), lambda t: (0, 0))],
        out_specs=[out] * 11,
        out_shape=[jax.ShapeDtypeStruct((rows, W_GROUP), F32)] * 11,
        scratch_shapes=[pltpu.VMEM((SEQ_TILE + 2 * HALO, W_GROUP), F32),
                        pltpu.VMEM((SEQ_TILE + 2 * HALO, LANES), F32)],
        compiler_params=_cparams(("arbitrary",)),
        name="rwkv_prep",
    )(*([z] * 12), p["mu"], p["w2"], p["a2"], p["g2"], p["w0"], p["a0"], p["kk"], p["ka"], p["rk"], e_seg)


def _rwkv_scan_body(*refs):
    c = RW_CHUNK
    in_refs, y_refs, st_ref = refs[:12], refs[12:14], refs[14]
    n_pair = RWKV_HEADS // 2

    @pl.when(pl.program_id(1) == 0)
    def _():
        st_ref[...] = jnp.zeros_like(st_ref)

    rt = lax.broadcasted_iota(jnp.int32, (c, c), 0)
    ct = lax.broadcasted_iota(jnp.int32, (c, c), 1)
    lane = lax.broadcasted_iota(jnp.int32, (c, LANES), 1)
    head0 = lane < RWKV_N
    rr = lax.broadcasted_iota(jnp.int32, (2 * c, 2 * c), 0)
    cc = lax.broadcasted_iota(jnp.int32, (2 * c, 2 * c), 1)
    same = (rr >= c) == (cc >= c)
    tt = rr & (c - 1)
    ss = cc & (c - 1)

    def stack(x):
        z = jnp.zeros_like(x)
        return jnp.concatenate([jnp.where(head0, x, z), jnp.where(head0, z, x)], axis=0).astype(BF16)

    chains = []
    for d in range(2):
        rev = d == 1
        r_ref, v_ref, kk_ref, lw_ref, kd_ref, b_ref = in_refs[6 * d:6 * d + 6]
        tri = ((rt <= ct) if rev else (rt >= ct)).astype(BF16)
        lw = lw_ref[...]
        hi = lw.astype(BF16)
        r1 = lw - hi.astype(F32)
        mid = r1.astype(BF16)
        lo = (r1 - mid.astype(F32)).astype(BF16)
        cum = _dot(tri, hi) + _dot(tri, mid) + _dot(tri, lo)
        last = 0 if rev else c - 1
        tot = cum[last:last + 1, :]
        g_end = jnp.exp(tot - cum)
        g_inv = jnp.exp(-cum)
        full = dict(a=-kk_ref[...] * jnp.exp(cum - lw), r=r_ref[...] * jnp.exp(cum), k=kd_ref[...] * g_inv,
                    b=b_ref[...] * g_inv, kg=kd_ref[...] * g_end, bg=b_ref[...] * g_end, v=v_ref[...])
        g_tot = jnp.exp(tot)
        strict = same & ((tt < ss) if rev else (tt > ss))
        incl = same & ((tt <= ss) if rev else (tt >= ss))
        for p in range(n_pair):
            sl = slice(p * LANES, (p + 1) * LANES)
            ch = {k: stack(x[:, sl]) for k, x in full.items()}
            ch.update(d=d, p=p, sl=sl, strict=strict, incl=incl, g_tot=g_tot[:, sl])
            chains.append(ch)

    c2 = 2 * c
    mask2 = lambda m, x: jnp.where(jnp.concatenate([m, m], axis=1), x, 0.0).astype(BF16)
    for ch in chains:
        kb = jnp.concatenate([ch["b"], ch["k"]], axis=0)
        sc = mask2(ch["strict"], _dot_nt(ch["a"], kb))
        ch["low"], ch["a_k"] = sc[:, :c2], sc[:, c2:]
        ch["r_bk"] = mask2(ch["incl"], _dot_nt(ch["r"], kb))
    for ch in chains:
        ch["st"] = st_ref[ch["d"], ch["p"]]
        ch["stb"] = ch["st"].astype(BF16)
        ch["u"] = _dot_nt(ch["a"], ch["stb"]) + _dot(ch["a_k"], ch["v"])
    n_sq = int(math.log2(c)) - 1
    for it in range(n_sq + 1):
        for ch in chains:
            if it < n_sq:
                both = _dot(ch["low"], jnp.concatenate([ch["low"], ch["u"].astype(BF16)], axis=1))
                ch["low"], ch["u"] = both[:, :c2].astype(BF16), ch["u"] + both[:, c2:]
            else:
                ch["u"] = ch["u"] + _dot(ch["low"], ch["u"].astype(BF16))
    for ch in chains:
        ub = ch["u"].astype(BF16)
        uv = jnp.concatenate([ub, ch["v"]], axis=0)
        y = _dot_nt(ch["r"], ch["stb"]) + _dot(ch["r_bk"], uv)
        y_refs[ch["d"]][:, ch["sl"]] = y[:c] + y[c:]
        st_ref[ch["d"], ch["p"]] = ch["st"] * ch["g_tot"] + _dot_tn(uv, jnp.concatenate([ch["bg"], ch["kg"]], axis=0))


def _rwkv_scan(pre, n_b, n_l, n_c):
    rows = pre[0].shape[0]
    blk = lambda rev: pl.BlockSpec((RW_CHUNK, W_GROUP),
                                   lambda b, i, rev=rev: (_seq_block(rev, b, i, n_c, n_l, n_b), 0))
    args, specs = [], []
    for d in range(2):
        args += [pre[0], pre[1], pre[2], pre[3 + 3 * d], pre[4 + 3 * d], pre[5 + 3 * d]]
        specs += [blk(d == 1)] * 6
    return pl.pallas_call(
        _rwkv_scan_body,
        grid=(n_b, n_c + n_l),
        in_specs=specs,
        out_specs=[blk(False), blk(True)],
        out_shape=[jax.ShapeDtypeStruct((rows, W_GROUP), F32)] * 2,
        scratch_shapes=[pltpu.VMEM((2, RWKV_HEADS // 2, LANES, LANES), F32)],
        compiler_params=_cparams(("arbitrary", "arbitrary")),
        name="rwkv_scan",
    )(*args)


def _rwkv_finish_body(yf_ref, yb_ref, bon_ref, g_ref, lg_ref, lb_ref, e_ref, o_ref):
    y = yf_ref[...] + yb_ref[...]
    dlt = y - _segsum(y, e_ref) * (1.0 / RWKV_N)
    var = _segsum(dlt * dlt, e_ref) * (1.0 / RWKV_N)
    yn = dlt * lax.rsqrt(var + GN_EPS) * lg_ref[...] + lb_ref[...]
    o_ref[...] = ((yn + bon_ref[...]) * g_ref[...]).astype(BF16)


def _rwkv_finish(yf, yb, bonus, gate, lnx_g, lnx_b, e_seg):
    rows = yf.shape[0]
    blk = pl.BlockSpec((ROW_TILE, W_GROUP), lambda i: (i, 0))
    vec = pl.BlockSpec((1, W_GROUP), lambda i: (0, 0))
    return pl.pallas_call(
        _rwkv_finish_body,
        grid=(rows // ROW_TILE,),
        in_specs=[blk, blk, blk, blk, vec, vec, pl.BlockSpec((W_GROUP, W_GROUP), lambda i: (0, 0))],
        out_specs=blk,
        out_shape=jax.ShapeDtypeStruct((rows, W_GROUP), BF16),
        compiler_params=_cparams(("arbitrary",)),
        name="rwkv_finish",
    )(yf, yb, bonus, gate, lnx_g.reshape(1, W_GROUP), lnx_b.reshape(1, W_GROUP), e_seg)


def _outproj_body(x_ref, ya_ref, yb_ref, yc_ref, yd_ref, w_ref, mod_ref, g2_ref, wrh_ref, wrl_ref, br_ref,
                  xo_ref, h2_ref, lg_ref):
    mix = _dot(ya_ref[...], w_ref[0:W_GROUP, :])
    mix = mix + _dot(yb_ref[...], w_ref[W_GROUP:2 * W_GROUP, :])
    mix = mix + _dot(yc_ref[...], w_ref[2 * W_GROUP:3 * W_GROUP, :])
    mix = mix + _dot(yd_ref[...], w_ref[3 * W_GROUP:, :])
    x = x_ref[...] + mod_ref[0, 2:3, :] * mix
    xo_ref[...] = x
    ms = jnp.mean(x * x, axis=-1, keepdims=True)
    h2 = x * lax.rsqrt(ms + NORM_EPS) * g2_ref[...]
    h2 = h2 * (1.0 + mod_ref[0, 4:5, :]) + mod_ref[0, 3:4, :]
    h2_ref[...] = h2.astype(BF16)
    lg_ref[...] = _dot3(h2, wrh_ref[...], wrl_ref[...]) + br_ref[...]


def _outproj(x, ys, w_out, mod3, g2, wr_hi, wr_lo, br, n_b, n_seq):
    rows = x.shape[0]
    big = pl.BlockSpec((OUT_TILE, D_MODEL), lambda i: (i, 0))
    yb = pl.BlockSpec((OUT_TILE, W_GROUP), lambda i: (i, 0))
    cst = lambda shape: pl.BlockSpec(shape, lambda i: (0, 0))
    return pl.pallas_call(
        _outproj_body,
        grid=(rows // OUT_TILE,),
        in_specs=[big, yb, yb, yb, yb, cst((D_MODEL, D_MODEL)),
                  pl.BlockSpec((1, 6, D_MODEL), lambda i: (_mod_row(i, OUT_TILE, n_b, n_seq), 0, 0)),
                  cst((1, D_MODEL)), cst((D_MODEL, LANES)), cst((D_MODEL, LANES)), cst((1, LANES))],
        out_specs=[big, big, pl.BlockSpec((OUT_TILE, LANES), lambda i: (i, 0))],
        out_shape=[jax.ShapeDtypeStruct((rows, D_MODEL), F32),
                   jax.ShapeDtypeStruct((rows, D_MODEL), BF16),
                   jax.ShapeDtypeStruct((rows, LANES), F32)],
        compiler_params=_cparams(("arbitrary",)),
        name="outproj_norm_router",
    )(x, ys[0], ys[1], ys[2], ys[3], w_out, mod3, g2.reshape(1, D_MODEL), wr_hi, wr_lo, br)


def _expert_body(be_ref, nv_ref, new_ref, x_ref, w1_ref, w3_ref, w2_ref, o_ref, w1b_ref, w3b_ref, w2b_ref):
    i = pl.program_id(0)

    @pl.when(new_ref[i] == 1)
    def _():
        w1b_ref[...] = w1_ref[0, 0].astype(BF16)
        w3b_ref[...] = w3_ref[0, 0].astype(BF16)
        w2b_ref[...] = w2_ref[0, 0].astype(BF16)

    @pl.when(i < nv_ref[0])
    def _():
        x = x_ref[...]
        h1 = _dot(x, w1b_ref[...])
        hid = (h1 * jax.nn.sigmoid(h1)) * _dot(x, w3b_ref[...])
        o_ref[...] = _dot(hid.astype(BF16), w2b_ref[...]).astype(o_ref.dtype)

    @pl.when(i >= nv_ref[0])
    def _():
        o_ref[...] = jnp.zeros_like(o_ref)


def _experts(xb, block_e, n_valid, layer, w1, w3, w2):
    n_blocks = xb.shape[0] // MOE_BM
    is_new = jnp.concatenate([jnp.ones((1,), jnp.int32), (block_e[1:] != block_e[:-1]).astype(jnp.int32)])
    wspec = lambda shape: pl.BlockSpec((1, 1) + shape, lambda i, be, nv, nw: (layer, be[i], 0, 0),
                                       pipeline_mode=pl.Buffered(1))
    return pl.pallas_call(
        _expert_body,
        grid_spec=pltpu.PrefetchScalarGridSpec(
            num_scalar_prefetch=3,
            grid=(n_blocks,),
            in_specs=[pl.BlockSpec((MOE_BM, D_MODEL), lambda i, be, nv, nw: (i, 0)),
                      wspec((D_MODEL, D_EXPERT)), wspec((D_MODEL, D_EXPERT)), wspec((D_EXPERT, D_MODEL))],
            out_specs=pl.BlockSpec((MOE_BM, D_MODEL), lambda i, be, nv, nw: (i, 0)),
            scratch_shapes=[pltpu.VMEM((D_MODEL, D_EXPERT), BF16), pltpu.VMEM((D_MODEL, D_EXPERT), BF16),
                            pltpu.VMEM((D_EXPERT, D_MODEL), BF16)]),
        out_shape=jax.ShapeDtypeStruct((xb.shape[0], D_MODEL), BF16),
        compiler_params=_cparams(("arbitrary",)),
        name="moe_experts",
    )(block_e, n_valid, is_new, xb, w1, w3, w2)


def _combine_body(x_ref, y0_ref, y1_ref, rt_ref, mod_ref, o_ref):
    rt = rt_ref[...]
    mo = rt[:, RT_W:RT_W + 1] * y0_ref[...].astype(F32) + rt[:, RT_W + 1:RT_W + 2] * y1_ref[...].astype(F32)
    o_ref[...] = x_ref[...] + mod_ref[0, 5:6, :] * mo


def _combine(x, y0, y1, route, mod3, n_b, n_seq, n_out_rows):
    big = pl.BlockSpec((ROW_TILE, D_MODEL), lambda i: (i, 0))
    return pl.pallas_call(
        _combine_body,
        grid=(n_out_rows // ROW_TILE,),
        in_specs=[big, big, big, pl.BlockSpec((ROW_TILE, LANES), lambda i: (i, 0)),
                  pl.BlockSpec((1, 6, D_MODEL), lambda i: (_mod_row(i, ROW_TILE, n_b, n_seq), 0, 0))],
        out_specs=big,
        out_shape=jax.ShapeDtypeStruct((n_out_rows, D_MODEL), F32),
        compiler_params=_cparams(("arbitrary",)),
        name="moe_combine",
    )(x, y0, y1, route, mod3)


RT_E, RT_RANK, RT_W = 0, 2, 4
LOGIT_E0 = N_EGROUPS


def _route_body(lg_ref, rt_ref, cnt_ref, run_ref):
    tm = lg_ref.shape[0]

    @pl.when(pl.program_id(0) == 0)
    def _():
        run_ref[...] = jnp.zeros_like(run_ref)

    x = lg_ref[...]
    lane = lax.broadcasted_iota(jnp.int32, (tm, LANES), 1).astype(F32)
    neg = jnp.float32(-jnp.inf)

    def first_max(v):
        m = jnp.max(v, axis=-1, keepdims=True)
        return m, jnp.min(jnp.where(v == m, lane, float(LANES)), axis=-1, keepdims=True)

    g = jnp.where(lane < N_EGROUPS, x, neg)
    mg, gsel = first_max(g)
    gate_g = 1.0 / jnp.sum(jnp.exp(g - mg), axis=-1, keepdims=True)
    lo = LOGIT_E0 + EXP_PER_GROUP * gsel
    e = jnp.where((lane >= lo) & (lane < lo + EXP_PER_GROUP), x, neg)
    m1, i1 = first_max(e)
    m2, i2 = first_max(jnp.where(lane == i1, neg, e))
    e2 = jnp.exp(m2 - m1)
    w1 = gate_g / (1.0 + e2)
    w2 = gate_g * e2 / (1.0 + e2)
    oh1 = lane == i1
    oh2 = lane == i2
    oh = (oh1 | oh2).astype(BF16)
    rr = lax.broadcasted_iota(jnp.int32, (tm, tm), 0)
    cc = lax.broadcasted_iota(jnp.int32, (tm, tm), 1)
    before = _dot((rr > cc).astype(BF16), oh) + run_ref[0:1, :]
    r1 = jnp.sum(jnp.where(oh1, before, 0.0), axis=-1, keepdims=True)
    r2 = jnp.sum(jnp.where(oh2, before, 0.0), axis=-1, keepdims=True)
    run_ref[...] = run_ref[...] + jnp.sum(oh.astype(F32), axis=0, keepdims=True)
    cnt_ref[...] = run_ref[...]
    rec = jnp.where(lane == RT_E, i1 - LOGIT_E0, 0.0)
    rec = jnp.where(lane == RT_E + 1, i2 - LOGIT_E0, rec)
    rec = jnp.where(lane == RT_RANK, r1, rec)
    rec = jnp.where(lane == RT_RANK + 1, r2, rec)
    rec = jnp.where(lane == RT_W, w1, rec)
    rt_ref[...] = jnp.where(lane == RT_W + 1, w2, rec)


def _route(logits):
    rows = logits.shape[0]
    blk = pl.BlockSpec((ROW_TILE, LANES), lambda i: (i, 0))
    return pl.pallas_call(
        _route_body,
        grid=(rows // ROW_TILE,),
        in_specs=[blk],
        out_specs=[blk, pl.BlockSpec((SUBLANES, LANES), lambda i: (0, 0))],
        out_shape=[jax.ShapeDtypeStruct((rows, LANES), F32), jax.ShapeDtypeStruct((SUBLANES, LANES), F32)],
        scratch_shapes=[pltpu.VMEM((SUBLANES, LANES), F32)],
        compiler_params=_cparams(("arbitrary",)),
        name="moe_route",
    )(logits)


def _moe(x, h2, logits, mod3, layer, w1, w3, w2, n_b, n_seq, n_out_rows):
    rows = x.shape[0]
    route, cnt = _route(logits)
    counts = cnt[0, LOGIT_E0:LOGIT_E0 + N_EXPERTS].astype(jnp.int32)
    pcounts = ((counts + MOE_BM - 1) // MOE_BM) * MOE_BM
    pend = jnp.cumsum(pcounts)
    pstart = pend - pcounts
    eid = route[:, RT_E:RT_E + 2].astype(jnp.int32)
    dest = pstart[eid] + route[:, RT_RANK:RT_RANK + 2].astype(jnp.int32)
    n_blocks = -(-(2 * rows) // MOE_BM) + N_EXPERTS
    tok = jnp.broadcast_to(jnp.arange(rows, dtype=jnp.int32)[:, None], (rows, 2))
    slot_tok = jnp.zeros((n_blocks * MOE_BM,), jnp.int32).at[dest.reshape(-1)].set(tok.reshape(-1))
    take_rows = lambda a, idx: a.at[idx].get(mode="promise_in_bounds")
    xb = take_rows(h2, slot_tok)
    blk_start = jnp.arange(n_blocks) * MOE_BM
    block_e = jnp.sum(blk_start[:, None] >= pend[None, :], axis=1)
    block_e = jnp.minimum(block_e, N_EXPERTS - 1).astype(jnp.int32)
    n_valid = (pend[-1] // MOE_BM).astype(jnp.int32).reshape(1)
    yb = _experts(xb, block_e, n_valid, layer, w1, w3, w2)
    y0 = take_rows(yb, dest[:n_out_rows, 0])
    y1 = take_rows(yb, dest[:n_out_rows, 1])
    return _combine(x, y0, y1, route, mod3, n_b, n_seq, n_out_rows)


def _pad_rows(w, lo, n):
    pad = [(0, 0)] * (w.ndim - 2) + [(lo, n - lo - w.shape[-2]), (0, 0)]
    return jnp.pad(w.astype(F32), pad).astype(BF16)


def kernel(x, c, ctx, c_ctx, w_ada, b_ada, g_norm1, g_norm2, w_in, w_out, s5_lam_re, s5_lam_im, s5_log_step, s5_b_re, s5_b_im, s5_c_re, s5_c_im, s5_d, s5_w_glu, s5_b_glu, lru_conv_w, lru_conv_b, lru_lam, lru_wa, lru_ba, lru_wx, lru_bx, diff_gq, diff_gk, diff_lq1, diff_lk1, diff_lq2, diff_lk2, diff_subln, rw_mu, rw_w0, rw_w2, rw_a0, rw_a2, rw_g2, rw_kk, rw_ka, rw_rk, rw_lnx_g, rw_lnx_b, moe_w_rg, moe_b_rg, moe_w_re, moe_b_re, moe_w1, moe_w3, moe_w2):
    n_b, n_seq, _ = x.shape
    n_ctx = ctx.shape[1]
    depth = w_ada.shape[0]
    assert n_seq % SEQ_TILE == 0 and n_ctx % SEQ_TILE == 0 and n_seq % n_ctx == 0
    assert (n_b * n_ctx) % ROW_TILE == 0 and n_seq % ROW_TILE == 0 and n_seq % ATT_TK == 0
    xs = jnp.concatenate([x.reshape(n_b * n_seq, D_MODEL), ctx.reshape(n_b * n_ctx, D_MODEL)], axis=0)
    rows = xs.shape[0]
    cvec = jnp.zeros((SUBLANES, D_MODEL), F32).at[0].set(c_ctx).at[1:1 + n_b].set(c)
    rope = _rope_tables(n_seq)
    seg = jnp.arange(W_GROUP) // RWKV_N
    e_seg = (seg[:, None] == seg[None, :]).astype(BF16)
    nl_s, nc_s = n_seq // SEQ_TILE, n_ctx // SEQ_TILE
    nl_r, nc_r = n_seq // RW_CHUNK, n_ctx // RW_CHUNK

    for l in range(depth):
        lam_init = 0.8 - 0.6 * math.exp(-0.3 * l)
        mod = _modulation(cvec, l, w_ada, b_ada[l])
        mod3 = mod[:1 + n_b].reshape(1 + n_b, 6, D_MODEL)
        w_in_b = jnp.pad(w_in[l].astype(BF16), ((0, 0), (0, D_IN_PAD - D_IN)))
        za = _inproj(xs, mod3, g_norm1[l], w_in_b, n_b, n_seq)
        s5p = _s5_prepare(s5_lam_re[l], s5_lam_im[l], s5_log_step[l], s5_b_re[l], s5_b_im[l],
                          s5_c_re[l], s5_c_im[l])
        y5 = [_s5_scan(za, s5p, d, n_b, nl_s, nc_s) for d in range(2)]
        ya = _s5_finish(y5[0], y5[1], za, s5_d[l], s5_w_glu[l], s5_b_glu[l])
        lrp = dict(cw=lru_conv_w[l], cb=lru_conv_b[l].reshape(1, W_GROUP),
                   wa=_blockdiag(lru_wa[l]).astype(BF16), wx=_blockdiag(lru_wx[l]).astype(BF16),
                   ba=lru_ba[l].reshape(2, 1, W_GROUP), bx=lru_bx[l].reshape(2, 1, W_GROUP),
                   sp=jax.nn.softplus(-lru_lam[l]).reshape(2, 1, W_GROUP))
        hl = [_lru_scan(za, lrp, d, n_b, nl_s, nc_s) for d in range(2)]
        yb_ = _lru_finish(hl[0], hl[1], za)
        lam = jnp.exp(jnp.sum(diff_lq1[l] * diff_lk1[l])) - jnp.exp(jnp.sum(diff_lq2[l] * diff_lk2[l])) + lam_init
        lam_vec = jnp.full((1, LANES), lam, F32)
        qn, kn, vn = _qkv_prep(za, rope, diff_gq[l], diff_gk[l], e_seg, n_b, n_seq)
        o_lat, o_ctx = _attention(qn, kn, vn, lam_vec, diff_subln[l], lam_init, n_b, n_seq, n_ctx)
        yc = jnp.concatenate([o_lat, o_ctx], axis=0)
        lo_w, lo_a = RANK_G, RANK_G + RANK_W
        rwp = dict(mu=rw_mu[l].reshape(1, RWKV_COLS),
                   w2=_pad_rows(rw_w2[l], lo_w, LANES), a2=_pad_rows(rw_a2[l], lo_a, LANES),
                   g2=_pad_rows(rw_g2[l], 0, LANES), w0=rw_w0[l], a0=rw_a0[l],
                   kk=rw_kk[l].reshape(1, W_GROUP), ka=rw_ka[l].reshape(1, W_GROUP),
                   rk=rw_rk[l].reshape(1, W_GROUP))
        pre = _rwkv_prep(za, rwp, e_seg, n_b, nl_s, nc_s)
        yr = _rwkv_scan(pre, n_b, nl_r, nc_r)
        yd = _rwkv_finish(yr[0], yr[1], pre[9], pre[10], rw_lnx_g[l], rw_lnx_b[l], e_seg)
        wr = jnp.zeros((D_MODEL, LANES), F32).at[:, :N_EGROUPS].set(moe_w_rg[l])
        wr = wr.at[:, N_EGROUPS:N_EGROUPS + N_EXPERTS].set(moe_w_re[l])
        wr_hi, wr_lo = _split2(wr)
        br = jnp.zeros((1, LANES), F32).at[0, :N_EGROUPS].set(moe_b_rg[l])
        br = br.at[0, N_EGROUPS:N_EGROUPS + N_EXPERTS].set(moe_b_re[l])
        xs, h2, logits = _outproj(xs, (ya, yb_, yc, yd), w_out[l].astype(BF16), mod3, g_norm2[l],
                                  wr_hi, wr_lo, br, n_b, n_seq)
        n_out_rows = rows if l < depth - 1 else n_b * n_seq
        xs = _moe(xs, h2, logits, mod3, l, moe_w1, moe_w3, moe_w2, n_b, n_seq, n_out_rows)
    return xs.reshape(n_b, n_seq, D_MODEL)
```

```python
import functools
import math

import jax
import jax.numpy as jnp
from jax import lax
from jax.experimental import pallas as pl
from jax.experimental.pallas import tpu as pltpu

F32 = jnp.float32
BF16 = jnp.bfloat16

D_MODEL = 2048
W_GROUP = 512
S5_CH, S5_GROUPS, S5_STATE = 16, 32, 64
S5_NS = S5_GROUPS * S5_STATE
LRU_C = 8.0
DIFF_HEADS, DIFF_D = 4, 64
ROPE_AX = 16
ROPE_BASE = 10000.0
RWKV_N, RWKV_HEADS = 64, 8
RANK_G, RANK_W, RANK_A = 64, 32, 32
RWKV_COLS = 3 * W_GROUP + RANK_G + RANK_W + RANK_A
N_EGROUPS, EXP_PER_GROUP, N_EXPERTS = 4, 8, 32
D_EXPERT = 1024
NORM_EPS = 1e-6
GN_EPS = 64e-5
GRID_W = 64

SUBLANES = 8
LANES = 128
ROW_TILE = 512
SEQ_TILE = 256
RW_CHUNK = 64
ATT_TQ = 256
ATT_UNROLL = 32
ATT_TK = 512
ATT_RB = 128
MOE_BM = 256
OUT_TILE = 256
IN_TN = 1024
D_IN = 6 * W_GROUP + RWKV_COLS
D_IN_PAD = -(-D_IN // IN_TN) * IN_TN
COL_R, COL_K, COL_V = 6, 7, 8
COL_LOW = (9 * W_GROUP) // LANES
VMEM_LIMIT = 56 * 1024 * 1024


def _cparams(sem):
    return pltpu.CompilerParams(dimension_semantics=sem, vmem_limit_bytes=VMEM_LIMIT)


def _split2(x):
    hi = x.astype(BF16)
    lo = (x - hi.astype(F32)).astype(BF16)
    return hi, lo


def _dot(a, b):
    return jnp.dot(a, b, preferred_element_type=F32)


def _dot_nt(a, b):
    return lax.dot_general(a, b, (((1,), (1,)), ((), ())), preferred_element_type=F32)


def _dot_tn(a, b):
    return lax.dot_general(a, b, (((0,), (0,)), ((), ())), preferred_element_type=F32)


def _dot3(a, b_hi, b_lo):
    a_hi, a_lo = _split2(a)
    return _dot(a_hi, b_hi) + _dot(a_hi, b_lo) + _dot(a_lo, b_hi)


def _segsum(x, e_ref):
    hi, lo = _split2(x)
    e = e_ref[...]
    return _dot(hi, e) + _dot(lo, e)


def _seq_block(rev, b, i, n_c, n_l, n_b):
    if rev:
        ctx = n_b * n_l + b * n_c + (n_c - 1 - i)
        lat = b * n_l + (n_l - 1 - (i - n_c))
    else:
        ctx = n_b * n_l + b * n_c + i
        lat = b * n_l + (i - n_c)
    return jnp.where(i < n_c, ctx, lat)


def _seg_pos(t, n_b, n_l, n_c):
    is_lat = t < n_b * n_l
    p = jnp.where(is_lat, t % n_l, (t - n_b * n_l) % n_c)
    n = jnp.where(is_lat, n_l, n_c)
    return p, n


def _mod_body(c_ref, w_ref, b_ref, o_ref):
    c = c_ref[...]
    s = c * jax.nn.sigmoid(c)
    w_hi, w_lo = _split2(w_ref[0])
    o_ref[...] = _dot3(s, w_hi, w_lo) + b_ref[...]


def _modulation(cvec, layer, w_ada, b_ada):
    d6 = w_ada.shape[2]
    tn = 1536
    return pl.pallas_call(
        _mod_body,
        grid=(d6 // tn,),
        in_specs=[pl.BlockSpec((SUBLANES, D_MODEL), lambda j: (0, 0)),
                  pl.BlockSpec((1, D_MODEL, tn), lambda j: (layer, 0, j)),
                  pl.BlockSpec((1, tn), lambda j: (0, j))],
        out_specs=pl.BlockSpec((SUBLANES, tn), lambda j: (0, j)),
        out_shape=jax.ShapeDtypeStruct((SUBLANES, d6), F32),
        compiler_params=_cparams(("arbitrary",)),
        name="adaln_mod",
    )(cvec, w_ada, b_ada.reshape(1, d6))


def _mod_row(i, tile, n_b, n_seq):
    return jnp.where(i < n_b * n_seq // tile, 1 + i // (n_seq // tile), 0)


def _inproj_body(x_ref, mod_ref, g_ref, w_ref, z_ref, h_ref):
    @pl.when(pl.program_id(1) == 0)
    def _():
        x = x_ref[...]
        ms = jnp.mean(x * x, axis=-1, keepdims=True)
        xn = x * lax.rsqrt(ms + NORM_EPS) * g_ref[...]
        h_ref[...] = (xn * (1.0 + mod_ref[0, 1:2, :]) + mod_ref[0, 0:1, :]).astype(BF16)

    z_ref[...] = _dot(h_ref[...], w_ref[...])


def _inproj(x, mod3, g1, w, n_b, n_seq):
    rows = x.shape[0]
    tn = IN_TN
    return pl.pallas_call(
        _inproj_body,
        grid=(rows // ROW_TILE, w.shape[1] // tn),
        in_specs=[pl.BlockSpec((ROW_TILE, D_MODEL), lambda i, j: (i, 0)),
                  pl.BlockSpec((1, 6, D_MODEL), lambda i, j: (_mod_row(i, ROW_TILE, n_b, n_seq), 0, 0)),
                  pl.BlockSpec((1, D_MODEL), lambda i, j: (0, 0)),
                  pl.BlockSpec((D_MODEL, tn), lambda i, j: (0, j))],
        out_specs=pl.BlockSpec((ROW_TILE, tn), lambda i, j: (i, j)),
        out_shape=jax.ShapeDtypeStruct((rows, w.shape[1]), F32),
        scratch_shapes=[pltpu.VMEM((ROW_TILE, D_MODEL), BF16)],
        compiler_params=_cparams(("arbitrary", "arbitrary")),
        name="norm_inproj",
    )(x, mod3, g1.reshape(1, D_MODEL), w)


S5_SEG = SEQ_TILE // SUBLANES
S5_HALF = 2


def _s5_body(rev, u_ref, bre_ref, bim_ref, dec_ref, dseg_ref, t3_ref, pw_ref, cre_ref, cim_ref,
             y_ref, sr_ref, si_ref, car_ref, loc_ref):
    n_k = S5_SEG
    hw, hs = W_GROUP // S5_HALF, S5_NS // S5_HALF

    @pl.when(pl.program_id(1) == 0)
    def _():
        car_ref[...] = jnp.zeros_like(car_ref)

    ri = lax.broadcasted_iota(jnp.int32, (SEQ_TILE, SEQ_TILE), 0)
    ci = lax.broadcasted_iota(jnp.int32, (SEQ_TILE, SEQ_TILE), 1)
    lg8, lgk = int(math.log2(SUBLANES)), int(math.log2(n_k))
    perm = (ci == ((ri & (SUBLANES - 1)) << lgk) + (ri >> lg8)).astype(BF16)
    up = _dot(perm, u_ref[...].astype(BF16)).astype(BF16)
    for h in range(S5_HALF):
        uh = up[:, h * hw:(h + 1) * hw]
        sr_ref[:, :, h * hs:(h + 1) * hs] = _dot(uh, bre_ref[0, h]).reshape(n_k, SUBLANES, hs)
        si_ref[:, :, h * hs:(h + 1) * hs] = _dot(uh, bim_ref[0, h]).reshape(n_k, SUBLANES, hs)

    loc_ref[...] = jnp.zeros_like(loc_ref)
    ar, ai = dec_ref[0, 0], dec_ref[0, 1]

    def step(k, _):
        kk = (n_k - 1 - k) if rev else k
        cr, ci = loc_ref[0], loc_ref[1]
        nr = ar * cr - ai * ci + sr_ref[kk]
        ni = ar * ci + ai * cr + si_ref[kk]
        sr_ref[kk] = nr
        si_ref[kk] = ni
        loc_ref[0] = nr
        loc_ref[1] = ni
        return 0

    lax.fori_loop(0, n_k, step, 0)

    sub = lax.broadcasted_iota(jnp.int32, (SUBLANES, S5_NS), 0)
    first = (SUBLANES - 1) if rev else 0
    sh1 = (SUBLANES - 1) if rev else 1
    lr, li = loc_ref[0], loc_ref[1]
    er = jnp.where(sub == first, car_ref[0], pltpu.roll(lr, sh1, axis=0))
    ei = jnp.where(sub == first, car_ref[1], pltpu.roll(li, sh1, axis=0))
    for lvl, s in enumerate((1, 2, 4)):
        sh = (SUBLANES - s) if rev else s
        pr, pi = pltpu.roll(er, sh, axis=0), pltpu.roll(ei, sh, axis=0)
        tr, ti = t3_ref[0, 0, lvl], t3_ref[0, 1, lvl]
        er, ei = er + tr * pr - ti * pi, ei + tr * pi + ti * pr
    dr, di = dseg_ref[0, 0], dseg_ref[0, 1]
    last = 0 if rev else SUBLANES - 1
    fr = lr + dr * er - di * ei
    fi = li + dr * ei + di * er
    car_ref[0] = jnp.broadcast_to(fr[last:last + 1, :], (SUBLANES, S5_NS))
    car_ref[1] = jnp.broadcast_to(fi[last:last + 1, :], (SUBLANES, S5_NS))

    pr, pi = pw_ref[0, 0], pw_ref[0, 1]
    s_r = (sr_ref[...] + pr * er[None] - pi * ei[None]).reshape(SEQ_TILE, S5_NS).astype(BF16)
    s_i = (si_ref[...] + pr * ei[None] + pi * er[None]).reshape(SEQ_TILE, S5_NS).astype(BF16)
    ys = []
    for h in range(S5_HALF):
        cols = slice(h * hs, (h + 1) * hs)
        ys.append(_dot(s_r[:, cols], cre_ref[0, h]) - _dot(s_i[:, cols], cim_ref[0, h]))
    y_hi, y_lo = _split2(jnp.concatenate(ys, axis=1))
    inv = (ci == ((ri & (n_k - 1)) << lg8) + (ri >> lgk)).astype(BF16)
    y_ref[...] = _dot(inv, y_hi) + _dot(inv, y_lo)


def _s5_scan(za, p, d, n_b, n_l, n_c):
    rev = d == 1
    rows = za.shape[0]
    seq = lambda b, i: (_seq_block(rev, b, i, n_c, n_l, n_b), 0)
    cst = lambda nd: (lambda b, i: (d,) + (0,) * (nd - 1))
    hw, hs = W_GROUP // S5_HALF, S5_NS // S5_HALF
    state = pltpu.VMEM((S5_SEG, SUBLANES, S5_NS), F32)
    pair = pltpu.VMEM((2, SUBLANES, S5_NS), F32)
    return pl.pallas_call(
        functools.partial(_s5_body, rev),
        grid=(n_b, n_c + n_l),
        in_specs=[pl.BlockSpec((SEQ_TILE, W_GROUP), seq),
                  pl.BlockSpec((1, S5_HALF, hw, hs), cst(4)),
                  pl.BlockSpec((1, S5_HALF, hw, hs), cst(4)),
                  pl.BlockSpec((1, 2, SUBLANES, S5_NS), cst(4)),
                  pl.BlockSpec((1, 2, SUBLANES, S5_NS), cst(4)),
                  pl.BlockSpec((1, 2, 3, SUBLANES, S5_NS), cst(5)),
                  pl.BlockSpec((1, 2, S5_SEG, SUBLANES, S5_NS), cst(5)),
                  pl.BlockSpec((1, S5_HALF, hs, hw), cst(4)),
                  pl.BlockSpec((1, S5_HALF, hs, hw), cst(4))],
        out_specs=pl.BlockSpec((SEQ_TILE, W_GROUP), seq),
        out_shape=jax.ShapeDtypeStruct((rows, W_GROUP), F32),
        scratch_shapes=[state, state, pair, pair],
        compiler_params=_cparams(("arbitrary", "arbitrary")),
        name="s5_scan_rev" if rev else "s5_scan_fwd",
    )(za, p["bre"], p["bim"], p["dec"], p["dseg"], p["t3"], p["pw"], p["cre"], p["cim"])


def _s5_prepare(lam_re, lam_im, log_step, b_re, b_im, c_re, c_im):
    lr, li = lam_re.astype(F32), lam_im.astype(F32)
    dt = jnp.exp(log_step.astype(F32))[..., None]
    mag = jnp.exp(lr * dt)
    ar, ai = mag * jnp.cos(li * dt), mag * jnp.sin(li * dt)
    den = lr * lr + li * li
    cr = ((ar - 1.0) * lr + ai * li) / den
    ci = (ai * lr - (ar - 1.0) * li) / den
    bbr = cr[..., None] * b_re - ci[..., None] * b_im
    bbi = cr[..., None] * b_im + ci[..., None] * b_re
    gh = S5_GROUPS // S5_HALF
    eye = jnp.eye(gh, dtype=F32)

    def bd(t):
        t = t.reshape(2, S5_HALF, gh, S5_STATE, S5_CH)
        return jnp.einsum('dxgph,gk->dxghkp', t, eye).reshape(2, S5_HALF, gh * S5_CH, gh * S5_STATE).astype(BF16)

    def cd(t):
        t = t.astype(F32).reshape(2, S5_HALF, gh, S5_CH, S5_STATE)
        return jnp.einsum('dxghp,gk->dxgpkh', t, eye).reshape(2, S5_HALF, gh * S5_STATE, gh * S5_CH).astype(BF16)

    cmul = lambda x, y: (x[0] * y[0] - x[1] * y[1], x[0] * y[1] + x[1] * y[0])
    dec = (ar.reshape(2, S5_NS), ai.reshape(2, S5_NS))
    pw = [dec]
    for _ in range(S5_SEG - 1):
        pw.append(cmul(pw[-1], dec))
    seg = [pw[-1]]
    for _ in range(2):
        seg.append(cmul(seg[-1], seg[-1]))
    wide = lambda t: jnp.broadcast_to(t[:, None, :], (2, SUBLANES, S5_NS))
    pair = lambda z: jnp.stack([wide(z[0]), wide(z[1])], axis=1)
    k = jnp.arange(SUBLANES)
    t3 = []
    for j, s in enumerate((1, 2, 4)):
        m = jnp.stack([k >= s, k + s <= SUBLANES - 1])
        t3.append(jnp.where(m[:, None, :, None], pair(seg[j]), 0.0))
    t3 = jnp.stack(t3, axis=2)
    pw_f = jnp.stack([jnp.stack([wide(p[0])[0], wide(p[1])[0]]) for p in pw], axis=1)
    pw_b = jnp.stack([jnp.stack([wide(p[0])[1], wide(p[1])[1]]) for p in reversed(pw)], axis=1)
    return dict(bre=bd(bbr), bim=bd(bbi), cre=cd(c_re), cim=cd(c_im), dec=pair(dec), dseg=pair(seg[0]),
                t3=t3, pw=jnp.stack([pw_f, pw_b]))


def _s5_finish_body(yf_ref, yb_ref, u_ref, d_ref, w_ref, b_ref, o_ref):
    y = jax.nn.gelu(yf_ref[...] + yb_ref[...] + d_ref[...] * u_ref[...])
    gate = jax.nn.sigmoid(_dot(y.astype(BF16), w_ref[...]) + b_ref[...])
    o_ref[...] = (y * gate).astype(BF16)


def _s5_finish(yf, yb, za, d_skip, w_glu, b_glu):
    rows = yf.shape[0]
    blk = pl.BlockSpec((ROW_TILE, W_GROUP), lambda i: (i, 0))
    vec = pl.BlockSpec((1, W_GROUP), lambda i: (0, 0))
    return pl.pallas_call(
        _s5_finish_body,
        grid=(rows // ROW_TILE,),
        in_specs=[blk, blk, blk, vec, pl.BlockSpec((W_GROUP, W_GROUP), lambda i: (0, 0)), vec],
        out_specs=blk,
        out_shape=jax.ShapeDtypeStruct((rows, W_GROUP), BF16),
        compiler_params=_cparams(("arbitrary",)),
        name="s5_finish",
    )(yf, yb, za, d_skip.reshape(1, W_GROUP), w_glu.astype(BF16), b_glu.reshape(1, W_GROUP))


HALO = SUBLANES


def _halo_maps(tile_fn, rows, col):
    per = SEQ_TILE // HALO
    last = rows // HALO - 1
    prv = lambda *ids: (jnp.maximum(tile_fn(*ids) * per - 1, 0), col)
    nxt = lambda *ids: (jnp.minimum((tile_fn(*ids) + 1) * per, last), col)
    return prv, nxt


def _lru_body(rev, n_b, n_l, n_c, xp_ref, xc_ref, xn_ref, cw_ref, cb_ref, wa_ref, ba_ref, wx_ref, bx_ref,
              sp_ref, h_ref, buf_ref, a_ref, b_ref, car_ref):
    g_n = SEQ_TILE // SUBLANES
    b = pl.program_id(0)
    i = pl.program_id(1)

    @pl.when(i == 0)
    def _():
        car_ref[...] = jnp.zeros_like(car_ref)

    p, n = _seg_pos(_seq_block(rev, b, i, n_c, n_l, n_b), n_b, n_l, n_c)
    prev_ok = (p > 0).astype(F32)
    next_ok = (p < n - 1).astype(F32)
    buf_ref[0:HALO, :] = xp_ref[...] * prev_ok
    buf_ref[HALO:HALO + SEQ_TILE, :] = xc_ref[...]
    buf_ref[HALO + SEQ_TILE:, :] = xn_ref[...] * next_ok
    v = cb_ref[...] + cw_ref[2:3, :] * xc_ref[...]
    v = v + cw_ref[0:1, :] * buf_ref[HALO - 2:HALO - 2 + SEQ_TILE, :]
    v = v + cw_ref[1:2, :] * buf_ref[HALO - 1:HALO - 1 + SEQ_TILE, :]
    v = v + cw_ref[3:4, :] * buf_ref[HALO + 1:HALO + 1 + SEQ_TILE, :]
    vb = v.astype(BF16)
    r = jax.nn.sigmoid(_dot(vb, wa_ref[0]) + ba_ref[0])
    ig = jax.nn.sigmoid(_dot(vb, wx_ref[0]) + bx_ref[0])
    a = jnp.exp(-LRU_C * r * sp_ref[0])
    bb = jnp.sqrt(1.0 - a * a) * (ig * v)
    a3 = a.reshape(g_n, SUBLANES, W_GROUP)
    b3 = bb.reshape(g_n, SUBLANES, W_GROUP)
    k = lax.broadcasted_iota(jnp.int32, (g_n, SUBLANES, W_GROUP), 1)
    for s in (1, 2, 4):
        sh = (SUBLANES - s) if rev else s
        m = (k + s <= SUBLANES - 1) if rev else (k >= s)
        a_s = pltpu.roll(a3, sh, axis=1)
        b_s = pltpu.roll(b3, sh, axis=1)
        b3 = jnp.where(m, a3 * b_s + b3, b3)
        a3 = jnp.where(m, a3 * a_s, a3)
    a_ref[...] = a3
    b_ref[...] = b3
    row = 0 if rev else SUBLANES - 1

    def step(g, _):
        gg = (g_n - 1 - g) if rev else g
        hh = b_ref[gg] + a_ref[gg] * car_ref[...]
        b_ref[gg] = hh
        car_ref[...] = jnp.broadcast_to(hh[row:row + 1, :], (SUBLANES, W_GROUP))
        return 0

    lax.fori_loop(0, g_n, step, 0)
    h_ref[...] = b_ref[...].reshape(SEQ_TILE, W_GROUP)


def _lru_scan(za, p, d, n_b, n_l, n_c):
    rev = d == 1
    rows = za.shape[0]
    tile = lambda b, i: _seq_block(rev, b, i, n_c, n_l, n_b)
    cur = lambda b, i: (tile(b, i), 1)
    prv, nxt = _halo_maps(tile, rows, 1)
    out = lambda b, i: (tile(b, i), 0)
    vec = pl.BlockSpec((1, W_GROUP), lambda b, i: (0, 0))
    dvec = pl.BlockSpec((1, 1, W_GROUP), lambda b, i: (d, 0, 0))
    dmat = pl.BlockSpec((1, W_GROUP, W_GROUP), lambda b, i: (d, 0, 0))
    blk = lambda f: pl.BlockSpec((SEQ_TILE, W_GROUP), f)
    halo = lambda f: pl.BlockSpec((HALO, W_GROUP), f)
    g_n = SEQ_TILE // SUBLANES
    return pl.pallas_call(
        functools.partial(_lru_body, rev, n_b, n_l, n_c),
        grid=(n_b, n_c + n_l),
        in_specs=[halo(prv), blk(cur), halo(nxt),
                  pl.BlockSpec((4, W_GROUP), lambda b, i: (0, 0)), vec,
                  dmat, dvec, dmat, dvec, dvec],
        out_specs=blk(out),
        out_shape=jax.ShapeDtypeStruct((rows, W_GROUP), F32),
        scratch_shapes=[pltpu.VMEM((SEQ_TILE + 2 * HALO, W_GROUP), F32),
                        pltpu.VMEM((g_n, SUBLANES, W_GROUP), F32),
                        pltpu.VMEM((g_n, SUBLANES, W_GROUP), F32),
                        pltpu.VMEM((SUBLANES, W_GROUP), F32)],
        compiler_params=_cparams(("arbitrary", "arbitrary")),
        name="lru_scan_rev" if rev else "lru_scan_fwd",
    )(za, za, za, p["cw"], p["cb"], p["wa"], p["ba"], p["wx"], p["bx"], p["sp"])


def _blockdiag(w):
    nb, c = w.shape[1], w.shape[2]
    eye = jnp.eye(nb, dtype=F32)
    return jnp.einsum('dncf,nm->dncmf', w.astype(F32), eye).reshape(2, nb * c, nb * c)


def _lru_finish_body(hf_ref, hb_ref, g_ref, o_ref):
    o_ref[...] = ((hf_ref[...] + hb_ref[...]) * jax.nn.gelu(g_ref[...])).astype(BF16)


def _lru_finish(hf, hb, za):
    rows = hf.shape[0]
    blk = pl.BlockSpec((ROW_TILE, W_GROUP), lambda i: (i, 0))
    return pl.pallas_call(
        _lru_finish_body,
        grid=(rows // ROW_TILE,),
        in_specs=[blk, blk, pl.BlockSpec((ROW_TILE, W_GROUP), lambda i: (i, 2))],
        out_specs=blk,
        out_shape=jax.ShapeDtypeStruct((rows, W_GROUP), BF16),
        compiler_params=_cparams(("arbitrary",)),
        name="lru_finish",
    )(hf, hb, za)


def _qkv_prep_body(n_lat_tiles, q_ref, k_ref, v_ref, cos_ref, s1_ref, s2_ref, gq_ref, gk_ref, e_ref,
                   qo_ref, ko_ref, vo_ref):
    reps = W_GROUP // LANES
    is_ctx = pl.program_id(0) >= n_lat_tiles
    cos = jnp.concatenate([jnp.where(is_ctx, 1.0, cos_ref[...])] * reps, axis=1)
    s1 = jnp.concatenate([jnp.where(is_ctx, 0.0, s1_ref[...])] * reps, axis=1)
    s2 = jnp.concatenate([jnp.where(is_ctx, 0.0, s2_ref[...])] * reps, axis=1)

    def prep(x, g):
        ms = _segsum(x * x, e_ref) * (1.0 / DIFF_D)
        x = x * lax.rsqrt(ms + NORM_EPS) * g
        return x * cos + pltpu.roll(x, ROPE_AX, axis=1) * s1 + pltpu.roll(x, W_GROUP - ROPE_AX, axis=1) * s2

    qo_ref[...] = (prep(q_ref[...], gq_ref[...]) * (DIFF_D ** -0.5)).astype(BF16)
    ko_ref[...] = prep(k_ref[...], gk_ref[...]).astype(BF16)
    vo_ref[...] = v_ref[...].astype(BF16)


def _qkv_prep(za, rope, gq, gk, e_seg, n_b, n_seq):
    rows = za.shape[0]
    per_batch = n_seq // ROW_TILE
    col = lambda c: pl.BlockSpec((ROW_TILE, W_GROUP), lambda i: (i, c))
    tab = pl.BlockSpec((ROW_TILE, LANES), lambda i: (i % per_batch, 0))
    vec = pl.BlockSpec((1, W_GROUP), lambda i: (0, 0))
    out = pl.BlockSpec((ROW_TILE, W_GROUP), lambda i: (i, 0))
    reps = W_GROUP // DIFF_D
    return pl.pallas_call(
        functools.partial(_qkv_prep_body, n_b * per_batch),
        grid=(rows // ROW_TILE,),
        in_specs=[col(3), col(4), col(5), tab, tab, tab, vec, vec,
                  pl.BlockSpec((W_GROUP, W_GROUP), lambda i: (0, 0))],
        out_specs=[out, out, out],
        out_shape=[jax.ShapeDtypeStruct((rows, W_GROUP), BF16)] * 3,
        compiler_params=_cparams(("arbitrary",)),
        name="attn_qkv_prep",
    )(za, za, za, rope[0], rope[1], rope[2], jnp.tile(gq, reps).reshape(1, W_GROUP),
      jnp.tile(gk, reps).reshape(1, W_GROUP), e_seg)


def _attn_body(seg_lens, out_scale, lam_ref, sub_ref, q_ref, *rest):
    o_ref, q2_ref, m_ref, acc_ref, sa_ref, sb_ref = rest[-6:]
    kv = rest[:-6]
    tq = q_ref.shape[0]
    q = q_ref[...]
    lane = lax.broadcasted_iota(jnp.int32, (tq, LANES), 1)
    zero = jnp.zeros_like(q)
    q2 = jnp.concatenate([jnp.where(lane < DIFF_D, q, zero), jnp.where(lane < DIFF_D, zero, q)], axis=0)
    q2_ref[...] = q2
    m_ref[...] = jnp.full_like(m_ref, -jnp.inf)
    acc_ref[...] = jnp.zeros_like(acc_ref)
    n_rb = (2 * tq) // ATT_RB

    rows = [slice(rb * ATT_RB, (rb + 1) * ATT_RB) for rb in range(n_rb)]

    def scores(kc, s_ref):
        for rs in rows:
            s_ref[rs, 0:kc.shape[0]] = _dot_nt(q2_ref[rs, :], kc)

    def softmax_pv(s_ref, vc):
        tk = vc.shape[0]
        v_ext = jnp.concatenate([vc, jnp.ones_like(vc)], axis=1)
        for rs in rows:
            s = s_ref[rs, 0:tk]
            m_old = m_ref[rs, :]
            m_new = jnp.maximum(m_old, jnp.max(s, axis=-1, keepdims=True))
            alpha = jnp.exp(m_old - m_new)
            pm = jnp.exp((s - jnp.concatenate([m_new] * (tk // LANES), axis=1)).astype(BF16))
            acc_ref[rs, :] = jnp.concatenate([alpha, alpha], axis=1) * acc_ref[rs, :] + _dot(pm, v_ext)
            m_ref[rs, :] = m_new

    for si, n_k in enumerate(seg_lens):
        k_ref, v_ref = kv[2 * si], kv[2 * si + 1]
        tk = min(ATT_TK, n_k)
        n_ch = n_k // tk
        if n_ch < 2:
            for j in range(n_ch):
                scores(k_ref[j * tk:(j + 1) * tk, :], sa_ref)
                softmax_pv(sa_ref, v_ref[j * tk:(j + 1) * tk, :])
            continue
        unroll = min(ATT_UNROLL, n_ch)
        assert unroll % 2 == 0 and n_ch % unroll == 0
        chunk = lambda ref, j, tk=tk: ref[pl.ds(pl.multiple_of(j * tk, tk), tk), :]
        scores(chunk(k_ref, 0), sa_ref)

        def body(jj, _, k_ref=k_ref, v_ref=v_ref, n_ch=n_ch, unroll=unroll):
            for u in range(unroll):
                j = unroll * jj + u
                cur, nxt = (sa_ref, sb_ref) if u % 2 == 0 else (sb_ref, sa_ref)
                scores(chunk(k_ref, jnp.minimum(j + 1, n_ch - 1)), nxt)
                softmax_pv(cur, chunk(v_ref, j))
            return 0

        lax.fori_loop(0, n_ch // unroll, body, 0)

    o = acc_ref[:, 0:LANES] / acc_ref[:, LANES:]
    o = o[:tq] - lam_ref[...] * o[tq:]
    ms = jnp.mean(o * o, axis=-1, keepdims=True)
    o_ref[...] = (o * lax.rsqrt(ms + NORM_EPS) * sub_ref[...] * out_scale).astype(BF16)


def _attention(qn, kn, vn, lam_vec, subln, lam_init, n_b, n_seq, n_ctx):
    vec = lambda nd: pl.BlockSpec((1, LANES), (lambda b, h, i: (0, 0)) if nd == 3 else (lambda b, h: (0, 0)))
    scratch = lambda tq: [pltpu.VMEM((2 * tq, LANES), BF16), pltpu.VMEM((2 * tq, LANES), F32),
                          pltpu.VMEM((2 * tq, 2 * LANES), F32),
                          pltpu.VMEM((2 * tq, ATT_TK), F32), pltpu.VMEM((2 * tq, ATT_TK), F32)]
    sub = subln.reshape(1, LANES)
    ctx_blk0 = n_b * n_seq // n_ctx
    qpb = n_seq // ATT_TQ
    lat_kv = pl.BlockSpec((n_seq, LANES), lambda b, h, i: (b, h))
    ctx_kv = pl.BlockSpec((n_ctx, LANES), lambda b, h, i: (ctx_blk0 + b, h))
    o_lat = pl.pallas_call(
        functools.partial(_attn_body, (n_seq, n_ctx), 1.0 - lam_init),
        grid=(n_b, DIFF_HEADS, qpb),
        in_specs=[vec(3), vec(3), pl.BlockSpec((ATT_TQ, LANES), lambda b, h, i: (b * qpb + i, h)),
                  lat_kv, lat_kv, ctx_kv, ctx_kv],
        out_specs=pl.BlockSpec((ATT_TQ, LANES), lambda b, h, i: (b * qpb + i, h)),
        out_shape=jax.ShapeDtypeStruct((n_b * n_seq, W_GROUP), BF16),
        scratch_shapes=scratch(ATT_TQ),
        compiler_params=_cparams(("arbitrary", "arbitrary", "arbitrary")),
        name="diff_attn_latent",
    )(lam_vec, sub, qn, kn, vn, kn, vn)
    ckv = pl.BlockSpec((n_ctx, LANES), lambda b, h: (ctx_blk0 + b, h))
    o_ctx = pl.pallas_call(
        functools.partial(_attn_body, (n_ctx,), 1.0 - lam_init),
        grid=(n_b, DIFF_HEADS),
        in_specs=[vec(2), vec(2), ckv, ckv, ckv],
        out_specs=pl.BlockSpec((n_ctx, LANES), lambda b, h: (b, h)),
        out_shape=jax.ShapeDtypeStruct((n_b * n_ctx, W_GROUP), BF16),
        scratch_shapes=scratch(n_ctx),
        compiler_params=_cparams(("arbitrary", "arbitrary")),
        name="diff_attn_context",
    )(lam_vec, sub, qn, kn, vn)
    return o_lat, o_ctx


def _rope_tables(n_seq):
    pos = jnp.arange(n_seq)
    row = (pos // GRID_W).astype(F32)
    col = (pos % GRID_W).astype(F32)
    inv_freq = ROPE_BASE ** (-jnp.arange(ROPE_AX, dtype=F32) / ROPE_AX)
    ang_r, ang_c = row[:, None] * inv_freq, col[:, None] * inv_freq
    z = jnp.zeros_like(ang_r)
    cos = jnp.concatenate([jnp.cos(ang_r)] * 2 + [jnp.cos(ang_c)] * 2, axis=1)
    s1 = jnp.concatenate([z, jnp.sin(ang_r), z, jnp.sin(ang_c)], axis=1)
    s2 = jnp.concatenate([-jnp.sin(ang_r), z, -jnp.sin(ang_c), z], axis=1)

    wide = lambda t: jnp.tile(t, (1, LANES // DIFF_D))
    return wide(cos), wide(s1), wide(s2)


def _rwkv_prep_body(n_b, n_l, n_c,
                    rp_ref, rc_ref, rn_ref, kp_ref, kc_ref, kn_ref, vp_ref, vc_ref, vn_ref, lp_ref, lc_ref, ln_ref,
                    mu_ref, w2_ref, a2_ref, g2_ref, w0_ref, a0_ref, kk_w_ref, ka_ref, rk_ref, e_ref,
                    r_ref, v_ref, kk_ref, lw0_ref, kd0_ref, b0_ref, lw1_ref, kd1_ref, b1_ref, bon_ref, g_ref,
                    buf_ref, bufl_ref):
    t = pl.program_id(0)
    p, n = _seg_pos(t, n_b, n_l, n_c)
    prev_ok = (p > 0).astype(F32)
    next_ok = (p < n - 1).astype(F32)

    def shifted(zp_ref, zc_ref, zn_ref, buf, mu):
        buf[0:HALO, :] = zp_ref[...] * prev_ok
        buf[HALO:HALO + SEQ_TILE, :] = zc_ref[...]
        buf[HALO + SEQ_TILE:, :] = zn_ref[...] * next_ok
        z = zc_ref[...]
        return z + mu * (0.5 * (buf[HALO - 1:HALO - 1 + SEQ_TILE, :] + buf[HALO + 1:HALO + 1 + SEQ_TILE, :]) - z)

    r = shifted(rp_ref, rc_ref, rn_ref, buf_ref, mu_ref[:, 0:W_GROUP])
    k = shifted(kp_ref, kc_ref, kn_ref, buf_ref, mu_ref[:, W_GROUP:2 * W_GROUP])
    v = shifted(vp_ref, vc_ref, vn_ref, buf_ref, mu_ref[:, 2 * W_GROUP:3 * W_GROUP])
    low = shifted(lp_ref, lc_ref, ln_ref, bufl_ref, mu_ref[:, 3 * W_GROUP:])
    kk = k * kk_w_ref[...]
    ss = _segsum(kk * kk, e_ref)
    kk = kk / jnp.maximum(jnp.sqrt(ss), 1e-12)
    tw = jnp.tanh(low).astype(BF16)
    lb = low.astype(BF16)
    r_ref[...] = r.astype(r_ref.dtype)
    v_ref[...] = v.astype(v_ref.dtype)
    kk_ref[...] = kk.astype(kk_ref.dtype)
    ksum = jnp.zeros_like(k)
    for d, (lw_ref, kd_ref, b_ref) in enumerate(((lw0_ref, kd0_ref, b0_ref), (lw1_ref, kd1_ref, b1_ref))):
        y = w0_ref[d:d + 1, :] + _dot(tw, w2_ref[d])
        y = -y
        softplus = jnp.maximum(y, 0.0) + jnp.log(1.0 + jnp.exp(-jnp.abs(y)))
        lw_ref[...] = -jnp.exp(-softplus - 0.5)
        ag = jax.nn.sigmoid(a0_ref[d:d + 1, :] + _dot(lb, a2_ref[d]))
        kd = k * (1.0 + (ag - 1.0) * ka_ref[...])
        kd_ref[...] = kd.astype(kd_ref.dtype)
        b_ref[...] = (kk * ag).astype(b_ref.dtype)
        ksum = ksum + kd
    bon_ref[...] = (_segsum(r * ksum * rk_ref[...], e_ref) * v).astype(bon_ref.dtype)
    g_ref[...] = _dot(jax.nn.sigmoid(low).astype(BF16), g2_ref[...]).astype(g_ref.dtype)


def _rwkv_prep(z, p, e_seg, n_b, n_l, n_c):
    rows = z.shape[0]
    vec = pl.BlockSpec((1, W_GROUP), lambda t: (0, 0))
    two = pl.BlockSpec((2, W_GROUP), lambda t: (0, 0))
    lowm = pl.BlockSpec((2, LANES, W_GROUP), lambda t: (0, 0, 0))
    out = pl.BlockSpec((SEQ_TILE, W_GROUP), lambda t: (t, 0))
    z_specs = []
    for col, width in ((COL_R, W_GROUP), (COL_K, W_GROUP), (COL_V, W_GROUP), (COL_LOW, LANES)):
        prv, nxt = _halo_maps(lambda t: t, rows, col)
        z_specs += [pl.BlockSpec((HALO, width), prv),
                    pl.BlockSpec((SEQ_TILE, width), lambda t, col=col: (t, col)),
                    pl.BlockSpec((HALO, width), nxt)]
    return pl.pallas_call(
        functools.partial(_rwkv_prep_body, n_b, n_l, n_c),
        grid=(rows // SEQ_TILE,),
        in_specs=z_specs + [pl.BlockSpec((1, RWKV_COLS), lambda t: (0, 0)), lowm, lowm,
                            pl.BlockSpec((LANES, W_GROUP), lambda t: (0, 0)), two, two, vec, vec, vec,
                            pl.BlockSpec((W_GROUP, W_GROUP), lambda t: (0, 0))],
        out_specs=[out] * 11,
        out_shape=[jax.ShapeDtypeStruct((rows, W_GROUP), F32 if j in (3, 6) else BF16) for j in range(11)],
        scratch_shapes=[pltpu.VMEM((SEQ_TILE + 2 * HALO, W_GROUP), F32),
                        pltpu.VMEM((SEQ_TILE + 2 * HALO, LANES), F32)],
        compiler_params=_cparams(("arbitrary",)),
        name="rwkv_prep",
    )(*([z] * 12), p["mu"], p["w2"], p["a2"], p["g2"], p["w0"], p["a0"], p["kk"], p["ka"], p["rk"], e_seg)


def _rwkv_scan_body(*refs):
    c = RW_CHUNK
    in_refs, y_refs, st_ref = refs[:12], refs[12:14], refs[14]
    n_pair = RWKV_HEADS // 2

    @pl.when(pl.program_id(1) == 0)
    def _():
        st_ref[...] = jnp.zeros_like(st_ref)

    rt = lax.broadcasted_iota(jnp.int32, (c, c), 0)
    ct = lax.broadcasted_iota(jnp.int32, (c, c), 1)
    lane = lax.broadcasted_iota(jnp.int32, (c, LANES), 1)
    head0 = lane < RWKV_N
    rr = lax.broadcasted_iota(jnp.int32, (2 * c, 2 * c), 0)
    cc = lax.broadcasted_iota(jnp.int32, (2 * c, 2 * c), 1)
    same = (rr >= c) == (cc >= c)
    tt = rr & (c - 1)
    ss = cc & (c - 1)

    def stack(x):
        z = jnp.zeros_like(x)
        return jnp.concatenate([jnp.where(head0, x, z), jnp.where(head0, z, x)], axis=0).astype(BF16)

    chains = []
    for d in range(2):
        rev = d == 1
        r_ref, v_ref, kk_ref, lw_ref, kd_ref, b_ref = in_refs[6 * d:6 * d + 6]
        tri = ((rt <= ct) if rev else (rt >= ct)).astype(BF16)
        lw = lw_ref[...]
        hi = lw.astype(BF16)
        r1 = lw - hi.astype(F32)
        mid = r1.astype(BF16)
        lo = (r1 - mid.astype(F32)).astype(BF16)
        cum = _dot(tri, hi) + _dot(tri, mid) + _dot(tri, lo)
        last = 0 if rev else c - 1
        tot = cum[last:last + 1, :]
        g_end = jnp.exp(tot - cum)
        g_inv = jnp.exp(-cum)
        full = dict(a=-kk_ref[...] * jnp.exp(cum - lw), r=r_ref[...] * jnp.exp(cum), k=kd_ref[...] * g_inv,
                    b=b_ref[...] * g_inv, kg=kd_ref[...] * g_end, bg=b_ref[...] * g_end, v=v_ref[...])
        g_tot = jnp.exp(tot)
        strict = same & ((tt < ss) if rev else (tt > ss))
        incl = same & ((tt <= ss) if rev else (tt >= ss))
        for p in range(n_pair):
            sl = slice(p * LANES, (p + 1) * LANES)
            ch = {k: stack(x[:, sl]) for k, x in full.items()}
            ch.update(d=d, p=p, sl=sl, strict=strict, incl=incl, g_tot=g_tot[:, sl])
            chains.append(ch)

    c2 = 2 * c
    mask2 = lambda m, x: jnp.where(jnp.concatenate([m, m], axis=1), x, 0.0).astype(BF16)
    for ch in chains:
        kb = jnp.concatenate([ch["b"], ch["k"]], axis=0)
        sc = mask2(ch["strict"], _dot_nt(ch["a"], kb))
        ch["low"], ch["a_k"] = sc[:, :c2], sc[:, c2:]
        ch["r_bk"] = mask2(ch["incl"], _dot_nt(ch["r"], kb))
    for ch in chains:
        ch["st"] = st_ref[ch["d"], ch["p"]]
        ch["stb"] = ch["st"].astype(BF16)
        ch["u"] = _dot_nt(ch["a"], ch["stb"]) + _dot(ch["a_k"], ch["v"])
    n_sq = int(math.log2(c)) - 1
    for it in range(n_sq + 1):
        for ch in chains:
            if it < n_sq:
                both = _dot(ch["low"], jnp.concatenate([ch["low"], ch["u"].astype(BF16)], axis=1))
                ch["low"], ch["u"] = both[:, :c2].astype(BF16), ch["u"] + both[:, c2:]
            else:
                ch["u"] = ch["u"] + _dot(ch["low"], ch["u"].astype(BF16))
    for ch in chains:
        ub = ch["u"].astype(BF16)
        uv = jnp.concatenate([ub, ch["v"]], axis=0)
        y = _dot_nt(ch["r"], ch["stb"]) + _dot(ch["r_bk"], uv)
        y_refs[ch["d"]][:, ch["sl"]] = y[:c] + y[c:]
        st_ref[ch["d"], ch["p"]] = ch["st"] * ch["g_tot"] + _dot_tn(uv, jnp.concatenate([ch["bg"], ch["kg"]], axis=0))


def _rwkv_scan(pre, n_b, n_l, n_c):
    rows = pre[0].shape[0]
    blk = lambda rev: pl.BlockSpec((RW_CHUNK, W_GROUP),
                                   lambda b, i, rev=rev: (_seq_block(rev, b, i, n_c, n_l, n_b), 0))
    args, specs = [], []
    for d in range(2):
        args += [pre[0], pre[1], pre[2], pre[3 + 3 * d], pre[4 + 3 * d], pre[5 + 3 * d]]
        specs += [blk(d == 1)] * 6
    return pl.pallas_call(
        _rwkv_scan_body,
        grid=(n_b, n_c + n_l),
        in_specs=specs,
        out_specs=[blk(False), blk(True)],
        out_shape=[jax.ShapeDtypeStruct((rows, W_GROUP), F32)] * 2,
        scratch_shapes=[pltpu.VMEM((2, RWKV_HEADS // 2, LANES, LANES), F32)],
        compiler_params=_cparams(("arbitrary", "arbitrary")),
        name="rwkv_scan",
    )(*args)


def _rwkv_finish_body(yf_ref, yb_ref, bon_ref, g_ref, lg_ref, lb_ref, e_ref, o_ref):
    y = yf_ref[...] + yb_ref[...]
    dlt = y - _segsum(y, e_ref) * (1.0 / RWKV_N)
    var = _segsum(dlt * dlt, e_ref) * (1.0 / RWKV_N)
    yn = dlt * lax.rsqrt(var + GN_EPS) * lg_ref[...] + lb_ref[...]
    o_ref[...] = ((yn + bon_ref[...]) * g_ref[...]).astype(BF16)


def _rwkv_finish(yf, yb, bonus, gate, lnx_g, lnx_b, e_seg):
    rows = yf.shape[0]
    blk = pl.BlockSpec((ROW_TILE, W_GROUP), lambda i: (i, 0))
    vec = pl.BlockSpec((1, W_GROUP), lambda i: (0, 0))
    return pl.pallas_call(
        _rwkv_finish_body,
        grid=(rows // ROW_TILE,),
        in_specs=[blk, blk, blk, blk, vec, vec, pl.BlockSpec((W_GROUP, W_GROUP), lambda i: (0, 0))],
        out_specs=blk,
        out_shape=jax.ShapeDtypeStruct((rows, W_GROUP), BF16),
        compiler_params=_cparams(("arbitrary",)),
        name="rwkv_finish",
    )(yf, yb, bonus, gate, lnx_g.reshape(1, W_GROUP), lnx_b.reshape(1, W_GROUP), e_seg)


def _outproj_body(x_ref, ya_ref, yb_ref, yc_ref, yd_ref, w_ref, mod_ref, g2_ref, wrh_ref, wrl_ref, br_ref,
                  xo_ref, h2_ref, lg_ref):
    mix = _dot(ya_ref[...], w_ref[0:W_GROUP, :])
    mix = mix + _dot(yb_ref[...], w_ref[W_GROUP:2 * W_GROUP, :])
    mix = mix + _dot(yc_ref[...], w_ref[2 * W_GROUP:3 * W_GROUP, :])
    mix = mix + _dot(yd_ref[...], w_ref[3 * W_GROUP:, :])
    x = x_ref[...] + mod_ref[0, 2:3, :] * mix
    xo_ref[...] = x
    ms = jnp.mean(x * x, axis=-1, keepdims=True)
    h2 = x * lax.rsqrt(ms + NORM_EPS) * g2_ref[...]
    h2 = h2 * (1.0 + mod_ref[0, 4:5, :]) + mod_ref[0, 3:4, :]
    h2_ref[...] = h2.astype(BF16)
    lg_ref[...] = _dot3(h2, wrh_ref[...], wrl_ref[...]) + br_ref[...]


def _outproj(x, ys, w_out, mod3, g2, wr_hi, wr_lo, br, n_b, n_seq):
    rows = x.shape[0]
    big = pl.BlockSpec((OUT_TILE, D_MODEL), lambda i: (i, 0))
    yb = pl.BlockSpec((OUT_TILE, W_GROUP), lambda i: (i, 0))
    cst = lambda shape: pl.BlockSpec(shape, lambda i: (0, 0))
    return pl.pallas_call(
        _outproj_body,
        grid=(rows // OUT_TILE,),
        in_specs=[big, yb, yb, yb, yb, cst((D_MODEL, D_MODEL)),
                  pl.BlockSpec((1, 6, D_MODEL), lambda i: (_mod_row(i, OUT_TILE, n_b, n_seq), 0, 0)),
                  cst((1, D_MODEL)), cst((D_MODEL, LANES)), cst((D_MODEL, LANES)), cst((1, LANES))],
        out_specs=[big, big, pl.BlockSpec((OUT_TILE, LANES), lambda i: (i, 0))],
        out_shape=[jax.ShapeDtypeStruct((rows, D_MODEL), F32),
                   jax.ShapeDtypeStruct((rows, D_MODEL), BF16),
                   jax.ShapeDtypeStruct((rows, LANES), F32)],
        compiler_params=_cparams(("arbitrary",)),
        name="outproj_norm_router",
    )(x, ys[0], ys[1], ys[2], ys[3], w_out, mod3, g2.reshape(1, D_MODEL), wr_hi, wr_lo, br)


def _expert_body(be_ref, nv_ref, new_ref, x_ref, w1_ref, w3_ref, w2_ref, o_ref, w1b_ref, w3b_ref, w2b_ref):
    i = pl.program_id(0)

    @pl.when(new_ref[i] == 1)
    def _():
        w1b_ref[...] = w1_ref[0, 0].astype(BF16)
        w3b_ref[...] = w3_ref[0, 0].astype(BF16)
        w2b_ref[...] = w2_ref[0, 0].astype(BF16)

    @pl.when(i < nv_ref[0])
    def _():
        x = x_ref[...]
        h1 = _dot(x, w1b_ref[...])
        hid = (h1 * jax.nn.sigmoid(h1)) * _dot(x, w3b_ref[...])
        o_ref[...] = _dot(hid.astype(BF16), w2b_ref[...]).astype(o_ref.dtype)

    @pl.when(i >= nv_ref[0])
    def _():
        o_ref[...] = jnp.zeros_like(o_ref)


def _experts(xb, block_e, n_valid, layer, w1, w3, w2):
    n_blocks = xb.shape[0] // MOE_BM
    is_new = jnp.concatenate([jnp.ones((1,), jnp.int32), (block_e[1:] != block_e[:-1]).astype(jnp.int32)])
    wspec = lambda shape: pl.BlockSpec((1, 1) + shape, lambda i, be, nv, nw: (layer, be[i], 0, 0),
                                       pipeline_mode=pl.Buffered(1))
    return pl.pallas_call(
        _expert_body,
        grid_spec=pltpu.PrefetchScalarGridSpec(
            num_scalar_prefetch=3,
            grid=(n_blocks,),
            in_specs=[pl.BlockSpec((MOE_BM, D_MODEL), lambda i, be, nv, nw: (i, 0)),
                      wspec((D_MODEL, D_EXPERT)), wspec((D_MODEL, D_EXPERT)), wspec((D_EXPERT, D_MODEL))],
            out_specs=pl.BlockSpec((MOE_BM, D_MODEL), lambda i, be, nv, nw: (i, 0)),
            scratch_shapes=[pltpu.VMEM((D_MODEL, D_EXPERT), BF16), pltpu.VMEM((D_MODEL, D_EXPERT), BF16),
                            pltpu.VMEM((D_EXPERT, D_MODEL), BF16)]),
        out_shape=jax.ShapeDtypeStruct((xb.shape[0], D_MODEL), BF16),
        compiler_params=_cparams(("arbitrary",)),
        name="moe_experts",
    )(block_e, n_valid, is_new, xb, w1, w3, w2)


def _combine_body(x_ref, y0_ref, y1_ref, rt_ref, mod_ref, o_ref):
    rt = rt_ref[...]
    mo = rt[:, RT_W:RT_W + 1] * y0_ref[...].astype(F32) + rt[:, RT_W + 1:RT_W + 2] * y1_ref[...].astype(F32)
    o_ref[...] = x_ref[...] + mod_ref[0, 5:6, :] * mo


def _combine(x, y0, y1, route, mod3, n_b, n_seq, n_out_rows):
    big = pl.BlockSpec((ROW_TILE, D_MODEL), lambda i: (i, 0))
    return pl.pallas_call(
        _combine_body,
        grid=(n_out_rows // ROW_TILE,),
        in_specs=[big, big, big, pl.BlockSpec((ROW_TILE, LANES), lambda i: (i, 0)),
                  pl.BlockSpec((1, 6, D_MODEL), lambda i: (_mod_row(i, ROW_TILE, n_b, n_seq), 0, 0))],
        out_specs=big,
        out_shape=jax.ShapeDtypeStruct((n_out_rows, D_MODEL), F32),
        compiler_params=_cparams(("arbitrary",)),
        name="moe_combine",
    )(x, y0, y1, route, mod3)


RT_E, RT_RANK, RT_W = 0, 2, 4
LOGIT_E0 = N_EGROUPS


def _route_body(lg_ref, rt_ref, cnt_ref, run_ref):
    tm = lg_ref.shape[0]

    @pl.when(pl.program_id(0) == 0)
    def _():
        run_ref[...] = jnp.zeros_like(run_ref)

    x = lg_ref[...]
    lane = lax.broadcasted_iota(jnp.int32, (tm, LANES), 1).astype(F32)
    neg = jnp.float32(-jnp.inf)

    def first_max(v):
        m = jnp.max(v, axis=-1, keepdims=True)
        return m, jnp.min(jnp.where(v == m, lane, float(LANES)), axis=-1, keepdims=True)

    g = jnp.where(lane < N_EGROUPS, x, neg)
    mg, gsel = first_max(g)
    gate_g = 1.0 / jnp.sum(jnp.exp(g - mg), axis=-1, keepdims=True)
    lo = LOGIT_E0 + EXP_PER_GROUP * gsel
    e = jnp.where((lane >= lo) & (lane < lo + EXP_PER_GROUP), x, neg)
    m1, i1 = first_max(e)
    m2, i2 = first_max(jnp.where(lane == i1, neg, e))
    e2 = jnp.exp(m2 - m1)
    w1 = gate_g / (1.0 + e2)
    w2 = gate_g * e2 / (1.0 + e2)
    oh1 = lane == i1
    oh2 = lane == i2
    oh = (oh1 | oh2).astype(BF16)
    rr = lax.broadcasted_iota(jnp.int32, (tm, tm), 0)
    cc = lax.broadcasted_iota(jnp.int32, (tm, tm), 1)
    before = _dot((rr > cc).astype(BF16), oh) + run_ref[0:1, :]
    r1 = jnp.sum(jnp.where(oh1, before, 0.0), axis=-1, keepdims=True)
    r2 = jnp.sum(jnp.where(oh2, before, 0.0), axis=-1, keepdims=True)
    run_ref[...] = run_ref[...] + jnp.sum(oh.astype(F32), axis=0, keepdims=True)
    cnt_ref[...] = run_ref[...]
    rec = jnp.where(lane == RT_E, i1 - LOGIT_E0, 0.0)
    rec = jnp.where(lane == RT_E + 1, i2 - LOGIT_E0, rec)
    rec = jnp.where(lane == RT_RANK, r1, rec)
    rec = jnp.where(lane == RT_RANK + 1, r2, rec)
    rec = jnp.where(lane == RT_W, w1, rec)
    rt_ref[...] = jnp.where(lane == RT_W + 1, w2, rec)


def _route(logits):
    rows = logits.shape[0]
    blk = pl.BlockSpec((ROW_TILE, LANES), lambda i: (i, 0))
    return pl.pallas_call(
        _route_body,
        grid=(rows // ROW_TILE,),
        in_specs=[blk],
        out_specs=[blk, pl.BlockSpec((SUBLANES, LANES), lambda i: (0, 0))],
        out_shape=[jax.ShapeDtypeStruct((rows, LANES), F32), jax.ShapeDtypeStruct((SUBLANES, LANES), F32)],
        scratch_shapes=[pltpu.VMEM((SUBLANES, LANES), F32)],
        compiler_params=_cparams(("arbitrary",)),
        name="moe_route",
    )(logits)


def _moe(x, h2, logits, mod3, layer, w1, w3, w2, n_b, n_seq, n_out_rows):
    rows = x.shape[0]
    route, cnt = _route(logits)
    counts = cnt[0, LOGIT_E0:LOGIT_E0 + N_EXPERTS].astype(jnp.int32)
    pcounts = ((counts + MOE_BM - 1) // MOE_BM) * MOE_BM
    pend = jnp.cumsum(pcounts)
    pstart = pend - pcounts
    eid = route[:, RT_E:RT_E + 2].astype(jnp.int32)
    dest = pstart[eid] + route[:, RT_RANK:RT_RANK + 2].astype(jnp.int32)
    n_blocks = -(-(2 * rows) // MOE_BM) + N_EXPERTS
    tok = jnp.broadcast_to(jnp.arange(rows, dtype=jnp.int32)[:, None], (rows, 2))
    slot_tok = (jnp.arange(n_blocks * MOE_BM, dtype=jnp.int32) % rows).at[dest.reshape(-1)].set(tok.reshape(-1))
    take_rows = lambda a, idx: a.at[idx].get(mode="promise_in_bounds")
    xb = take_rows(h2, slot_tok)
    blk_start = jnp.arange(n_blocks) * MOE_BM
    block_e = jnp.sum(blk_start[:, None] >= pend[None, :], axis=1)
    block_e = jnp.minimum(block_e, N_EXPERTS - 1).astype(jnp.int32)
    n_valid = (pend[-1] // MOE_BM).astype(jnp.int32).reshape(1)
    yb = _experts(xb, block_e, n_valid, layer, w1, w3, w2)
    y0 = take_rows(yb, dest[:n_out_rows, 0])
    y1 = take_rows(yb, dest[:n_out_rows, 1])
    return _combine(x, y0, y1, route, mod3, n_b, n_seq, n_out_rows)


def _pad_rows(w, lo, n):
    pad = [(0, 0)] * (w.ndim - 2) + [(lo, n - lo - w.shape[-2]), (0, 0)]
    return jnp.pad(w.astype(F32), pad).astype(BF16)


def kernel(x, c, ctx, c_ctx, w_ada, b_ada, g_norm1, g_norm2, w_in, w_out, s5_lam_re, s5_lam_im, s5_log_step, s5_b_re, s5_b_im, s5_c_re, s5_c_im, s5_d, s5_w_glu, s5_b_glu, lru_conv_w, lru_conv_b, lru_lam, lru_wa, lru_ba, lru_wx, lru_bx, diff_gq, diff_gk, diff_lq1, diff_lk1, diff_lq2, diff_lk2, diff_subln, rw_mu, rw_w0, rw_w2, rw_a0, rw_a2, rw_g2, rw_kk, rw_ka, rw_rk, rw_lnx_g, rw_lnx_b, moe_w_rg, moe_b_rg, moe_w_re, moe_b_re, moe_w1, moe_w3, moe_w2):
    n_b, n_seq, _ = x.shape
    n_ctx = ctx.shape[1]
    depth = w_ada.shape[0]
    assert n_seq % SEQ_TILE == 0 and n_ctx % SEQ_TILE == 0 and n_seq % n_ctx == 0
    assert (n_b * n_ctx) % ROW_TILE == 0 and n_seq % ROW_TILE == 0 and n_seq % ATT_TK == 0
    xs = jnp.concatenate([x.reshape(n_b * n_seq, D_MODEL), ctx.reshape(n_b * n_ctx, D_MODEL)], axis=0)
    rows = xs.shape[0]
    cvec = jnp.zeros((SUBLANES, D_MODEL), F32).at[0].set(c_ctx).at[1:1 + n_b].set(c)
    rope = _rope_tables(n_seq)
    seg = jnp.arange(W_GROUP) // RWKV_N
    e_seg = (seg[:, None] == seg[None, :]).astype(BF16)
    nl_s, nc_s = n_seq // SEQ_TILE, n_ctx // SEQ_TILE
    nl_r, nc_r = n_seq // RW_CHUNK, n_ctx // RW_CHUNK

    for l in range(depth):
        lam_init = 0.8 - 0.6 * math.exp(-0.3 * l)
        mod = _modulation(cvec, l, w_ada, b_ada[l])
        mod3 = mod[:1 + n_b].reshape(1 + n_b, 6, D_MODEL)
        w_in_b = jnp.pad(w_in[l].astype(BF16), ((0, 0), (0, D_IN_PAD - D_IN)))
        za = _inproj(xs, mod3, g_norm1[l], w_in_b, n_b, n_seq)
        s5p = _s5_prepare(s5_lam_re[l], s5_lam_im[l], s5_log_step[l], s5_b_re[l], s5_b_im[l],
                          s5_c_re[l], s5_c_im[l])
        y5 = [_s5_scan(za, s5p, d, n_b, nl_s, nc_s) for d in range(2)]
        ya = _s5_finish(y5[0], y5[1], za, s5_d[l], s5_w_glu[l], s5_b_glu[l])
        lrp = dict(cw=lru_conv_w[l], cb=lru_conv_b[l].reshape(1, W_GROUP),
                   wa=_blockdiag(lru_wa[l]).astype(BF16), wx=_blockdiag(lru_wx[l]).astype(BF16),
                   ba=lru_ba[l].reshape(2, 1, W_GROUP), bx=lru_bx[l].reshape(2, 1, W_GROUP),
                   sp=jax.nn.softplus(-lru_lam[l]).reshape(2, 1, W_GROUP))
        hl = [_lru_scan(za, lrp, d, n_b, nl_s, nc_s) for d in range(2)]
        yb_ = _lru_finish(hl[0], hl[1], za)
        lam = jnp.exp(jnp.sum(diff_lq1[l] * diff_lk1[l])) - jnp.exp(jnp.sum(diff_lq2[l] * diff_lk2[l])) + lam_init
        lam_vec = jnp.full((1, LANES), lam, F32)
        qn, kn, vn = _qkv_prep(za, rope, diff_gq[l], diff_gk[l], e_seg, n_b, n_seq)
        o_lat, o_ctx = _attention(qn, kn, vn, lam_vec, diff_subln[l], lam_init, n_b, n_seq, n_ctx)
        yc = jnp.concatenate([o_lat, o_ctx], axis=0)
        lo_w, lo_a = RANK_G, RANK_G + RANK_W
        rwp = dict(mu=rw_mu[l].reshape(1, RWKV_COLS),
                   w2=_pad_rows(rw_w2[l], lo_w, LANES), a2=_pad_rows(rw_a2[l], lo_a, LANES),
                   g2=_pad_rows(rw_g2[l], 0, LANES), w0=rw_w0[l], a0=rw_a0[l],
                   kk=rw_kk[l].reshape(1, W_GROUP), ka=rw_ka[l].reshape(1, W_GROUP),
                   rk=rw_rk[l].reshape(1, W_GROUP))
        pre = _rwkv_prep(za, rwp, e_seg, n_b, nl_s, nc_s)
        yr = _rwkv_scan(pre, n_b, nl_r, nc_r)
        yd = _rwkv_finish(yr[0], yr[1], pre[9], pre[10], rw_lnx_g[l], rw_lnx_b[l], e_seg)
        wr = jnp.zeros((D_MODEL, LANES), F32).at[:, :N_EGROUPS].set(moe_w_rg[l])
        wr = wr.at[:, N_EGROUPS:N_EGROUPS + N_EXPERTS].set(moe_w_re[l])
        wr_hi, wr_lo = _split2(wr)
        br = jnp.zeros((1, LANES), F32).at[0, :N_EGROUPS].set(moe_b_rg[l])
        br = br.at[0, N_EGROUPS:N_EGROUPS + N_EXPERTS].set(moe_b_re[l])
        xs, h2, logits = _outproj(xs, (ya, yb_, yc, yd), w_out[l].astype(BF16), mod3, g_norm2[l],
                                  wr_hi, wr_lo, br, n_b, n_seq)
        n_out_rows = rows if l < depth - 1 else n_b * n_seq
        xs = _moe(xs, h2, logits, mod3, l, moe_w1, moe_w3, moe_w2, n_b, n_seq, n_out_rows)
    return xs.reshape(n_b, n_seq, D_MODEL)
```

```python
import functools
import math

import jax
import jax.numpy as jnp
from jax import lax
from jax.experimental import pallas as pl
from jax.experimental.pallas import tpu as pltpu

F32 = jnp.float32
BF16 = jnp.bfloat16

D_MODEL = 2048
W_GROUP = 512
S5_CH, S5_GROUPS, S5_STATE = 16, 32, 64
S5_NS = S5_GROUPS * S5_STATE
LRU_C = 8.0
DIFF_HEADS, DIFF_D = 4, 64
ROPE_AX = 16
ROPE_BASE = 10000.0
RWKV_N, RWKV_HEADS = 64, 8
RANK_G, RANK_W, RANK_A = 64, 32, 32
RWKV_COLS = 3 * W_GROUP + RANK_G + RANK_W + RANK_A
N_EGROUPS, EXP_PER_GROUP, N_EXPERTS = 4, 8, 32
D_EXPERT = 1024
NORM_EPS = 1e-6
GN_EPS = 64e-5
GRID_W = 64

SUBLANES = 8
LANES = 128
ROW_TILE = 512
SEQ_TILE = 256
RW_CHUNK = 64
ATT_TQ = 256
ATT_UNROLL = 32
ATT_TK = 512
ATT_RB = 128
MOE_BM = 256
OUT_TILE = 256
IN_TN = 1024
D_IN = 6 * W_GROUP + RWKV_COLS
D_IN_PAD = -(-D_IN // IN_TN) * IN_TN
COL_R, COL_K, COL_V = 6, 7, 8
COL_LOW = (9 * W_GROUP) // LANES
VMEM_LIMIT = 56 * 1024 * 1024


def _cparams(sem):
    return pltpu.CompilerParams(dimension_semantics=sem, vmem_limit_bytes=VMEM_LIMIT)


def _split2(x):
    hi = x.astype(BF16)
    lo = (x - hi.astype(F32)).astype(BF16)
    return hi, lo


def _dot(a, b):
    return jnp.dot(a, b, preferred_element_type=F32)


def _dot_nt(a, b):
    return lax.dot_general(a, b, (((1,), (1,)), ((), ())), preferred_element_type=F32)


def _dot_tn(a, b):
    return lax.dot_general(a, b, (((0,), (0,)), ((), ())), preferred_element_type=F32)


def _dot3(a, b_hi, b_lo):
    a_hi, a_lo = _split2(a)
    return _dot(a_hi, b_hi) + _dot(a_hi, b_lo) + _dot(a_lo, b_hi)


def _segsum(x, e_ref):
    hi, lo = _split2(x)
    e = e_ref[...]
    return _dot(hi, e) + _dot(lo, e)


def _seq_block(rev, b, i, n_c, n_l, n_b):
    if rev:
        ctx = n_b * n_l + b * n_c + (n_c - 1 - i)
        lat = b * n_l + (n_l - 1 - (i - n_c))
    else:
        ctx = n_b * n_l + b * n_c + i
        lat = b * n_l + (i - n_c)
    return jnp.where(i < n_c, ctx, lat)


def _seg_pos(t, n_b, n_l, n_c):
    is_lat = t < n_b * n_l
    p = jnp.where(is_lat, t % n_l, (t - n_b * n_l) % n_c)
    n = jnp.where(is_lat, n_l, n_c)
    return p, n


def _mod_body(c_ref, w_ref, b_ref, o_ref):
    c = c_ref[...]
    s = c * jax.nn.sigmoid(c)
    w_hi, w_lo = _split2(w_ref[0])
    o_ref[...] = _dot3(s, w_hi, w_lo) + b_ref[...]


def _modulation(cvec, layer, w_ada, b_ada):
    d6 = w_ada.shape[2]
    tn = 1536
    return pl.pallas_call(
        _mod_body,
        grid=(d6 // tn,),
        in_specs=[pl.BlockSpec((SUBLANES, D_MODEL), lambda j: (0, 0)),
                  pl.BlockSpec((1, D_MODEL, tn), lambda j: (layer, 0, j)),
                  pl.BlockSpec((1, tn), lambda j: (0, j))],
        out_specs=pl.BlockSpec((SUBLANES, tn), lambda j: (0, j)),
        out_shape=jax.ShapeDtypeStruct((SUBLANES, d6), F32),
        compiler_params=_cparams(("arbitrary",)),
        name="adaln_mod",
    )(cvec, w_ada, b_ada.reshape(1, d6))


def _mod_row(i, tile, n_b, n_seq):
    return jnp.where(i < n_b * n_seq // tile, 1 + i // (n_seq // tile), 0)


def _inproj_body(x_ref, mod_ref, g_ref, w_ref, z_ref, h_ref):
    @pl.when(pl.program_id(1) == 0)
    def _():
        x = x_ref[...]
        ms = jnp.mean(x * x, axis=-1, keepdims=True)
        xn = x * lax.rsqrt(ms + NORM_EPS) * g_ref[...]
        h_ref[...] = (xn * (1.0 + mod_ref[0, 1:2, :]) + mod_ref[0, 0:1, :]).astype(BF16)

    z_ref[...] = _dot(h_ref[...], w_ref[...]).astype(z_ref.dtype)


def _inproj(x, mod3, g1, w, n_b, n_seq):
    rows = x.shape[0]
    tn = IN_TN
    return pl.pallas_call(
        _inproj_body,
        grid=(rows // ROW_TILE, w.shape[1] // tn),
        in_specs=[pl.BlockSpec((ROW_TILE, D_MODEL), lambda i, j: (i, 0)),
                  pl.BlockSpec((1, 6, D_MODEL), lambda i, j: (_mod_row(i, ROW_TILE, n_b, n_seq), 0, 0)),
                  pl.BlockSpec((1, D_MODEL), lambda i, j: (0, 0)),
                  pl.BlockSpec((D_MODEL, tn), lambda i, j: (0, j))],
        out_specs=pl.BlockSpec((ROW_TILE, tn), lambda i, j: (i, j)),
        out_shape=jax.ShapeDtypeStruct((rows, w.shape[1]), BF16),
        scratch_shapes=[pltpu.VMEM((ROW_TILE, D_MODEL), BF16)],
        compiler_params=_cparams(("arbitrary", "arbitrary")),
        name="norm_inproj",
    )(x, mod3, g1.reshape(1, D_MODEL), w)


S5_SEG = SEQ_TILE // SUBLANES
S5_HALF = 2


def _s5_body(rev, u_ref, bre_ref, bim_ref, dec_ref, dseg_ref, t3_ref, pw_ref, cre_ref, cim_ref,
             y_ref, sr_ref, si_ref, car_ref, loc_ref):
    n_k = S5_SEG
    hw, hs = W_GROUP // S5_HALF, S5_NS // S5_HALF

    @pl.when(pl.program_id(1) == 0)
    def _():
        car_ref[...] = jnp.zeros_like(car_ref)

    ri = lax.broadcasted_iota(jnp.int32, (SEQ_TILE, SEQ_TILE), 0)
    ci = lax.broadcasted_iota(jnp.int32, (SEQ_TILE, SEQ_TILE), 1)
    lg8, lgk = int(math.log2(SUBLANES)), int(math.log2(n_k))
    perm = (ci == ((ri & (SUBLANES - 1)) << lgk) + (ri >> lg8)).astype(BF16)
    up = _dot(perm, u_ref[...].astype(BF16)).astype(BF16)
    for h in range(S5_HALF):
        uh = up[:, h * hw:(h + 1) * hw]
        sr_ref[:, :, h * hs:(h + 1) * hs] = _dot(uh, bre_ref[0, h]).reshape(n_k, SUBLANES, hs)
        si_ref[:, :, h * hs:(h + 1) * hs] = _dot(uh, bim_ref[0, h]).reshape(n_k, SUBLANES, hs)

    loc_ref[...] = jnp.zeros_like(loc_ref)
    ar, ai = dec_ref[0, 0], dec_ref[0, 1]

    def step(k, _):
        kk = (n_k - 1 - k) if rev else k
        cr, ci = loc_ref[0], loc_ref[1]
        nr = ar * cr - ai * ci + sr_ref[kk]
        ni = ar * ci + ai * cr + si_ref[kk]
        sr_ref[kk] = nr
        si_ref[kk] = ni
        loc_ref[0] = nr
        loc_ref[1] = ni
        return 0

    lax.fori_loop(0, n_k, step, 0)

    sub = lax.broadcasted_iota(jnp.int32, (SUBLANES, S5_NS), 0)
    first = (SUBLANES - 1) if rev else 0
    sh1 = (SUBLANES - 1) if rev else 1
    lr, li = loc_ref[0], loc_ref[1]
    er = jnp.where(sub == first, car_ref[0], pltpu.roll(lr, sh1, axis=0))
    ei = jnp.where(sub == first, car_ref[1], pltpu.roll(li, sh1, axis=0))
    for lvl, s in enumerate((1, 2, 4)):
        sh = (SUBLANES - s) if rev else s
        pr, pi = pltpu.roll(er, sh, axis=0), pltpu.roll(ei, sh, axis=0)
        tr, ti = t3_ref[0, 0, lvl], t3_ref[0, 1, lvl]
        er, ei = er + tr * pr - ti * pi, ei + tr * pi + ti * pr
    dr, di = dseg_ref[0, 0], dseg_ref[0, 1]
    last = 0 if rev else SUBLANES - 1
    fr = lr + dr * er - di * ei
    fi = li + dr * ei + di * er
    car_ref[0] = jnp.broadcast_to(fr[last:last + 1, :], (SUBLANES, S5_NS))
    car_ref[1] = jnp.broadcast_to(fi[last:last + 1, :], (SUBLANES, S5_NS))

    pr, pi = pw_ref[0, 0], pw_ref[0, 1]
    s_r = (sr_ref[...] + pr * er[None] - pi * ei[None]).reshape(SEQ_TILE, S5_NS).astype(BF16)
    s_i = (si_ref[...] + pr * ei[None] + pi * er[None]).reshape(SEQ_TILE, S5_NS).astype(BF16)
    ys = []
    for h in range(S5_HALF):
        cols = slice(h * hs, (h + 1) * hs)
        ys.append(_dot(s_r[:, cols], cre_ref[0, h]) - _dot(s_i[:, cols], cim_ref[0, h]))
    y_hi, y_lo = _split2(jnp.concatenate(ys, axis=1))
    inv = (ci == ((ri & (n_k - 1)) << lg8) + (ri >> lgk)).astype(BF16)
    y_ref[...] = _dot(inv, y_hi) + _dot(inv, y_lo)


def _s5_scan(za, p, d, n_b, n_l, n_c):
    rev = d == 1
    rows = za.shape[0]
    seq = lambda b, i: (_seq_block(rev, b, i, n_c, n_l, n_b), 0)
    cst = lambda nd: (lambda b, i: (d,) + (0,) * (nd - 1))
    hw, hs = W_GROUP // S5_HALF, S5_NS // S5_HALF
    state = pltpu.VMEM((S5_SEG, SUBLANES, S5_NS), F32)
    pair = pltpu.VMEM((2, SUBLANES, S5_NS), F32)
    return pl.pallas_call(
        functools.partial(_s5_body, rev),
        grid=(n_b, n_c + n_l),
        in_specs=[pl.BlockSpec((SEQ_TILE, W_GROUP), seq),
                  pl.BlockSpec((1, S5_HALF, hw, hs), cst(4)),
                  pl.BlockSpec((1, S5_HALF, hw, hs), cst(4)),
                  pl.BlockSpec((1, 2, SUBLANES, S5_NS), cst(4)),
                  pl.BlockSpec((1, 2, SUBLANES, S5_NS), cst(4)),
                  pl.BlockSpec((1, 2, 3, SUBLANES, S5_NS), cst(5)),
                  pl.BlockSpec((1, 2, S5_SEG, SUBLANES, S5_NS), cst(5)),
                  pl.BlockSpec((1, S5_HALF, hs, hw), cst(4)),
                  pl.BlockSpec((1, S5_HALF, hs, hw), cst(4))],
        out_specs=pl.BlockSpec((SEQ_TILE, W_GROUP), seq),
        out_shape=jax.ShapeDtypeStruct((rows, W_GROUP), F32),
        scratch_shapes=[state, state, pair, pair],
        compiler_params=_cparams(("arbitrary", "arbitrary")),
        name="s5_scan_rev" if rev else "s5_scan_fwd",
    )(za, p["bre"], p["bim"], p["dec"], p["dseg"], p["t3"], p["pw"], p["cre"], p["cim"])


def _s5_prepare(lam_re, lam_im, log_step, b_re, b_im, c_re, c_im):
    lr, li = lam_re.astype(F32), lam_im.astype(F32)
    dt = jnp.exp(log_step.astype(F32))[..., None]
    mag = jnp.exp(lr * dt)
    ar, ai = mag * jnp.cos(li * dt), mag * jnp.sin(li * dt)
    den = lr * lr + li * li
    cr = ((ar - 1.0) * lr + ai * li) / den
    ci = (ai * lr - (ar - 1.0) * li) / den
    bbr = cr[..., None] * b_re - ci[..., None] * b_im
    bbi = cr[..., None] * b_im + ci[..., None] * b_re
    gh = S5_GROUPS // S5_HALF
    eye = jnp.eye(gh, dtype=F32)

    def bd(t):
        t = t.reshape(2, S5_HALF, gh, S5_STATE, S5_CH)
        return jnp.einsum('dxgph,gk->dxghkp', t, eye).reshape(2, S5_HALF, gh * S5_CH, gh * S5_STATE).astype(BF16)

    def cd(t):
        t = t.astype(F32).reshape(2, S5_HALF, gh, S5_CH, S5_STATE)
        return jnp.einsum('dxghp,gk->dxgpkh', t, eye).reshape(2, S5_HALF, gh * S5_STATE, gh * S5_CH).astype(BF16)

    cmul = lambda x, y: (x[0] * y[0] - x[1] * y[1], x[0] * y[1] + x[1] * y[0])
    dec = (ar.reshape(2, S5_NS), ai.reshape(2, S5_NS))
    pw = [dec]
    for _ in range(S5_SEG - 1):
        pw.append(cmul(pw[-1], dec))
    seg = [pw[-1]]
    for _ in range(2):
        seg.append(cmul(seg[-1], seg[-1]))
    wide = lambda t: jnp.broadcast_to(t[:, None, :], (2, SUBLANES, S5_NS))
    pair = lambda z: jnp.stack([wide(z[0]), wide(z[1])], axis=1)
    k = jnp.arange(SUBLANES)
    t3 = []
    for j, s in enumerate((1, 2, 4)):
        m = jnp.stack([k >= s, k + s <= SUBLANES - 1])
        t3.append(jnp.where(m[:, None, :, None], pair(seg[j]), 0.0))
    t3 = jnp.stack(t3, axis=2)
    pw_f = jnp.stack([jnp.stack([wide(p[0])[0], wide(p[1])[0]]) for p in pw], axis=1)
    pw_b = jnp.stack([jnp.stack([wide(p[0])[1], wide(p[1])[1]]) for p in reversed(pw)], axis=1)
    return dict(bre=bd(bbr), bim=bd(bbi), cre=cd(c_re), cim=cd(c_im), dec=pair(dec), dseg=pair(seg[0]),
                t3=t3, pw=jnp.stack([pw_f, pw_b]))


def _s5_finish_body(yf_ref, yb_ref, u_ref, d_ref, w_ref, b_ref, o_ref):
    y = jax.nn.gelu(yf_ref[...] + yb_ref[...] + d_ref[...] * u_ref[...].astype(F32))
    gate = jax.nn.sigmoid(_dot(y.astype(BF16), w_ref[...]) + b_ref[...])
    o_ref[...] = (y * gate).astype(BF16)


def _s5_finish(yf, yb, za, d_skip, w_glu, b_glu):
    rows = yf.shape[0]
    blk = pl.BlockSpec((ROW_TILE, W_GROUP), lambda i: (i, 0))
    vec = pl.BlockSpec((1, W_GROUP), lambda i: (0, 0))
    return pl.pallas_call(
        _s5_finish_body,
        grid=(rows // ROW_TILE,),
        in_specs=[blk, blk, blk, vec, pl.BlockSpec((W_GROUP, W_GROUP), lambda i: (0, 0)), vec],
        out_specs=blk,
        out_shape=jax.ShapeDtypeStruct((rows, W_GROUP), BF16),
        compiler_params=_cparams(("arbitrary",)),
        name="s5_finish",
    )(yf, yb, za, d_skip.reshape(1, W_GROUP), w_glu.astype(BF16), b_glu.reshape(1, W_GROUP))


HALO = 2 * SUBLANES


def _halo_maps(tile_fn, rows, col):
    per = SEQ_TILE // HALO
    last = rows // HALO - 1
    prv = lambda *ids: (jnp.maximum(tile_fn(*ids) * per - 1, 0), col)
    nxt = lambda *ids: (jnp.minimum((tile_fn(*ids) + 1) * per, last), col)
    return prv, nxt


def _lru_body(rev, n_b, n_l, n_c, xp_ref, xc_ref, xn_ref, cw_ref, cb_ref, wa_ref, ba_ref, wx_ref, bx_ref,
              sp_ref, h_ref, buf_ref, a_ref, b_ref, car_ref):
    g_n = SEQ_TILE // SUBLANES
    b = pl.program_id(0)
    i = pl.program_id(1)

    @pl.when(i == 0)
    def _():
        car_ref[...] = jnp.zeros_like(car_ref)

    p, n = _seg_pos(_seq_block(rev, b, i, n_c, n_l, n_b), n_b, n_l, n_c)
    prev_ok = (p > 0).astype(F32)
    next_ok = (p < n - 1).astype(F32)
    xc = xc_ref[...].astype(F32)
    buf_ref[0:HALO, :] = xp_ref[...].astype(F32) * prev_ok
    buf_ref[HALO:HALO + SEQ_TILE, :] = xc
    buf_ref[HALO + SEQ_TILE:, :] = xn_ref[...].astype(F32) * next_ok
    v = cb_ref[...] + cw_ref[2:3, :] * xc
    v = v + cw_ref[0:1, :] * buf_ref[HALO - 2:HALO - 2 + SEQ_TILE, :]
    v = v + cw_ref[1:2, :] * buf_ref[HALO - 1:HALO - 1 + SEQ_TILE, :]
    v = v + cw_ref[3:4, :] * buf_ref[HALO + 1:HALO + 1 + SEQ_TILE, :]
    vb = v.astype(BF16)
    r = jax.nn.sigmoid(_dot(vb, wa_ref[0]) + ba_ref[0])
    ig = jax.nn.sigmoid(_dot(vb, wx_ref[0]) + bx_ref[0])
    a = jnp.exp(-LRU_C * r * sp_ref[0])
    bb = jnp.sqrt(1.0 - a * a) * (ig * v)
    a3 = a.reshape(g_n, SUBLANES, W_GROUP)
    b3 = bb.reshape(g_n, SUBLANES, W_GROUP)
    k = lax.broadcasted_iota(jnp.int32, (g_n, SUBLANES, W_GROUP), 1)
    for s in (1, 2, 4):
        sh = (SUBLANES - s) if rev else s
        m = (k + s <= SUBLANES - 1) if rev else (k >= s)
        a_s = pltpu.roll(a3, sh, axis=1)
        b_s = pltpu.roll(b3, sh, axis=1)
        b3 = jnp.where(m, a3 * b_s + b3, b3)
        a3 = jnp.where(m, a3 * a_s, a3)
    a_ref[...] = a3
    b_ref[...] = b3
    row = 0 if rev else SUBLANES - 1

    def step(g, _):
        gg = (g_n - 1 - g) if rev else g
        hh = b_ref[gg] + a_ref[gg] * car_ref[...]
        b_ref[gg] = hh
        car_ref[...] = jnp.broadcast_to(hh[row:row + 1, :], (SUBLANES, W_GROUP))
        return 0

    lax.fori_loop(0, g_n, step, 0)
    h_ref[...] = b_ref[...].reshape(SEQ_TILE, W_GROUP)


def _lru_scan(za, p, d, n_b, n_l, n_c):
    rev = d == 1
    rows = za.shape[0]
    tile = lambda b, i: _seq_block(rev, b, i, n_c, n_l, n_b)
    cur = lambda b, i: (tile(b, i), 1)
    prv, nxt = _halo_maps(tile, rows, 1)
    out = lambda b, i: (tile(b, i), 0)
    vec = pl.BlockSpec((1, W_GROUP), lambda b, i: (0, 0))
    dvec = pl.BlockSpec((1, 1, W_GROUP), lambda b, i: (d, 0, 0))
    dmat = pl.BlockSpec((1, W_GROUP, W_GROUP), lambda b, i: (d, 0, 0))
    blk = lambda f: pl.BlockSpec((SEQ_TILE, W_GROUP), f)
    halo = lambda f: pl.BlockSpec((HALO, W_GROUP), f)
    g_n = SEQ_TILE // SUBLANES
    return pl.pallas_call(
        functools.partial(_lru_body, rev, n_b, n_l, n_c),
        grid=(n_b, n_c + n_l),
        in_specs=[halo(prv), blk(cur), halo(nxt),
                  pl.BlockSpec((4, W_GROUP), lambda b, i: (0, 0)), vec,
                  dmat, dvec, dmat, dvec, dvec],
        out_specs=blk(out),
        out_shape=jax.ShapeDtypeStruct((rows, W_GROUP), F32),
        scratch_shapes=[pltpu.VMEM((SEQ_TILE + 2 * HALO, W_GROUP), F32),
                        pltpu.VMEM((g_n, SUBLANES, W_GROUP), F32),
                        pltpu.VMEM((g_n, SUBLANES, W_GROUP), F32),
                        pltpu.VMEM((SUBLANES, W_GROUP), F32)],
        compiler_params=_cparams(("arbitrary", "arbitrary")),
        name="lru_scan_rev" if rev else "lru_scan_fwd",
    )(za, za, za, p["cw"], p["cb"], p["wa"], p["ba"], p["wx"], p["bx"], p["sp"])


def _blockdiag(w):
    nb, c = w.shape[1], w.shape[2]
    eye = jnp.eye(nb, dtype=F32)
    return jnp.einsum('dncf,nm->dncmf', w.astype(F32), eye).reshape(2, nb * c, nb * c)


def _lru_finish_body(hf_ref, hb_ref, g_ref, o_ref):
    o_ref[...] = ((hf_ref[...] + hb_ref[...]) * jax.nn.gelu(g_ref[...].astype(F32))).astype(BF16)


def _lru_finish(hf, hb, za):
    rows = hf.shape[0]
    blk = pl.BlockSpec((ROW_TILE, W_GROUP), lambda i: (i, 0))
    return pl.pallas_call(
        _lru_finish_body,
        grid=(rows // ROW_TILE,),
        in_specs=[blk, blk, pl.BlockSpec((ROW_TILE, W_GROUP), lambda i: (i, 2))],
        out_specs=blk,
        out_shape=jax.ShapeDtypeStruct((rows, W_GROUP), BF16),
        compiler_params=_cparams(("arbitrary",)),
        name="lru_finish",
    )(hf, hb, za)


def _qkv_prep_body(n_lat_tiles, q_ref, k_ref, v_ref, cos_ref, s1_ref, s2_ref, gq_ref, gk_ref, e_ref,
                   qo_ref, ko_ref, vo_ref):
    reps = W_GROUP // LANES
    is_ctx = pl.program_id(0) >= n_lat_tiles
    cos = jnp.concatenate([jnp.where(is_ctx, 1.0, cos_ref[...])] * reps, axis=1)
    s1 = jnp.concatenate([jnp.where(is_ctx, 0.0, s1_ref[...])] * reps, axis=1)
    s2 = jnp.concatenate([jnp.where(is_ctx, 0.0, s2_ref[...])] * reps, axis=1)

    def prep(x, g):
        ms = _segsum(x * x, e_ref) * (1.0 / DIFF_D)
        x = x * lax.rsqrt(ms + NORM_EPS) * g
        return x * cos + pltpu.roll(x, ROPE_AX, axis=1) * s1 + pltpu.roll(x, W_GROUP - ROPE_AX, axis=1) * s2

    qo_ref[...] = (prep(q_ref[...].astype(F32), gq_ref[...]) * (DIFF_D ** -0.5)).astype(BF16)
    ko_ref[...] = prep(k_ref[...].astype(F32), gk_ref[...]).astype(BF16)
    vo_ref[...] = v_ref[...].astype(BF16)


def _qkv_prep(za, rope, gq, gk, e_seg, n_b, n_seq):
    rows = za.shape[0]
    per_batch = n_seq // ROW_TILE
    col = lambda c: pl.BlockSpec((ROW_TILE, W_GROUP), lambda i: (i, c))
    tab = pl.BlockSpec((ROW_TILE, LANES), lambda i: (i % per_batch, 0))
    vec = pl.BlockSpec((1, W_GROUP), lambda i: (0, 0))
    out = pl.BlockSpec((ROW_TILE, W_GROUP), lambda i: (i, 0))
    reps = W_GROUP // DIFF_D
    return pl.pallas_call(
        functools.partial(_qkv_prep_body, n_b * per_batch),
        grid=(rows // ROW_TILE,),
        in_specs=[col(3), col(4), col(5), tab, tab, tab, vec, vec,
                  pl.BlockSpec((W_GROUP, W_GROUP), lambda i: (0, 0))],
        out_specs=[out, out, out],
        out_shape=[jax.ShapeDtypeStruct((rows, W_GROUP), BF16)] * 3,
        compiler_params=_cparams(("arbitrary",)),
        name="attn_qkv_prep",
    )(za, za, za, rope[0], rope[1], rope[2], jnp.tile(gq, reps).reshape(1, W_GROUP),
      jnp.tile(gk, reps).reshape(1, W_GROUP), e_seg)


def _attn_body(seg_lens, out_scale, lam_ref, sub_ref, q_ref, *rest):
    o_ref, q2_ref, m_ref, acc_ref, sa_ref, sb_ref = rest[-6:]
    kv = rest[:-6]
    tq = q_ref.shape[0]
    q = q_ref[...]
    lane = lax.broadcasted_iota(jnp.int32, (tq, LANES), 1)
    zero = jnp.zeros_like(q)
    q2 = jnp.concatenate([jnp.where(lane < DIFF_D, q, zero), jnp.where(lane < DIFF_D, zero, q)], axis=0)
    q2_ref[...] = q2
    m_ref[...] = jnp.full_like(m_ref, -jnp.inf)
    acc_ref[...] = jnp.zeros_like(acc_ref)
    n_rb = (2 * tq) // ATT_RB

    rows = [slice(rb * ATT_RB, (rb + 1) * ATT_RB) for rb in range(n_rb)]

    def scores(kc, s_ref):
        for rs in rows:
            s_ref[rs, 0:kc.shape[0]] = _dot_nt(q2_ref[rs, :], kc)

    def softmax_pv(s_ref, vc):
        tk = vc.shape[0]
        v_ext = jnp.concatenate([vc, jnp.ones_like(vc)], axis=1)
        for rs in rows:
            s = s_ref[rs, 0:tk]
            m_old = m_ref[rs, :]
            m_new = jnp.maximum(m_old, jnp.max(s, axis=-1, keepdims=True))
            alpha = jnp.exp(m_old - m_new)
            pm = jnp.exp((s - jnp.concatenate([m_new] * (tk // LANES), axis=1)).astype(BF16))
            acc_ref[rs, :] = jnp.concatenate([alpha, alpha], axis=1) * acc_ref[rs, :] + _dot(pm, v_ext)
            m_ref[rs, :] = m_new

    for si, n_k in enumerate(seg_lens):
        k_ref, v_ref = kv[2 * si], kv[2 * si + 1]
        tk = min(ATT_TK, n_k)
        n_ch = n_k // tk
        if n_ch < 2:
            for j in range(n_ch):
                scores(k_ref[j * tk:(j + 1) * tk, :], sa_ref)
                softmax_pv(sa_ref, v_ref[j * tk:(j + 1) * tk, :])
            continue
        unroll = min(ATT_UNROLL, n_ch)
        assert unroll % 2 == 0 and n_ch % unroll == 0
        chunk = lambda ref, j, tk=tk: ref[pl.ds(pl.multiple_of(j * tk, tk), tk), :]
        scores(chunk(k_ref, 0), sa_ref)

        def body(jj, _, k_ref=k_ref, v_ref=v_ref, n_ch=n_ch, unroll=unroll):
            for u in range(unroll):
                j = unroll * jj + u
                cur, nxt = (sa_ref, sb_ref) if u % 2 == 0 else (sb_ref, sa_ref)
                scores(chunk(k_ref, jnp.minimum(j + 1, n_ch - 1)), nxt)
                softmax_pv(cur, chunk(v_ref, j))
            return 0

        lax.fori_loop(0, n_ch // unroll, body, 0)

    o = acc_ref[:, 0:LANES] / acc_ref[:, LANES:]
    o = o[:tq] - lam_ref[...] * o[tq:]
    ms = jnp.mean(o * o, axis=-1, keepdims=True)
    o_ref[...] = (o * lax.rsqrt(ms + NORM_EPS) * sub_ref[...] * out_scale).astype(BF16)


def _attention(qn, kn, vn, lam_vec, subln, lam_init, n_b, n_seq, n_ctx):
    vec = lambda nd: pl.BlockSpec((1, LANES), (lambda b, h, i: (0, 0)) if nd == 3 else (lambda b, h: (0, 0)))
    scratch = lambda tq: [pltpu.VMEM((2 * tq, LANES), BF16), pltpu.VMEM((2 * tq, LANES), F32),
                          pltpu.VMEM((2 * tq, 2 * LANES), F32),
                          pltpu.VMEM((2 * tq, ATT_TK), F32), pltpu.VMEM((2 * tq, ATT_TK), F32)]
    sub = subln.reshape(1, LANES)
    ctx_blk0 = n_b * n_seq // n_ctx
    qpb = n_seq // ATT_TQ
    lat_kv = pl.BlockSpec((n_seq, LANES), lambda b, h, i: (b, h))
    ctx_kv = pl.BlockSpec((n_ctx, LANES), lambda b, h, i: (ctx_blk0 + b, h))
    o_lat = pl.pallas_call(
        functools.partial(_attn_body, (n_seq, n_ctx), 1.0 - lam_init),
        grid=(n_b, DIFF_HEADS, qpb),
        in_specs=[vec(3), vec(3), pl.BlockSpec((ATT_TQ, LANES), lambda b, h, i: (b * qpb + i, h)),
                  lat_kv, lat_kv, ctx_kv, ctx_kv],
        out_specs=pl.BlockSpec((ATT_TQ, LANES), lambda b, h, i: (b * qpb + i, h)),
        out_shape=jax.ShapeDtypeStruct((n_b * n_seq, W_GROUP), BF16),
        scratch_shapes=scratch(ATT_TQ),
        compiler_params=_cparams(("arbitrary", "arbitrary", "arbitrary")),
        name="diff_attn_latent",
    )(lam_vec, sub, qn, kn, vn, kn, vn)
    ckv = pl.BlockSpec((n_ctx, LANES), lambda b, h: (ctx_blk0 + b, h))
    o_ctx = pl.pallas_call(
        functools.partial(_attn_body, (n_ctx,), 1.0 - lam_init),
        grid=(n_b, DIFF_HEADS),
        in_specs=[vec(2), vec(2), ckv, ckv, ckv],
        out_specs=pl.BlockSpec((n_ctx, LANES), lambda b, h: (b, h)),
        out_shape=jax.ShapeDtypeStruct((n_b * n_ctx, W_GROUP), BF16),
        scratch_shapes=scratch(n_ctx),
        compiler_params=_cparams(("arbitrary", "arbitrary")),
        name="diff_attn_context",
    )(lam_vec, sub, qn, kn, vn)
    return o_lat, o_ctx


def _rope_tables(n_seq):
    pos = jnp.arange(n_seq)
    row = (pos // GRID_W).astype(F32)
    col = (pos % GRID_W).astype(F32)
    inv_freq = ROPE_BASE ** (-jnp.arange(ROPE_AX, dtype=F32) / ROPE_AX)
    ang_r, ang_c = row[:, None] * inv_freq, col[:, None] * inv_freq
    z = jnp.zeros_like(ang_r)
    cos = jnp.concatenate([jnp.cos(ang_r)] * 2 + [jnp.cos(ang_c)] * 2, axis=1)
    s1 = jnp.concatenate([z, jnp.sin(ang_r), z, jnp.sin(ang_c)], axis=1)
    s2 = jnp.concatenate([-jnp.sin(ang_r), z, -jnp.sin(ang_c), z], axis=1)

    wide = lambda t: jnp.tile(t, (1, LANES // DIFF_D))
    return wide(cos), wide(s1), wide(s2)


def _rwkv_prep_body(n_b, n_l, n_c,
                    rp_ref, rc_ref, rn_ref, kp_ref, kc_ref, kn_ref, vp_ref, vc_ref, vn_ref, lp_ref, lc_ref, ln_ref,
                    mu_ref, w2_ref, a2_ref, g2_ref, w0_ref, a0_ref, kk_w_ref, ka_ref, rk_ref, e_ref,
                    r_ref, v_ref, kk_ref, lw0_ref, kd0_ref, b0_ref, lw1_ref, kd1_ref, b1_ref, bon_ref, g_ref,
                    buf_ref, bufl_ref):
    t = pl.program_id(0)
    p, n = _seg_pos(t, n_b, n_l, n_c)
    prev_ok = (p > 0).astype(F32)
    next_ok = (p < n - 1).astype(F32)

    def shifted(zp_ref, zc_ref, zn_ref, buf, mu):
        z = zc_ref[...].astype(F32)
        buf[0:HALO, :] = zp_ref[...].astype(F32) * prev_ok
        buf[HALO:HALO + SEQ_TILE, :] = z
        buf[HALO + SEQ_TILE:, :] = zn_ref[...].astype(F32) * next_ok
        return z + mu * (0.5 * (buf[HALO - 1:HALO - 1 + SEQ_TILE, :] + buf[HALO + 1:HALO + 1 + SEQ_TILE, :]) - z)

    r = shifted(rp_ref, rc_ref, rn_ref, buf_ref, mu_ref[:, 0:W_GROUP])
    k = shifted(kp_ref, kc_ref, kn_ref, buf_ref, mu_ref[:, W_GROUP:2 * W_GROUP])
    v = shifted(vp_ref, vc_ref, vn_ref, buf_ref, mu_ref[:, 2 * W_GROUP:3 * W_GROUP])
    low = shifted(lp_ref, lc_ref, ln_ref, bufl_ref, mu_ref[:, 3 * W_GROUP:])
    kk = k * kk_w_ref[...]
    ss = _segsum(kk * kk, e_ref)
    kk = kk / jnp.maximum(jnp.sqrt(ss), 1e-12)
    tw = jnp.tanh(low).astype(BF16)
    lb = low.astype(BF16)
    r_ref[...] = r.astype(r_ref.dtype)
    v_ref[...] = v.astype(v_ref.dtype)
    kk_ref[...] = kk.astype(kk_ref.dtype)
    ksum = jnp.zeros_like(k)
    for d, (lw_ref, kd_ref, b_ref) in enumerate(((lw0_ref, kd0_ref, b0_ref), (lw1_ref, kd1_ref, b1_ref))):
        y = w0_ref[d:d + 1, :] + _dot(tw, w2_ref[d])
        y = -y
        softplus = jnp.maximum(y, 0.0) + jnp.log(1.0 + jnp.exp(-jnp.abs(y)))
        lw_ref[...] = -jnp.exp(-softplus - 0.5)
        ag = jax.nn.sigmoid(a0_ref[d:d + 1, :] + _dot(lb, a2_ref[d]))
        kd = k * (1.0 + (ag - 1.0) * ka_ref[...])
        kd_ref[...] = kd.astype(kd_ref.dtype)
        b_ref[...] = (kk * ag).astype(b_ref.dtype)
        ksum = ksum + kd
    bon_ref[...] = (_segsum(r * ksum * rk_ref[...], e_ref) * v).astype(bon_ref.dtype)
    g_ref[...] = _dot(jax.nn.sigmoid(low).astype(BF16), g2_ref[...]).astype(g_ref.dtype)


def _rwkv_prep(z, p, e_seg, n_b, n_l, n_c):
    rows = z.shape[0]
    vec = pl.BlockSpec((1, W_GROUP), lambda t: (0, 0))
    two = pl.BlockSpec((2, W_GROUP), lambda t: (0, 0))
    lowm = pl.BlockSpec((2, LANES, W_GROUP), lambda t: (0, 0, 0))
    out = pl.BlockSpec((SEQ_TILE, W_GROUP), lambda t: (t, 0))
    z_specs = []
    for col, width in ((COL_R, W_GROUP), (COL_K, W_GROUP), (COL_V, W_GROUP), (COL_LOW, LANES)):
        prv, nxt = _halo_maps(lambda t: t, rows, col)
        z_specs += [pl.BlockSpec((HALO, width), prv),
                    pl.BlockSpec((SEQ_TILE, width), lambda t, col=col: (t, col)),
                    pl.BlockSpec((HALO, width), nxt)]
    return pl.pallas_call(
        functools.partial(_rwkv_prep_body, n_b, n_l, n_c),
        grid=(rows // SEQ_TILE,),
        in_specs=z_specs + [pl.BlockSpec((1, RWKV_COLS), lambda t: (0, 0)), lowm, lowm,
                            pl.BlockSpec((LANES, W_GROUP), lambda t: (0, 0)), two, two, vec, vec, vec,
                            pl.BlockSpec((W_GROUP, W_GROUP), lambda t: (0, 0))],
        out_specs=[out] * 11,
        out_shape=[jax.ShapeDtypeStruct((rows, W_GROUP), F32 if j in (3, 6) else BF16) for j in range(11)],
        scratch_shapes=[pltpu.VMEM((SEQ_TILE + 2 * HALO, W_GROUP), F32),
                        pltpu.VMEM((SEQ_TILE + 2 * HALO, LANES), F32)],
        compiler_params=_cparams(("arbitrary",)),
        name="rwkv_prep",
    )(*([z] * 12), p["mu"], p["w2"], p["a2"], p["g2"], p["w0"], p["a0"], p["kk"], p["ka"], p["rk"], e_seg)


def _rwkv_scan_body(*refs):
    c = RW_CHUNK
    in_refs, y_refs, st_ref = refs[:12], refs[12:14], refs[14]
    n_pair = RWKV_HEADS // 2

    @pl.when(pl.program_id(1) == 0)
    def _():
        st_ref[...] = jnp.zeros_like(st_ref)

    rt = lax.broadcasted_iota(jnp.int32, (c, c), 0)
    ct = lax.broadcasted_iota(jnp.int32, (c, c), 1)
    lane = lax.broadcasted_iota(jnp.int32, (c, LANES), 1)
    head0 = lane < RWKV_N
    rr = lax.broadcasted_iota(jnp.int32, (2 * c, 2 * c), 0)
    cc = lax.broadcasted_iota(jnp.int32, (2 * c, 2 * c), 1)
    same = (rr >= c) == (cc >= c)
    tt = rr & (c - 1)
    ss = cc & (c - 1)

    def stack(x):
        z = jnp.zeros_like(x)
        return jnp.concatenate([jnp.where(head0, x, z), jnp.where(head0, z, x)], axis=0).astype(BF16)

    chains = []
    for d in range(2):
        rev = d == 1
        r_ref, v_ref, kk_ref, lw_ref, kd_ref, b_ref = in_refs[6 * d:6 * d + 6]
        tri = ((rt <= ct) if rev else (rt >= ct)).astype(BF16)
        lw = lw_ref[...]
        hi = lw.astype(BF16)
        r1 = lw - hi.astype(F32)
        mid = r1.astype(BF16)
        lo = (r1 - mid.astype(F32)).astype(BF16)
        cum = _dot(tri, hi) + _dot(tri, mid) + _dot(tri, lo)
        last = 0 if rev else c - 1
        tot = cum[last:last + 1, :]
        g_end = jnp.exp(tot - cum)
        g_inv = jnp.exp(-cum)
        full = dict(a=-kk_ref[...] * jnp.exp(cum - lw), r=r_ref[...] * jnp.exp(cum), k=kd_ref[...] * g_inv,
                    b=b_ref[...] * g_inv, kg=kd_ref[...] * g_end, bg=b_ref[...] * g_end, v=v_ref[...])
        g_tot = jnp.exp(tot)
        strict = same & ((tt < ss) if rev else (tt > ss))
        incl = same & ((tt <= ss) if rev else (tt >= ss))
        for p in range(n_pair):
            sl = slice(p * LANES, (p + 1) * LANES)
            ch = {k: stack(x[:, sl]) for k, x in full.items()}
            ch.update(d=d, p=p, sl=sl, strict=strict, incl=incl, g_tot=g_tot[:, sl])
            chains.append(ch)

    c2 = 2 * c
    mask2 = lambda m, x: jnp.where(jnp.concatenate([m, m], axis=1), x, 0.0).astype(BF16)
    for ch in chains:
        kb = jnp.concatenate([ch["b"], ch["k"]], axis=0)
        sc = mask2(ch["strict"], _dot_nt(ch["a"], kb))
        ch["low"], ch["a_k"] = sc[:, :c2], sc[:, c2:]
        ch["r_bk"] = mask2(ch["incl"], _dot_nt(ch["r"], kb))
    for ch in chains:
        ch["st"] = st_ref[ch["d"], ch["p"]]
        ch["stb"] = ch["st"].astype(BF16)
        ch["u"] = _dot_nt(ch["a"], ch["stb"]) + _dot(ch["a_k"], ch["v"])
    n_sq = int(math.log2(c)) - 1
    for it in range(n_sq + 1):
        for ch in chains:
            if it < n_sq:
                both = _dot(ch["low"], jnp.concatenate([ch["low"], ch["u"].astype(BF16)], axis=1))
                ch["low"], ch["u"] = both[:, :c2].astype(BF16), ch["u"] + both[:, c2:]
            else:
                ch["u"] = ch["u"] + _dot(ch["low"], ch["u"].astype(BF16))
    for ch in chains:
        ub = ch["u"].astype(BF16)
        uv = jnp.concatenate([ub, ch["v"]], axis=0)
        y = _dot_nt(ch["r"], ch["stb"]) + _dot(ch["r_bk"], uv)
        y_refs[ch["d"]][:, ch["sl"]] = y[:c] + y[c:]
        st_ref[ch["d"], ch["p"]] = ch["st"] * ch["g_tot"] + _dot_tn(uv, jnp.concatenate([ch["bg"], ch["kg"]], axis=0))


def _rwkv_scan(pre, n_b, n_l, n_c):
    rows = pre[0].shape[0]
    blk = lambda rev: pl.BlockSpec((RW_CHUNK, W_GROUP),
                                   lambda b, i, rev=rev: (_seq_block(rev, b, i, n_c, n_l, n_b), 0))
    args, specs = [], []
    for d in range(2):
        args += [pre[0], pre[1], pre[2], pre[3 + 3 * d], pre[4 + 3 * d], pre[5 + 3 * d]]
        specs += [blk(d == 1)] * 6
    return pl.pallas_call(
        _rwkv_scan_body,
        grid=(n_b, n_c + n_l),
        in_specs=specs,
        out_specs=[blk(False), blk(True)],
        out_shape=[jax.ShapeDtypeStruct((rows, W_GROUP), F32)] * 2,
        scratch_shapes=[pltpu.VMEM((2, RWKV_HEADS // 2, LANES, LANES), F32)],
        compiler_params=_cparams(("arbitrary", "arbitrary")),
        name="rwkv_scan",
    )(*args)


def _rwkv_finish_body(yf_ref, yb_ref, bon_ref, g_ref, lg_ref, lb_ref, e_ref, o_ref):
    y = yf_ref[...] + yb_ref[...]
    dlt = y - _segsum(y, e_ref) * (1.0 / RWKV_N)
    var = _segsum(dlt * dlt, e_ref) * (1.0 / RWKV_N)
    yn = dlt * lax.rsqrt(var + GN_EPS) * lg_ref[...] + lb_ref[...]
    o_ref[...] = ((yn + bon_ref[...]) * g_ref[...]).astype(BF16)


def _rwkv_finish(yf, yb, bonus, gate, lnx_g, lnx_b, e_seg):
    rows = yf.shape[0]
    blk = pl.BlockSpec((ROW_TILE, W_GROUP), lambda i: (i, 0))
    vec = pl.BlockSpec((1, W_GROUP), lambda i: (0, 0))
    return pl.pallas_call(
        _rwkv_finish_body,
        grid=(rows // ROW_TILE,),
        in_specs=[blk, blk, blk, blk, vec, vec, pl.BlockSpec((W_GROUP, W_GROUP), lambda i: (0, 0))],
        out_specs=blk,
        out_shape=jax.ShapeDtypeStruct((rows, W_GROUP), BF16),
        compiler_params=_cparams(("arbitrary",)),
        name="rwkv_finish",
    )(yf, yb, bonus, gate, lnx_g.reshape(1, W_GROUP), lnx_b.reshape(1, W_GROUP), e_seg)


def _outproj_body(x_ref, ya_ref, yb_ref, yc_ref, yd_ref, w_ref, mod_ref, g2_ref, wrh_ref, wrl_ref, br_ref,
                  xo_ref, h2_ref, lg_ref):
    mix = _dot(ya_ref[...], w_ref[0:W_GROUP, :])
    mix = mix + _dot(yb_ref[...], w_ref[W_GROUP:2 * W_GROUP, :])
    mix = mix + _dot(yc_ref[...], w_ref[2 * W_GROUP:3 * W_GROUP, :])
    mix = mix + _dot(yd_ref[...], w_ref[3 * W_GROUP:, :])
    x = x_ref[...] + mod_ref[0, 2:3, :] * mix
    xo_ref[...] = x
    ms = jnp.mean(x * x, axis=-1, keepdims=True)
    h2 = x * lax.rsqrt(ms + NORM_EPS) * g2_ref[...]
    h2 = h2 * (1.0 + mod_ref[0, 4:5, :]) + mod_ref[0, 3:4, :]
    h2_ref[...] = h2.astype(BF16)
    lg_ref[...] = _dot3(h2, wrh_ref[...], wrl_ref[...]) + br_ref[...]


def _outproj(x, ys, w_out, mod3, g2, wr_hi, wr_lo, br, n_b, n_seq):
    rows = x.shape[0]
    big = pl.BlockSpec((OUT_TILE, D_MODEL), lambda i: (i, 0))
    yb = pl.BlockSpec((OUT_TILE, W_GROUP), lambda i: (i, 0))
    cst = lambda shape: pl.BlockSpec(shape, lambda i: (0, 0))
    return pl.pallas_call(
        _outproj_body,
        grid=(rows // OUT_TILE,),
        in_specs=[big, yb, yb, yb, yb, cst((D_MODEL, D_MODEL)),
                  pl.BlockSpec((1, 6, D_MODEL), lambda i: (_mod_row(i, OUT_TILE, n_b, n_seq), 0, 0)),
                  cst((1, D_MODEL)), cst((D_MODEL, LANES)), cst((D_MODEL, LANES)), cst((1, LANES))],
        out_specs=[big, big, pl.BlockSpec((OUT_TILE, LANES), lambda i: (i, 0))],
        out_shape=[jax.ShapeDtypeStruct((rows, D_MODEL), F32),
                   jax.ShapeDtypeStruct((rows, D_MODEL), BF16),
                   jax.ShapeDtypeStruct((rows, LANES), F32)],
        compiler_params=_cparams(("arbitrary",)),
        name="outproj_norm_router",
    )(x, ys[0], ys[1], ys[2], ys[3], w_out, mod3, g2.reshape(1, D_MODEL), wr_hi, wr_lo, br)


def _expert_body(be_ref, nv_ref, new_ref, x_ref, w1_ref, w3_ref, w2_ref, o_ref, w1b_ref, w3b_ref, w2b_ref):
    i = pl.program_id(0)

    @pl.when(new_ref[i] == 1)
    def _():
        w1b_ref[...] = w1_ref[0, 0].astype(BF16)
        w3b_ref[...] = w3_ref[0, 0].astype(BF16)
        w2b_ref[...] = w2_ref[0, 0].astype(BF16)

    @pl.when(i < nv_ref[0])
    def _():
        x = x_ref[...]
        h1 = _dot(x, w1b_ref[...])
        hid = (h1 * jax.nn.sigmoid(h1)) * _dot(x, w3b_ref[...])
        o_ref[...] = _dot(hid.astype(BF16), w2b_ref[...]).astype(o_ref.dtype)

    @pl.when(i >= nv_ref[0])
    def _():
        o_ref[...] = jnp.zeros_like(o_ref)


def _experts(xb, block_e, n_valid, layer, w1, w3, w2):
    n_blocks = xb.shape[0] // MOE_BM
    is_new = jnp.concatenate([jnp.ones((1,), jnp.int32), (block_e[1:] != block_e[:-1]).astype(jnp.int32)])
    wspec = lambda shape, nbuf=1: pl.BlockSpec((1, 1) + shape, lambda i, be, nv, nw: (layer, be[i], 0, 0),
                                               pipeline_mode=pl.Buffered(nbuf))
    return pl.pallas_call(
        _expert_body,
        grid_spec=pltpu.PrefetchScalarGridSpec(
            num_scalar_prefetch=3,
            grid=(n_blocks,),
            in_specs=[pl.BlockSpec((MOE_BM, D_MODEL), lambda i, be, nv, nw: (i, 0)),
                      wspec((D_MODEL, D_EXPERT)), wspec((D_MODEL, D_EXPERT)), wspec((D_EXPERT, D_MODEL), 2)],
            out_specs=pl.BlockSpec((MOE_BM, D_MODEL), lambda i, be, nv, nw: (i, 0)),
            scratch_shapes=[pltpu.VMEM((D_MODEL, D_EXPERT), BF16), pltpu.VMEM((D_MODEL, D_EXPERT), BF16),
                            pltpu.VMEM((D_EXPERT, D_MODEL), BF16)]),
        out_shape=jax.ShapeDtypeStruct((xb.shape[0], D_MODEL), BF16),
        compiler_params=_cparams(("arbitrary",)),
        name="moe_experts",
    )(block_e, n_valid, is_new, xb, w1, w3, w2)


def _combine_body(x_ref, y0_ref, y1_ref, rt_ref, mod_ref, o_ref):
    rt = rt_ref[...]
    mo = rt[:, RT_W:RT_W + 1] * y0_ref[...].astype(F32) + rt[:, RT_W + 1:RT_W + 2] * y1_ref[...].astype(F32)
    o_ref[...] = x_ref[...] + mod_ref[0, 5:6, :] * mo


def _combine(x, y0, y1, route, mod3, n_b, n_seq, n_out_rows):
    big = pl.BlockSpec((ROW_TILE, D_MODEL), lambda i: (i, 0))
    return pl.pallas_call(
        _combine_body,
        grid=(n_out_rows // ROW_TILE,),
        in_specs=[big, big, big, pl.BlockSpec((ROW_TILE, LANES), lambda i: (i, 0)),
                  pl.BlockSpec((1, 6, D_MODEL), lambda i: (_mod_row(i, ROW_TILE, n_b, n_seq), 0, 0))],
        out_specs=big,
        out_shape=jax.ShapeDtypeStruct((n_out_rows, D_MODEL), F32),
        compiler_params=_cparams(("arbitrary",)),
        name="moe_combine",
    )(x, y0, y1, route, mod3)


RT_E, RT_RANK, RT_W = 0, 2, 4
LOGIT_E0 = N_EGROUPS


def _route_body(lg_ref, rt_ref, cnt_ref, run_ref):
    tm = lg_ref.shape[0]

    @pl.when(pl.program_id(0) == 0)
    def _():
        run_ref[...] = jnp.zeros_like(run_ref)

    x = lg_ref[...]
    lane = lax.broadcasted_iota(jnp.int32, (tm, LANES), 1).astype(F32)
    neg = jnp.float32(-jnp.inf)

    def first_max(v):
        m = jnp.max(v, axis=-1, keepdims=True)
        return m, jnp.min(jnp.where(v == m, lane, float(LANES)), axis=-1, keepdims=True)

    g = jnp.where(lane < N_EGROUPS, x, neg)
    mg, gsel = first_max(g)
    gate_g = 1.0 / jnp.sum(jnp.exp(g - mg), axis=-1, keepdims=True)
    lo = LOGIT_E0 + EXP_PER_GROUP * gsel
    e = jnp.where((lane >= lo) & (lane < lo + EXP_PER_GROUP), x, neg)
    m1, i1 = first_max(e)
    m2, i2 = first_max(jnp.where(lane == i1, neg, e))
    e2 = jnp.exp(m2 - m1)
    w1 = gate_g / (1.0 + e2)
    w2 = gate_g * e2 / (1.0 + e2)
    oh1 = lane == i1
    oh2 = lane == i2
    oh = (oh1 | oh2).astype(BF16)
    rr = lax.broadcasted_iota(jnp.int32, (tm, tm), 0)
    cc = lax.broadcasted_iota(jnp.int32, (tm, tm), 1)
    before = _dot((rr > cc).astype(BF16), oh) + run_ref[0:1, :]
    r1 = jnp.sum(jnp.where(oh1, before, 0.0), axis=-1, keepdims=True)
    r2 = jnp.sum(jnp.where(oh2, before, 0.0), axis=-1, keepdims=True)
    run_ref[...] = run_ref[...] + jnp.sum(oh.astype(F32), axis=0, keepdims=True)
    cnt_ref[...] = run_ref[...]
    rec = jnp.where(lane == RT_E, i1 - LOGIT_E0, 0.0)
    rec = jnp.where(lane == RT_E + 1, i2 - LOGIT_E0, rec)
    rec = jnp.where(lane == RT_RANK, r1, rec)
    rec = jnp.where(lane == RT_RANK + 1, r2, rec)
    rec = jnp.where(lane == RT_W, w1, rec)
    rt_ref[...] = jnp.where(lane == RT_W + 1, w2, rec)


def _route(logits):
    rows = logits.shape[0]
    blk = pl.BlockSpec((ROW_TILE, LANES), lambda i: (i, 0))
    return pl.pallas_call(
        _route_body,
        grid=(rows // ROW_TILE,),
        in_specs=[blk],
        out_specs=[blk, pl.BlockSpec((SUBLANES, LANES), lambda i: (0, 0))],
        out_shape=[jax.ShapeDtypeStruct((rows, LANES), F32), jax.ShapeDtypeStruct((SUBLANES, LANES), F32)],
        scratch_shapes=[pltpu.VMEM((SUBLANES, LANES), F32)],
        compiler_params=_cparams(("arbitrary",)),
        name="moe_route",
    )(logits)


def _moe(x, h2, logits, mod3, layer, w1, w3, w2, n_b, n_seq, n_out_rows):
    rows = x.shape[0]
    route, cnt = _route(logits)
    counts = cnt[0, LOGIT_E0:LOGIT_E0 + N_EXPERTS].astype(jnp.int32)
    pcounts = ((counts + MOE_BM - 1) // MOE_BM) * MOE_BM
    pend = jnp.cumsum(pcounts)
    pstart = pend - pcounts
    eid = route[:, RT_E:RT_E + 2].astype(jnp.int32)
    dest = pstart[eid] + route[:, RT_RANK:RT_RANK + 2].astype(jnp.int32)
    n_blocks = -(-(2 * rows) // MOE_BM) + N_EXPERTS
    tok = jnp.broadcast_to(jnp.arange(rows, dtype=jnp.int32)[:, None], (rows, 2))
    slot_tok = (jnp.arange(n_blocks * MOE_BM, dtype=jnp.int32) % rows).at[dest.reshape(-1)].set(tok.reshape(-1))
    take_rows = lambda a, idx: a.at[idx].get(mode="promise_in_bounds")
    xb = take_rows(h2, slot_tok)
    blk_start = jnp.arange(n_blocks) * MOE_BM
    block_e = jnp.sum(blk_start[:, None] >= pend[None, :], axis=1)
    block_e = jnp.minimum(block_e, N_EXPERTS - 1).astype(jnp.int32)
    n_valid = (pend[-1] // MOE_BM).astype(jnp.int32).reshape(1)
    yb = _experts(xb, block_e, n_valid, layer, w1, w3, w2)
    y0 = take_rows(yb, dest[:n_out_rows, 0])
    y1 = take_rows(yb, dest[:n_out_rows, 1])
    return _combine(x, y0, y1, route, mod3, n_b, n_seq, n_out_rows)


def _pad_rows(w, lo, n):
    pad = [(0, 0)] * (w.ndim - 2) + [(lo, n - lo - w.shape[-2]), (0, 0)]
    return jnp.pad(w.astype(F32), pad).astype(BF16)


def kernel(x, c, ctx, c_ctx, w_ada, b_ada, g_norm1, g_norm2, w_in, w_out, s5_lam_re, s5_lam_im, s5_log_step, s5_b_re, s5_b_im, s5_c_re, s5_c_im, s5_d, s5_w_glu, s5_b_glu, lru_conv_w, lru_conv_b, lru_lam, lru_wa, lru_ba, lru_wx, lru_bx, diff_gq, diff_gk, diff_lq1, diff_lk1, diff_lq2, diff_lk2, diff_subln, rw_mu, rw_w0, rw_w2, rw_a0, rw_a2, rw_g2, rw_kk, rw_ka, rw_rk, rw_lnx_g, rw_lnx_b, moe_w_rg, moe_b_rg, moe_w_re, moe_b_re, moe_w1, moe_w3, moe_w2):
    n_b, n_seq, _ = x.shape
    n_ctx = ctx.shape[1]
    depth = w_ada.shape[0]
    assert n_seq % SEQ_TILE == 0 and n_ctx % SEQ_TILE == 0 and n_seq % n_ctx == 0
    assert (n_b * n_ctx) % ROW_TILE == 0 and n_seq % ROW_TILE == 0 and n_seq % ATT_TK == 0
    xs = jnp.concatenate([x.reshape(n_b * n_seq, D_MODEL), ctx.reshape(n_b * n_ctx, D_MODEL)], axis=0)
    rows = xs.shape[0]
    cvec = jnp.zeros((SUBLANES, D_MODEL), F32).at[0].set(c_ctx).at[1:1 + n_b].set(c)
    rope = _rope_tables(n_seq)
    seg = jnp.arange(W_GROUP) // RWKV_N
    e_seg = (seg[:, None] == seg[None, :]).astype(BF16)
    nl_s, nc_s = n_seq // SEQ_TILE, n_ctx // SEQ_TILE
    nl_r, nc_r = n_seq // RW_CHUNK, n_ctx // RW_CHUNK

    for l in range(depth):
        lam_init = 0.8 - 0.6 * math.exp(-0.3 * l)
        mod = _modulation(cvec, l, w_ada, b_ada[l])
        mod3 = mod[:1 + n_b].reshape(1 + n_b, 6, D_MODEL)
        w_in_b = jnp.pad(w_in[l].astype(BF16), ((0, 0), (0, D_IN_PAD - D_IN)))
        za = _inproj(xs, mod3, g_norm1[l], w_in_b, n_b, n_seq)
        s5p = _s5_prepare(s5_lam_re[l], s5_lam_im[l], s5_log_step[l], s5_b_re[l], s5_b_im[l],
                          s5_c_re[l], s5_c_im[l])
        y5 = [_s5_scan(za, s5p, d, n_b, nl_s, nc_s) for d in range(2)]
        ya = _s5_finish(y5[0], y5[1], za, s5_d[l], s5_w_glu[l], s5_b_glu[l])
        lrp = dict(cw=lru_conv_w[l], cb=lru_conv_b[l].reshape(1, W_GROUP),
                   wa=_blockdiag(lru_wa[l]).astype(BF16), wx=_blockdiag(lru_wx[l]).astype(BF16),
                   ba=lru_ba[l].reshape(2, 1, W_GROUP), bx=lru_bx[l].reshape(2, 1, W_GROUP),
                   sp=jax.nn.softplus(-lru_lam[l]).reshape(2, 1, W_GROUP))
        hl = [_lru_scan(za, lrp, d, n_b, nl_s, nc_s) for d in range(2)]
        yb_ = _lru_finish(hl[0], hl[1], za)
        lam = jnp.exp(jnp.sum(diff_lq1[l] * diff_lk1[l])) - jnp.exp(jnp.sum(diff_lq2[l] * diff_lk2[l])) + lam_init
        lam_vec = jnp.full((1, LANES), lam, F32)
        qn, kn, vn = _qkv_prep(za, rope, diff_gq[l], diff_gk[l], e_seg, n_b, n_seq)
        o_lat, o_ctx = _attention(qn, kn, vn, lam_vec, diff_subln[l], lam_init, n_b, n_seq, n_ctx)
        yc = jnp.concatenate([o_lat, o_ctx], axis=0)
        lo_w, lo_a = RANK_G, RANK_G + RANK_W
        rwp = dict(mu=rw_mu[l].reshape(1, RWKV_COLS),
                   w2=_pad_rows(rw_w2[l], lo_w, LANES), a2=_pad_rows(rw_a2[l], lo_a, LANES),
                   g2=_pad_rows(rw_g2[l], 0, LANES), w0=rw_w0[l], a0=rw_a0[l],
                   kk=rw_kk[l].reshape(1, W_GROUP), ka=rw_ka[l].reshape(1, W_GROUP),
                   rk=rw_rk[l].reshape(1, W_GROUP))
        pre = _rwkv_prep(za, rwp, e_seg, n_b, nl_s, nc_s)
        yr = _rwkv_scan(pre, n_b, nl_r, nc_r)
        yd = _rwkv_finish(yr[0], yr[1], pre[9], pre[10], rw_lnx_g[l], rw_lnx_b[l], e_seg)
        wr = jnp.zeros((D_MODEL, LANES), F32).at[:, :N_EGROUPS].set(moe_w_rg[l])
        wr = wr.at[:, N_EGROUPS:N_EGROUPS + N_EXPERTS].set(moe_w_re[l])
        wr_hi, wr_lo = _split2(wr)
        br = jnp.zeros((1, LANES), F32).at[0, :N_EGROUPS].set(moe_b_rg[l])
        br = br.at[0, N_EGROUPS:N_EGROUPS + N_EXPERTS].set(moe_b_re[l])
        xs, h2, logits = _outproj(xs, (ya, yb_, yc, yd), w_out[l].astype(BF16), mod3, g_norm2[l],
                                  wr_hi, wr_lo, br, n_b, n_seq)
        n_out_rows = rows if l < depth - 1 else n_b * n_seq
        xs = _moe(xs, h2, logits, mod3, l, moe_w1, moe_w3, moe_w2, n_b, n_seq, n_out_rows)
    return xs.reshape(n_b, n_seq, D_MODEL)
```

```python
import functools
import math

import jax
import jax.numpy as jnp
from jax import lax
from jax.experimental import pallas as pl
from jax.experimental.pallas import tpu as pltpu

F32 = jnp.float32
BF16 = jnp.bfloat16

D_MODEL = 2048
W_GROUP = 512
S5_CH, S5_GROUPS, S5_STATE = 16, 32, 64
S5_NS = S5_GROUPS * S5_STATE
LRU_C = 8.0
DIFF_HEADS, DIFF_D = 4, 64
ROPE_AX = 16
ROPE_BASE = 10000.0
RWKV_N, RWKV_HEADS = 64, 8
RANK_G, RANK_W, RANK_A = 64, 32, 32
RWKV_COLS = 3 * W_GROUP + RANK_G + RANK_W + RANK_A
N_EGROUPS, EXP_PER_GROUP, N_EXPERTS = 4, 8, 32
D_EXPERT = 1024
NORM_EPS = 1e-6
GN_EPS = 64e-5
GRID_W = 64

SUBLANES = 8
LANES = 128
ROW_TILE = 512
SEQ_TILE = 256
RW_CHUNK = 64
ATT_TQ = 256
ATT_UNROLL = 32
ATT_TK = 512
ATT_RB = 128
MOE_BM = 256
OUT_TILE = 256
IN_TN = 1024
D_IN = 6 * W_GROUP + RWKV_COLS
D_IN_PAD = -(-D_IN // IN_TN) * IN_TN
COL_R, COL_K, COL_V = 6, 7, 8
COL_LOW = (9 * W_GROUP) // LANES
VMEM_LIMIT = 56 * 1024 * 1024


def _cparams(sem):
    return pltpu.CompilerParams(dimension_semantics=sem, vmem_limit_bytes=VMEM_LIMIT)


def _split2(x):
    hi = x.astype(BF16)
    lo = (x - hi.astype(F32)).astype(BF16)
    return hi, lo


def _dot(a, b):
    return jnp.dot(a, b, preferred_element_type=F32)


def _dot_nt(a, b):
    return lax.dot_general(a, b, (((1,), (1,)), ((), ())), preferred_element_type=F32)


def _dot_tn(a, b):
    return lax.dot_general(a, b, (((0,), (0,)), ((), ())), preferred_element_type=F32)


def _dot3(a, b_hi, b_lo):
    a_hi, a_lo = _split2(a)
    return _dot(a_hi, b_hi) + _dot(a_hi, b_lo) + _dot(a_lo, b_hi)


def _segsum(x, e_ref):
    hi, lo = _split2(x)
    e = e_ref[...]
    return _dot(hi, e) + _dot(lo, e)


def _seq_block(rev, b, i, n_c, n_l, n_b):
    if rev:
        ctx = n_b * n_l + b * n_c + (n_c - 1 - i)
        lat = b * n_l + (n_l - 1 - (i - n_c))
    else:
        ctx = n_b * n_l + b * n_c + i
        lat = b * n_l + (i - n_c)
    return jnp.where(i < n_c, ctx, lat)


def _seg_pos(t, n_b, n_l, n_c):
    is_lat = t < n_b * n_l
    p = jnp.where(is_lat, t % n_l, (t - n_b * n_l) % n_c)
    n = jnp.where(is_lat, n_l, n_c)
    return p, n


def _mod_body(c_ref, w_ref, b_ref, o_ref):
    c = c_ref[...]
    s = c * jax.nn.sigmoid(c)
    w_hi, w_lo = _split2(w_ref[0])
    o_ref[...] = _dot3(s, w_hi, w_lo) + b_ref[...]


def _modulation(cvec, layer, w_ada, b_ada):
    d6 = w_ada.shape[2]
    tn = 1536
    return pl.pallas_call(
        _mod_body,
        grid=(d6 // tn,),
        in_specs=[pl.BlockSpec((SUBLANES, D_MODEL), lambda j: (0, 0)),
                  pl.BlockSpec((1, D_MODEL, tn), lambda j: (layer, 0, j)),
                  pl.BlockSpec((1, tn), lambda j: (0, j))],
        out_specs=pl.BlockSpec((SUBLANES, tn), lambda j: (0, j)),
        out_shape=jax.ShapeDtypeStruct((SUBLANES, d6), F32),
        compiler_params=_cparams(("arbitrary",)),
        name="adaln_mod",
    )(cvec, w_ada, b_ada.reshape(1, d6))


def _mod_row(i, tile, n_b, n_seq):
    return jnp.where(i < n_b * n_seq // tile, 1 + i // (n_seq // tile), 0)


def _inproj_body(x_ref, mod_ref, g_ref, w_ref, z_ref, h_ref):
    @pl.when(pl.program_id(1) == 0)
    def _():
        x = x_ref[...]
        ms = jnp.mean(x * x, axis=-1, keepdims=True)
        xn = x * lax.rsqrt(ms + NORM_EPS) * g_ref[...]
        h_ref[...] = (xn * (1.0 + mod_ref[0, 1:2, :]) + mod_ref[0, 0:1, :]).astype(BF16)

    z_ref[...] = _dot(h_ref[...], w_ref[...]).astype(z_ref.dtype)


def _inproj(x, mod3, g1, w, n_b, n_seq):
    rows = x.shape[0]
    tn = IN_TN
    return pl.pallas_call(
        _inproj_body,
        grid=(rows // ROW_TILE, w.shape[1] // tn),
        in_specs=[pl.BlockSpec((ROW_TILE, D_MODEL), lambda i, j: (i, 0)),
                  pl.BlockSpec((1, 6, D_MODEL), lambda i, j: (_mod_row(i, ROW_TILE, n_b, n_seq), 0, 0)),
                  pl.BlockSpec((1, D_MODEL), lambda i, j: (0, 0)),
                  pl.BlockSpec((D_MODEL, tn), lambda i, j: (0, j))],
        out_specs=pl.BlockSpec((ROW_TILE, tn), lambda i, j: (i, j)),
        out_shape=jax.ShapeDtypeStruct((rows, w.shape[1]), BF16),
        scratch_shapes=[pltpu.VMEM((ROW_TILE, D_MODEL), BF16)],
        compiler_params=_cparams(("arbitrary", "arbitrary")),
        name="norm_inproj",
    )(x, mod3, g1.reshape(1, D_MODEL), w)


S5_SEG = SEQ_TILE // SUBLANES
S5_HALF = 2


def _s5_body(rev, u_ref, bre_ref, bim_ref, dec_ref, dseg_ref, t3_ref, pw_ref, cre_ref, cim_ref,
             y_ref, sr_ref, si_ref, car_ref, loc_ref):
    n_k = S5_SEG
    hw, hs = W_GROUP // S5_HALF, S5_NS // S5_HALF

    @pl.when(pl.program_id(1) == 0)
    def _():
        car_ref[...] = jnp.zeros_like(car_ref)

    ri = lax.broadcasted_iota(jnp.int32, (SEQ_TILE, SEQ_TILE), 0)
    ci = lax.broadcasted_iota(jnp.int32, (SEQ_TILE, SEQ_TILE), 1)
    lg8, lgk = int(math.log2(SUBLANES)), int(math.log2(n_k))
    perm = (ci == ((ri & (SUBLANES - 1)) << lgk) + (ri >> lg8)).astype(BF16)
    up = _dot(perm, u_ref[...].astype(BF16)).astype(BF16)
    for h in range(S5_HALF):
        uh = up[:, h * hw:(h + 1) * hw]
        sr_ref[:, :, h * hs:(h + 1) * hs] = _dot(uh, bre_ref[0, h]).reshape(n_k, SUBLANES, hs)
        si_ref[:, :, h * hs:(h + 1) * hs] = _dot(uh, bim_ref[0, h]).reshape(n_k, SUBLANES, hs)

    loc_ref[...] = jnp.zeros_like(loc_ref)
    ar, ai = dec_ref[0, 0], dec_ref[0, 1]

    def step(k, _):
        kk = (n_k - 1 - k) if rev else k
        cr, ci = loc_ref[0], loc_ref[1]
        nr = ar * cr - ai * ci + sr_ref[kk]
        ni = ar * ci + ai * cr + si_ref[kk]
        sr_ref[kk] = nr
        si_ref[kk] = ni
        loc_ref[0] = nr
        loc_ref[1] = ni
        return 0

    lax.fori_loop(0, n_k, step, 0)

    sub = lax.broadcasted_iota(jnp.int32, (SUBLANES, S5_NS), 0)
    first = (SUBLANES - 1) if rev else 0
    sh1 = (SUBLANES - 1) if rev else 1
    lr, li = loc_ref[0], loc_ref[1]
    er = jnp.where(sub == first, car_ref[0], pltpu.roll(lr, sh1, axis=0))
    ei = jnp.where(sub == first, car_ref[1], pltpu.roll(li, sh1, axis=0))
    for lvl, s in enumerate((1, 2, 4)):
        sh = (SUBLANES - s) if rev else s
        pr, pi = pltpu.roll(er, sh, axis=0), pltpu.roll(ei, sh, axis=0)
        tr, ti = t3_ref[0, 0, lvl], t3_ref[0, 1, lvl]
        er, ei = er + tr * pr - ti * pi, ei + tr * pi + ti * pr
    dr, di = dseg_ref[0, 0], dseg_ref[0, 1]
    last = 0 if rev else SUBLANES - 1
    fr = lr + dr * er - di * ei
    fi = li + dr * ei + di * er
    car_ref[0] = jnp.broadcast_to(fr[last:last + 1, :], (SUBLANES, S5_NS))
    car_ref[1] = jnp.broadcast_to(fi[last:last + 1, :], (SUBLANES, S5_NS))

    pr, pi = pw_ref[0, 0], pw_ref[0, 1]
    s_r = (sr_ref[...] + pr * er[None] - pi * ei[None]).reshape(SEQ_TILE, S5_NS).astype(BF16)
    s_i = (si_ref[...] + pr * ei[None] + pi * er[None]).reshape(SEQ_TILE, S5_NS).astype(BF16)
    ys = []
    for h in range(S5_HALF):
        cols = slice(h * hs, (h + 1) * hs)
        ys.append(_dot(s_r[:, cols], cre_ref[0, h]) - _dot(s_i[:, cols], cim_ref[0, h]))
    y_hi, y_lo = _split2(jnp.concatenate(ys, axis=1))
    inv = (ci == ((ri & (n_k - 1)) << lg8) + (ri >> lgk)).astype(BF16)
    y_ref[...] = _dot(inv, y_hi) + _dot(inv, y_lo)


def _s5_scan(za, p, d, n_b, n_l, n_c):
    rev = d == 1
    rows = za.shape[0]
    seq = lambda b, i: (_seq_block(rev, b, i, n_c, n_l, n_b), 0)
    cst = lambda nd: (lambda b, i: (d,) + (0,) * (nd - 1))
    hw, hs = W_GROUP // S5_HALF, S5_NS // S5_HALF
    state = pltpu.VMEM((S5_SEG, SUBLANES, S5_NS), F32)
    pair = pltpu.VMEM((2, SUBLANES, S5_NS), F32)
    return pl.pallas_call(
        functools.partial(_s5_body, rev),
        grid=(n_b, n_c + n_l),
        in_specs=[pl.BlockSpec((SEQ_TILE, W_GROUP), seq),
                  pl.BlockSpec((1, S5_HALF, hw, hs), cst(4)),
                  pl.BlockSpec((1, S5_HALF, hw, hs), cst(4)),
                  pl.BlockSpec((1, 2, SUBLANES, S5_NS), cst(4)),
                  pl.BlockSpec((1, 2, SUBLANES, S5_NS), cst(4)),
                  pl.BlockSpec((1, 2, 3, SUBLANES, S5_NS), cst(5)),
                  pl.BlockSpec((1, 2, S5_SEG, SUBLANES, S5_NS), cst(5)),
                  pl.BlockSpec((1, S5_HALF, hs, hw), cst(4)),
                  pl.BlockSpec((1, S5_HALF, hs, hw), cst(4))],
        out_specs=pl.BlockSpec((SEQ_TILE, W_GROUP), seq),
        out_shape=jax.ShapeDtypeStruct((rows, W_GROUP), F32),
        scratch_shapes=[state, state, pair, pair],
        compiler_params=_cparams(("arbitrary", "arbitrary")),
        name="s5_scan_rev" if rev else "s5_scan_fwd",
    )(za, p["bre"], p["bim"], p["dec"], p["dseg"], p["t3"], p["pw"], p["cre"], p["cim"])


def _s5_prepare(lam_re, lam_im, log_step, b_re, b_im, c_re, c_im):
    lr, li = lam_re.astype(F32), lam_im.astype(F32)
    dt = jnp.exp(log_step.astype(F32))[..., None]
    mag = jnp.exp(lr * dt)
    ar, ai = mag * jnp.cos(li * dt), mag * jnp.sin(li * dt)
    den = lr * lr + li * li
    cr = ((ar - 1.0) * lr + ai * li) / den
    ci = (ai * lr - (ar - 1.0) * li) / den
    bbr = cr[..., None] * b_re - ci[..., None] * b_im
    bbi = cr[..., None] * b_im + ci[..., None] * b_re
    gh = S5_GROUPS // S5_HALF
    eye = jnp.eye(gh, dtype=F32)

    def bd(t):
        t = t.reshape(2, S5_HALF, gh, S5_STATE, S5_CH)
        return jnp.einsum('dxgph,gk->dxghkp', t, eye).reshape(2, S5_HALF, gh * S5_CH, gh * S5_STATE).astype(BF16)

    def cd(t):
        t = t.astype(F32).reshape(2, S5_HALF, gh, S5_CH, S5_STATE)
        return jnp.einsum('dxghp,gk->dxgpkh', t, eye).reshape(2, S5_HALF, gh * S5_STATE, gh * S5_CH).astype(BF16)

    cmul = lambda x, y: (x[0] * y[0] - x[1] * y[1], x[0] * y[1] + x[1] * y[0])
    dec = (ar.reshape(2, S5_NS), ai.reshape(2, S5_NS))
    pw = [dec]
    for _ in range(S5_SEG - 1):
        pw.append(cmul(pw[-1], dec))
    seg = [pw[-1]]
    for _ in range(2):
        seg.append(cmul(seg[-1], seg[-1]))
    wide = lambda t: jnp.broadcast_to(t[:, None, :], (2, SUBLANES, S5_NS))
    pair = lambda z: jnp.stack([wide(z[0]), wide(z[1])], axis=1)
    k = jnp.arange(SUBLANES)
    t3 = []
    for j, s in enumerate((1, 2, 4)):
        m = jnp.stack([k >= s, k + s <= SUBLANES - 1])
        t3.append(jnp.where(m[:, None, :, None], pair(seg[j]), 0.0))
    t3 = jnp.stack(t3, axis=2)
    pw_f = jnp.stack([jnp.stack([wide(p[0])[0], wide(p[1])[0]]) for p in pw], axis=1)
    pw_b = jnp.stack([jnp.stack([wide(p[0])[1], wide(p[1])[1]]) for p in reversed(pw)], axis=1)
    return dict(bre=bd(bbr), bim=bd(bbi), cre=cd(c_re), cim=cd(c_im), dec=pair(dec), dseg=pair(seg[0]),
                t3=t3, pw=jnp.stack([pw_f, pw_b]))


def _s5_finish_body(yf_ref, yb_ref, u_ref, d_ref, w_ref, b_ref, o_ref):
    y = jax.nn.gelu(yf_ref[...] + yb_ref[...] + d_ref[...] * u_ref[...].astype(F32))
    gate = jax.nn.sigmoid(_dot(y.astype(BF16), w_ref[...]) + b_ref[...])
    o_ref[...] = (y * gate).astype(BF16)


def _s5_finish(yf, yb, za, d_skip, w_glu, b_glu):
    rows = yf.shape[0]
    blk = pl.BlockSpec((ROW_TILE, W_GROUP), lambda i: (i, 0))
    vec = pl.BlockSpec((1, W_GROUP), lambda i: (0, 0))
    return pl.pallas_call(
        _s5_finish_body,
        grid=(rows // ROW_TILE,),
        in_specs=[blk, blk, blk, vec, pl.BlockSpec((W_GROUP, W_GROUP), lambda i: (0, 0)), vec],
        out_specs=blk,
        out_shape=jax.ShapeDtypeStruct((rows, W_GROUP), BF16),
        compiler_params=_cparams(("arbitrary",)),
        name="s5_finish",
    )(yf, yb, za, d_skip.reshape(1, W_GROUP), w_glu.astype(BF16), b_glu.reshape(1, W_GROUP))


HALO = 2 * SUBLANES


def _halo_maps(tile_fn, rows, col):
    per = SEQ_TILE // HALO
    last = rows // HALO - 1
    prv = lambda *ids: (jnp.maximum(tile_fn(*ids) * per - 1, 0), col)
    nxt = lambda *ids: (jnp.minimum((tile_fn(*ids) + 1) * per, last), col)
    return prv, nxt


def _lru_body(rev, n_b, n_l, n_c, xp_ref, xc_ref, xn_ref, cw_ref, cb_ref, wa_ref, ba_ref, wx_ref, bx_ref,
              sp_ref, h_ref, buf_ref, a_ref, b_ref, car_ref):
    g_n = SEQ_TILE // SUBLANES
    b = pl.program_id(0)
    i = pl.program_id(1)

    @pl.when(i == 0)
    def _():
        car_ref[...] = jnp.zeros_like(car_ref)

    p, n = _seg_pos(_seq_block(rev, b, i, n_c, n_l, n_b), n_b, n_l, n_c)
    prev_ok = (p > 0).astype(F32)
    next_ok = (p < n - 1).astype(F32)
    xc = xc_ref[...].astype(F32)
    buf_ref[0:HALO, :] = xp_ref[...].astype(F32) * prev_ok
    buf_ref[HALO:HALO + SEQ_TILE, :] = xc
    buf_ref[HALO + SEQ_TILE:, :] = xn_ref[...].astype(F32) * next_ok
    v = cb_ref[...] + cw_ref[2:3, :] * xc
    v = v + cw_ref[0:1, :] * buf_ref[HALO - 2:HALO - 2 + SEQ_TILE, :]
    v = v + cw_ref[1:2, :] * buf_ref[HALO - 1:HALO - 1 + SEQ_TILE, :]
    v = v + cw_ref[3:4, :] * buf_ref[HALO + 1:HALO + 1 + SEQ_TILE, :]
    vb = v.astype(BF16)
    r = jax.nn.sigmoid(_dot(vb, wa_ref[0]) + ba_ref[0])
    ig = jax.nn.sigmoid(_dot(vb, wx_ref[0]) + bx_ref[0])
    a = jnp.exp(-LRU_C * r * sp_ref[0])
    bb = jnp.sqrt(1.0 - a * a) * (ig * v)
    a3 = a.reshape(g_n, SUBLANES, W_GROUP)
    b3 = bb.reshape(g_n, SUBLANES, W_GROUP)
    k = lax.broadcasted_iota(jnp.int32, (g_n, SUBLANES, W_GROUP), 1)
    for s in (1, 2, 4):
        sh = (SUBLANES - s) if rev else s
        m = (k + s <= SUBLANES - 1) if rev else (k >= s)
        a_s = pltpu.roll(a3, sh, axis=1)
        b_s = pltpu.roll(b3, sh, axis=1)
        b3 = jnp.where(m, a3 * b_s + b3, b3)
        a3 = jnp.where(m, a3 * a_s, a3)
    a_ref[...] = a3
    b_ref[...] = b3
    row = 0 if rev else SUBLANES - 1

    def step(g, _):
        gg = (g_n - 1 - g) if rev else g
        hh = b_ref[gg] + a_ref[gg] * car_ref[...]
        b_ref[gg] = hh
        car_ref[...] = jnp.broadcast_to(hh[row:row + 1, :], (SUBLANES, W_GROUP))
        return 0

    lax.fori_loop(0, g_n, step, 0)
    h_ref[...] = b_ref[...].reshape(SEQ_TILE, W_GROUP)


def _lru_scan(za, p, d, n_b, n_l, n_c):
    rev = d == 1
    rows = za.shape[0]
    tile = lambda b, i: _seq_block(rev, b, i, n_c, n_l, n_b)
    cur = lambda b, i: (tile(b, i), 1)
    prv, nxt = _halo_maps(tile, rows, 1)
    out = lambda b, i: (tile(b, i), 0)
    vec = pl.BlockSpec((1, W_GROUP), lambda b, i: (0, 0))
    dvec = pl.BlockSpec((1, 1, W_GROUP), lambda b, i: (d, 0, 0))
    dmat = pl.BlockSpec((1, W_GROUP, W_GROUP), lambda b, i: (d, 0, 0))
    blk = lambda f: pl.BlockSpec((SEQ_TILE, W_GROUP), f)
    halo = lambda f: pl.BlockSpec((HALO, W_GROUP), f)
    g_n = SEQ_TILE // SUBLANES
    return pl.pallas_call(
        functools.partial(_lru_body, rev, n_b, n_l, n_c),
        grid=(n_b, n_c + n_l),
        in_specs=[halo(prv), blk(cur), halo(nxt),
                  pl.BlockSpec((4, W_GROUP), lambda b, i: (0, 0)), vec,
                  dmat, dvec, dmat, dvec, dvec],
        out_specs=blk(out),
        out_shape=jax.ShapeDtypeStruct((rows, W_GROUP), F32),
        scratch_shapes=[pltpu.VMEM((SEQ_TILE + 2 * HALO, W_GROUP), F32),
                        pltpu.VMEM((g_n, SUBLANES, W_GROUP), F32),
                        pltpu.VMEM((g_n, SUBLANES, W_GROUP), F32),
                        pltpu.VMEM((SUBLANES, W_GROUP), F32)],
        compiler_params=_cparams(("arbitrary", "arbitrary")),
        name="lru_scan_rev" if rev else "lru_scan_fwd",
    )(za, za, za, p["cw"], p["cb"], p["wa"], p["ba"], p["wx"], p["bx"], p["sp"])


def _blockdiag(w):
    nb, c = w.shape[1], w.shape[2]
    eye = jnp.eye(nb, dtype=F32)
    return jnp.einsum('dncf,nm->dncmf', w.astype(F32), eye).reshape(2, nb * c, nb * c)


def _lru_finish_body(hf_ref, hb_ref, g_ref, o_ref):
    o_ref[...] = ((hf_ref[...] + hb_ref[...]) * jax.nn.gelu(g_ref[...].astype(F32))).astype(BF16)


def _lru_finish(hf, hb, za):
    rows = hf.shape[0]
    blk = pl.BlockSpec((ROW_TILE, W_GROUP), lambda i: (i, 0))
    return pl.pallas_call(
        _lru_finish_body,
        grid=(rows // ROW_TILE,),
        in_specs=[blk, blk, pl.BlockSpec((ROW_TILE, W_GROUP), lambda i: (i, 2))],
        out_specs=blk,
        out_shape=jax.ShapeDtypeStruct((rows, W_GROUP), BF16),
        compiler_params=_cparams(("arbitrary",)),
        name="lru_finish",
    )(hf, hb, za)


def _qkv_prep_body(n_lat_tiles, q_ref, k_ref, v_ref, cos_ref, s1_ref, s2_ref, gq_ref, gk_ref, e_ref,
                   qo_ref, ko_ref, vo_ref):
    reps = W_GROUP // LANES
    is_ctx = pl.program_id(0) >= n_lat_tiles
    cos = jnp.concatenate([jnp.where(is_ctx, 1.0, cos_ref[...])] * reps, axis=1)
    s1 = jnp.concatenate([jnp.where(is_ctx, 0.0, s1_ref[...])] * reps, axis=1)
    s2 = jnp.concatenate([jnp.where(is_ctx, 0.0, s2_ref[...])] * reps, axis=1)

    def prep(x, g):
        ms = _segsum(x * x, e_ref) * (1.0 / DIFF_D)
        x = x * lax.rsqrt(ms + NORM_EPS) * g
        return x * cos + pltpu.roll(x, ROPE_AX, axis=1) * s1 + pltpu.roll(x, W_GROUP - ROPE_AX, axis=1) * s2

    qo_ref[...] = (prep(q_ref[...].astype(F32), gq_ref[...]) * (DIFF_D ** -0.5)).astype(BF16)
    ko_ref[...] = prep(k_ref[...].astype(F32), gk_ref[...]).astype(BF16)
    vo_ref[...] = v_ref[...].astype(BF16)


def _qkv_prep(za, rope, gq, gk, e_seg, n_b, n_seq):
    rows = za.shape[0]
    per_batch = n_seq // ROW_TILE
    col = lambda c: pl.BlockSpec((ROW_TILE, W_GROUP), lambda i: (i, c))
    tab = pl.BlockSpec((ROW_TILE, LANES), lambda i: (i % per_batch, 0))
    vec = pl.BlockSpec((1, W_GROUP), lambda i: (0, 0))
    out = pl.BlockSpec((ROW_TILE, W_GROUP), lambda i: (i, 0))
    reps = W_GROUP // DIFF_D
    return pl.pallas_call(
        functools.partial(_qkv_prep_body, n_b * per_batch),
        grid=(rows // ROW_TILE,),
        in_specs=[col(3), col(4), col(5), tab, tab, tab, vec, vec,
                  pl.BlockSpec((W_GROUP, W_GROUP), lambda i: (0, 0))],
        out_specs=[out, out, out],
        out_shape=[jax.ShapeDtypeStruct((rows, W_GROUP), BF16)] * 3,
        compiler_params=_cparams(("arbitrary",)),
        name="attn_qkv_prep",
    )(za, za, za, rope[0], rope[1], rope[2], jnp.tile(gq, reps).reshape(1, W_GROUP),
      jnp.tile(gk, reps).reshape(1, W_GROUP), e_seg)


def _attn_body(seg_lens, out_scale, lam_ref, sub_ref, q_ref, *rest):
    o_ref, q2_ref, m_ref, acc_ref, sa_ref, sb_ref = rest[-6:]
    kv = rest[:-6]
    tq = q_ref.shape[0]
    q = q_ref[...]
    lane = lax.broadcasted_iota(jnp.int32, (tq, LANES), 1)
    zero = jnp.zeros_like(q)
    q2 = jnp.concatenate([jnp.where(lane < DIFF_D, q, zero), jnp.where(lane < DIFF_D, zero, q)], axis=0)
    q2_ref[...] = q2
    m_ref[...] = jnp.full_like(m_ref, -jnp.inf)
    acc_ref[...] = jnp.zeros_like(acc_ref)
    n_rb = (2 * tq) // ATT_RB

    rows = [slice(rb * ATT_RB, (rb + 1) * ATT_RB) for rb in range(n_rb)]

    def scores(kc, s_ref):
        for rs in rows:
            s_ref[rs, 0:kc.shape[0]] = _dot_nt(q2_ref[rs, :], kc)

    def softmax_pv(s_ref, vc):
        tk = vc.shape[0]
        v_ext = jnp.concatenate([vc, jnp.ones_like(vc)], axis=1)
        for rs in rows:
            s = s_ref[rs, 0:tk]
            m_old = m_ref[rs, :]
            m_new = jnp.maximum(m_old, jnp.max(s, axis=-1, keepdims=True))
            alpha = jnp.exp(m_old - m_new)
            pm = jnp.exp((s - jnp.concatenate([m_new] * (tk // LANES), axis=1)).astype(BF16))
            acc_ref[rs, :] = jnp.concatenate([alpha, alpha], axis=1) * acc_ref[rs, :] + _dot(pm, v_ext)
            m_ref[rs, :] = m_new

    for si, n_k in enumerate(seg_lens):
        k_ref, v_ref = kv[2 * si], kv[2 * si + 1]
        tk = min(ATT_TK, n_k)
        n_ch = n_k // tk
        if n_ch < 2:
            for j in range(n_ch):
                scores(k_ref[j * tk:(j + 1) * tk, :], sa_ref)
                softmax_pv(sa_ref, v_ref[j * tk:(j + 1) * tk, :])
            continue
        unroll = min(ATT_UNROLL, n_ch)
        assert unroll % 2 == 0 and n_ch % unroll == 0
        chunk = lambda ref, j, tk=tk: ref[pl.ds(pl.multiple_of(j * tk, tk), tk), :]
        scores(chunk(k_ref, 0), sa_ref)

        def body(jj, _, k_ref=k_ref, v_ref=v_ref, n_ch=n_ch, unroll=unroll):
            for u in range(unroll):
                j = unroll * jj + u
                cur, nxt = (sa_ref, sb_ref) if u % 2 == 0 else (sb_ref, sa_ref)
                scores(chunk(k_ref, jnp.minimum(j + 1, n_ch - 1)), nxt)
                softmax_pv(cur, chunk(v_ref, j))
            return 0

        lax.fori_loop(0, n_ch // unroll, body, 0)

    o = acc_ref[:, 0:LANES] / acc_ref[:, LANES:]
    o = o[:tq] - lam_ref[...] * o[tq:]
    ms = jnp.mean(o * o, axis=-1, keepdims=True)
    o_ref[...] = (o * lax.rsqrt(ms + NORM_EPS) * sub_ref[...] * out_scale).astype(BF16)


def _attention(qn, kn, vn, lam_vec, subln, lam_init, n_b, n_seq, n_ctx):
    vec = lambda nd: pl.BlockSpec((1, LANES), (lambda b, h, i: (0, 0)) if nd == 3 else (lambda b, h: (0, 0)))
    scratch = lambda tq: [pltpu.VMEM((2 * tq, LANES), BF16), pltpu.VMEM((2 * tq, LANES), F32),
                          pltpu.VMEM((2 * tq, 2 * LANES), F32),
                          pltpu.VMEM((2 * tq, ATT_TK), F32), pltpu.VMEM((2 * tq, ATT_TK), F32)]
    sub = subln.reshape(1, LANES)
    ctx_blk0 = n_b * n_seq // n_ctx
    qpb = n_seq // ATT_TQ
    lat_kv = pl.BlockSpec((n_seq, LANES), lambda b, h, i: (b, h))
    ctx_kv = pl.BlockSpec((n_ctx, LANES), lambda b, h, i: (ctx_blk0 + b, h))
    o_lat = pl.pallas_call(
        functools.partial(_attn_body, (n_seq, n_ctx), 1.0 - lam_init),
        grid=(n_b, DIFF_HEADS, qpb),
        in_specs=[vec(3), vec(3), pl.BlockSpec((ATT_TQ, LANES), lambda b, h, i: (b * qpb + i, h)),
                  lat_kv, lat_kv, ctx_kv, ctx_kv],
        out_specs=pl.BlockSpec((ATT_TQ, LANES), lambda b, h, i: (b * qpb + i, h)),
        out_shape=jax.ShapeDtypeStruct((n_b * n_seq, W_GROUP), BF16),
        scratch_shapes=scratch(ATT_TQ),
        compiler_params=_cparams(("arbitrary", "arbitrary", "arbitrary")),
        name="diff_attn_latent",
    )(lam_vec, sub, qn, kn, vn, kn, vn)
    ckv = pl.BlockSpec((n_ctx, LANES), lambda b, h: (ctx_blk0 + b, h))
    o_ctx = pl.pallas_call(
        functools.partial(_attn_body, (n_ctx,), 1.0 - lam_init),
        grid=(n_b, DIFF_HEADS),
        in_specs=[vec(2), vec(2), ckv, ckv, ckv],
        out_specs=pl.BlockSpec((n_ctx, LANES), lambda b, h: (b, h)),
        out_shape=jax.ShapeDtypeStruct((n_b * n_ctx, W_GROUP), BF16),
        scratch_shapes=scratch(n_ctx),
        compiler_params=_cparams(("arbitrary", "arbitrary")),
        name="diff_attn_context",
    )(lam_vec, sub, qn, kn, vn)
    return o_lat, o_ctx


def _rope_tables(n_seq):
    pos = jnp.arange(n_seq)
    row = (pos // GRID_W).astype(F32)
    col = (pos % GRID_W).astype(F32)
    inv_freq = ROPE_BASE ** (-jnp.arange(ROPE_AX, dtype=F32) / ROPE_AX)
    ang_r, ang_c = row[:, None] * inv_freq, col[:, None] * inv_freq
    z = jnp.zeros_like(ang_r)
    cos = jnp.concatenate([jnp.cos(ang_r)] * 2 + [jnp.cos(ang_c)] * 2, axis=1)
    s1 = jnp.concatenate([z, jnp.sin(ang_r), z, jnp.sin(ang_c)], axis=1)
    s2 = jnp.concatenate([-jnp.sin(ang_r), z, -jnp.sin(ang_c), z], axis=1)

    wide = lambda t: jnp.tile(t, (1, LANES // DIFF_D))
    return wide(cos), wide(s1), wide(s2)


def _rwkv_prep_body(n_b, n_l, n_c,
                    rp_ref, rc_ref, rn_ref, kp_ref, kc_ref, kn_ref, vp_ref, vc_ref, vn_ref, lp_ref, lc_ref, ln_ref,
                    mu_ref, w2_ref, a2_ref, g2_ref, w0_ref, a0_ref, kk_w_ref, ka_ref, rk_ref, e_ref,
                    r_ref, v_ref, kk_ref, lw0_ref, kd0_ref, b0_ref, lw1_ref, kd1_ref, b1_ref, bon_ref, g_ref,
                    buf_ref, bufl_ref):
    t = pl.program_id(0)
    p, n = _seg_pos(t, n_b, n_l, n_c)
    prev_ok = (p > 0).astype(F32)
    next_ok = (p < n - 1).astype(F32)

    def shifted(zp_ref, zc_ref, zn_ref, buf, mu):
        z = zc_ref[...].astype(F32)
        buf[0:HALO, :] = zp_ref[...].astype(F32) * prev_ok
        buf[HALO:HALO + SEQ_TILE, :] = z
        buf[HALO + SEQ_TILE:, :] = zn_ref[...].astype(F32) * next_ok
        return z + mu * (0.5 * (buf[HALO - 1:HALO - 1 + SEQ_TILE, :] + buf[HALO + 1:HALO + 1 + SEQ_TILE, :]) - z)

    r = shifted(rp_ref, rc_ref, rn_ref, buf_ref, mu_ref[:, 0:W_GROUP])
    k = shifted(kp_ref, kc_ref, kn_ref, buf_ref, mu_ref[:, W_GROUP:2 * W_GROUP])
    v = shifted(vp_ref, vc_ref, vn_ref, buf_ref, mu_ref[:, 2 * W_GROUP:3 * W_GROUP])
    low = shifted(lp_ref, lc_ref, ln_ref, bufl_ref, mu_ref[:, 3 * W_GROUP:])
    kk = k * kk_w_ref[...]
    ss = _segsum(kk * kk, e_ref)
    kk = kk / jnp.maximum(jnp.sqrt(ss), 1e-12)
    tw = jnp.tanh(low).astype(BF16)
    lb = low.astype(BF16)
    r_ref[0] = r.astype(r_ref.dtype)
    v_ref[0] = v.astype(v_ref.dtype)
    kk_ref[0] = kk.astype(kk_ref.dtype)
    ksum = jnp.zeros_like(k)
    for d, (lw_ref, kd_ref, b_ref) in enumerate(((lw0_ref, kd0_ref, b0_ref), (lw1_ref, kd1_ref, b1_ref))):
        y = w0_ref[d:d + 1, :] + _dot(tw, w2_ref[d])
        y = -y
        softplus = jnp.maximum(y, 0.0) + jnp.log(1.0 + jnp.exp(-jnp.abs(y)))
        lw_ref[0] = -jnp.exp(-softplus - 0.5)
        ag = jax.nn.sigmoid(a0_ref[d:d + 1, :] + _dot(lb, a2_ref[d]))
        kd = k * (1.0 + (ag - 1.0) * ka_ref[...])
        kd_ref[0] = kd.astype(kd_ref.dtype)
        b_ref[0] = (kk * ag).astype(b_ref.dtype)
        ksum = ksum + kd
    bon_ref[...] = (_segsum(r * ksum * rk_ref[...], e_ref) * v).astype(bon_ref.dtype)
    g_ref[...] = _dot(jax.nn.sigmoid(low).astype(BF16), g2_ref[...]).astype(g_ref.dtype)


def _rwkv_prep(z, p, e_seg, n_b, n_l, n_c):
    rows = z.shape[0]
    vec = pl.BlockSpec((1, W_GROUP), lambda t: (0, 0))
    two = pl.BlockSpec((2, W_GROUP), lambda t: (0, 0))
    lowm = pl.BlockSpec((2, LANES, W_GROUP), lambda t: (0, 0, 0))
    out = pl.BlockSpec((SEQ_TILE, W_GROUP), lambda t: (t, 0))

    def per_batch(t):
        tc = t - n_b * n_l
        is_lat = t < n_b * n_l
        return (jnp.where(is_lat, t // n_l, tc // n_c), jnp.where(is_lat, t % n_l, n_l + tc % n_c), 0)

    out_b = pl.BlockSpec((1, SEQ_TILE, W_GROUP), per_batch)
    t_b = (n_l + n_c) * SEQ_TILE
    z_specs = []
    for col, width in ((COL_R, W_GROUP), (COL_K, W_GROUP), (COL_V, W_GROUP), (COL_LOW, LANES)):
        prv, nxt = _halo_maps(lambda t: t, rows, col)
        z_specs += [pl.BlockSpec((HALO, width), prv),
                    pl.BlockSpec((SEQ_TILE, width), lambda t, col=col: (t, col)),
                    pl.BlockSpec((HALO, width), nxt)]
    return pl.pallas_call(
        functools.partial(_rwkv_prep_body, n_b, n_l, n_c),
        grid=(rows // SEQ_TILE,),
        in_specs=z_specs + [pl.BlockSpec((1, RWKV_COLS), lambda t: (0, 0)), lowm, lowm,
                            pl.BlockSpec((LANES, W_GROUP), lambda t: (0, 0)), two, two, vec, vec, vec,
                            pl.BlockSpec((W_GROUP, W_GROUP), lambda t: (0, 0))],
        out_specs=[out_b] * 9 + [out] * 2,
        out_shape=[jax.ShapeDtypeStruct((n_b, t_b, W_GROUP), F32 if j in (3, 6) else BF16) for j in range(9)]
        + [jax.ShapeDtypeStruct((rows, W_GROUP), BF16)] * 2,
        scratch_shapes=[pltpu.VMEM((SEQ_TILE + 2 * HALO, W_GROUP), F32),
                        pltpu.VMEM((SEQ_TILE + 2 * HALO, LANES), F32)],
        compiler_params=_cparams(("arbitrary",)),
        name="rwkv_prep",
    )(*([z] * 12), p["mu"], p["w2"], p["a2"], p["g2"], p["w0"], p["a0"], p["kk"], p["ka"], p["rk"], e_seg)


def _rwkv_scan_body(*refs):
    c = RW_CHUNK
    in_refs, y_refs, st_ref = refs[:12], refs[12:14], refs[14]
    n_pair = RWKV_HEADS // 2
    n_b = st_ref.shape[0]

    @pl.when(pl.program_id(0) == 0)
    def _():
        st_ref[...] = jnp.zeros_like(st_ref)

    rt = lax.broadcasted_iota(jnp.int32, (c, c), 0)
    ct = lax.broadcasted_iota(jnp.int32, (c, c), 1)
    lane = lax.broadcasted_iota(jnp.int32, (c, LANES), 1)
    head0 = lane < RWKV_N
    rr = lax.broadcasted_iota(jnp.int32, (2 * c, 2 * c), 0)
    cc = lax.broadcasted_iota(jnp.int32, (2 * c, 2 * c), 1)
    same = (rr >= c) == (cc >= c)
    tt = rr & (c - 1)
    ss = cc & (c - 1)

    def stack(x):
        z = jnp.zeros_like(x)
        return jnp.concatenate([jnp.where(head0, x, z), jnp.where(head0, z, x)], axis=0).astype(BF16)

    chains = []
    for bi, d in [(bi, d) for bi in range(n_b) for d in range(2)]:
        rev = d == 1
        r_ref, v_ref, kk_ref, lw_ref, kd_ref, b_ref = in_refs[6 * d:6 * d + 6]
        tri = ((rt <= ct) if rev else (rt >= ct)).astype(BF16)
        lw = lw_ref[bi]
        hi = lw.astype(BF16)
        r1 = lw - hi.astype(F32)
        mid = r1.astype(BF16)
        lo = (r1 - mid.astype(F32)).astype(BF16)
        cum = _dot(tri, hi) + _dot(tri, mid) + _dot(tri, lo)
        last = 0 if rev else c - 1
        tot = cum[last:last + 1, :]
        g_end = jnp.exp(tot - cum)
        g_inv = jnp.exp(-cum)
        full = dict(a=-kk_ref[bi] * jnp.exp(cum - lw), r=r_ref[bi] * jnp.exp(cum), k=kd_ref[bi] * g_inv,
                    b=b_ref[bi] * g_inv, kg=kd_ref[bi] * g_end, bg=b_ref[bi] * g_end, v=v_ref[bi])
        g_tot = jnp.exp(tot)
        strict = same & ((tt < ss) if rev else (tt > ss))
        incl = same & ((tt <= ss) if rev else (tt >= ss))
        for p in range(n_pair):
            sl = slice(p * LANES, (p + 1) * LANES)
            ch = {k: stack(x[:, sl]) for k, x in full.items()}
            ch.update(bi=bi, d=d, p=p, sl=sl, strict=strict, incl=incl, g_tot=g_tot[:, sl])
            chains.append(ch)

    c2 = 2 * c
    mask2 = lambda m, x: jnp.where(jnp.concatenate([m, m], axis=1), x, 0.0).astype(BF16)
    for ch in chains:
        kb = jnp.concatenate([ch["b"], ch["k"]], axis=0)
        sc = mask2(ch["strict"], _dot_nt(ch["a"], kb))
        ch["low"], ch["a_k"] = sc[:, :c2], sc[:, c2:]
        ch["r_bk"] = mask2(ch["incl"], _dot_nt(ch["r"], kb))
    for ch in chains:
        ch["st"] = st_ref[ch["bi"], ch["d"], ch["p"]]
        ch["stb"] = ch["st"].astype(BF16)
        ch["u"] = _dot_nt(ch["a"], ch["stb"]) + _dot(ch["a_k"], ch["v"])
    n_sq = int(math.log2(c)) - 1
    for it in range(n_sq + 1):
        for ch in chains:
            if it < n_sq:
                both = _dot(ch["low"], jnp.concatenate([ch["low"], ch["u"].astype(BF16)], axis=1))
                ch["low"], ch["u"] = both[:, :c2].astype(BF16), ch["u"] + both[:, c2:]
            else:
                ch["u"] = ch["u"] + _dot(ch["low"], ch["u"].astype(BF16))
    for ch in chains:
        ub = ch["u"].astype(BF16)
        uv = jnp.concatenate([ub, ch["v"]], axis=0)
        y = _dot_nt(ch["r"], ch["stb"]) + _dot(ch["r_bk"], uv)
        y_refs[ch["d"]][ch["bi"], :, ch["sl"]] = y[:c] + y[c:]
        st_ref[ch["bi"], ch["d"], ch["p"]] = (ch["st"] * ch["g_tot"]
                                              + _dot_tn(uv, jnp.concatenate([ch["bg"], ch["kg"]], axis=0)))


def _rwkv_scan(pre, n_l, n_c):
    n_b, t_b, _ = pre[0].shape

    def chunk(rev, i):
        ctx = n_l + ((n_c - 1 - i) if rev else i)
        lat = (n_l - 1 - (i - n_c)) if rev else (i - n_c)
        return jnp.where(i < n_c, ctx, lat)

    blk = lambda rev: pl.BlockSpec((n_b, RW_CHUNK, W_GROUP), lambda i, rev=rev: (0, chunk(rev, i), 0))
    args, specs = [], []
    for d in range(2):
        args += [pre[0], pre[1], pre[2], pre[3 + 3 * d], pre[4 + 3 * d], pre[5 + 3 * d]]
        specs += [blk(d == 1)] * 6
    return pl.pallas_call(
        _rwkv_scan_body,
        grid=(n_c + n_l,),
        in_specs=specs,
        out_specs=[blk(False), blk(True)],
        out_shape=[jax.ShapeDtypeStruct((n_b, t_b, W_GROUP), F32)] * 2,
        scratch_shapes=[pltpu.VMEM((n_b, 2, RWKV_HEADS // 2, LANES, LANES), F32)],
        compiler_params=_cparams(("arbitrary",)),
        name="rwkv_scan",
    )(*args)


def _rwkv_finish_body(yf_ref, yb_ref, bon_ref, g_ref, lg_ref, lb_ref, e_ref, o_ref):
    y = yf_ref[0] + yb_ref[0]
    dlt = y - _segsum(y, e_ref) * (1.0 / RWKV_N)
    var = _segsum(dlt * dlt, e_ref) * (1.0 / RWKV_N)
    yn = dlt * lax.rsqrt(var + GN_EPS) * lg_ref[...] + lb_ref[...]
    o_ref[...] = ((yn + bon_ref[...]) * g_ref[...]).astype(BF16)


def _rwkv_finish(yf, yb, bonus, gate, lnx_g, lnx_b, e_seg, n_l, n_c):
    rows = bonus.shape[0]
    n_b = yf.shape[0]
    blk = pl.BlockSpec((SEQ_TILE, W_GROUP), lambda i: (i, 0))
    vec = pl.BlockSpec((1, W_GROUP), lambda i: (0, 0))

    def per_batch(t):
        tc = t - n_b * n_l
        is_lat = t < n_b * n_l
        return (jnp.where(is_lat, t // n_l, tc // n_c), jnp.where(is_lat, t % n_l, n_l + tc % n_c), 0)

    yblk = pl.BlockSpec((1, SEQ_TILE, W_GROUP), per_batch)
    return pl.pallas_call(
        _rwkv_finish_body,
        grid=(rows // SEQ_TILE,),
        in_specs=[yblk, yblk, blk, blk, vec, vec, pl.BlockSpec((W_GROUP, W_GROUP), lambda i: (0, 0))],
        out_specs=blk,
        out_shape=jax.ShapeDtypeStruct((rows, W_GROUP), BF16),
        compiler_params=_cparams(("arbitrary",)),
        name="rwkv_finish",
    )(yf, yb, bonus, gate, lnx_g.reshape(1, W_GROUP), lnx_b.reshape(1, W_GROUP), e_seg)


def _outproj_body(x_ref, ya_ref, yb_ref, yc_ref, yd_ref, w_ref, mod_ref, g2_ref, wrh_ref, wrl_ref, br_ref,
                  xo_ref, h2_ref, lg_ref):
    mix = _dot(ya_ref[...], w_ref[0:W_GROUP, :])
    mix = mix + _dot(yb_ref[...], w_ref[W_GROUP:2 * W_GROUP, :])
    mix = mix + _dot(yc_ref[...], w_ref[2 * W_GROUP:3 * W_GROUP, :])
    mix = mix + _dot(yd_ref[...], w_ref[3 * W_GROUP:, :])
    x = x_ref[...] + mod_ref[0, 2:3, :] * mix
    xo_ref[...] = x
    ms = jnp.mean(x * x, axis=-1, keepdims=True)
    h2 = x * lax.rsqrt(ms + NORM_EPS) * g2_ref[...]
    h2 = h2 * (1.0 + mod_ref[0, 4:5, :]) + mod_ref[0, 3:4, :]
    h2_ref[...] = h2.astype(BF16)
    lg_ref[...] = _dot3(h2, wrh_ref[...], wrl_ref[...]) + br_ref[...]


def _outproj(x, ys, w_out, mod3, g2, wr_hi, wr_lo, br, n_b, n_seq):
    rows = x.shape[0]
    big = pl.BlockSpec((OUT_TILE, D_MODEL), lambda i: (i, 0))
    yb = pl.BlockSpec((OUT_TILE, W_GROUP), lambda i: (i, 0))
    cst = lambda shape: pl.BlockSpec(shape, lambda i: (0, 0))
    return pl.pallas_call(
        _outproj_body,
        grid=(rows // OUT_TILE,),
        in_specs=[big, yb, yb, yb, yb, cst((D_MODEL, D_MODEL)),
                  pl.BlockSpec((1, 6, D_MODEL), lambda i: (_mod_row(i, OUT_TILE, n_b, n_seq), 0, 0)),
                  cst((1, D_MODEL)), cst((D_MODEL, LANES)), cst((D_MODEL, LANES)), cst((1, LANES))],
        out_specs=[big, big, pl.BlockSpec((OUT_TILE, LANES), lambda i: (i, 0))],
        out_shape=[jax.ShapeDtypeStruct((rows, D_MODEL), F32),
                   jax.ShapeDtypeStruct((rows, D_MODEL), BF16),
                   jax.ShapeDtypeStruct((rows, LANES), F32)],
        compiler_params=_cparams(("arbitrary",)),
        name="outproj_norm_router",
    )(x, ys[0], ys[1], ys[2], ys[3], w_out, mod3, g2.reshape(1, D_MODEL), wr_hi, wr_lo, br)


def _expert_body(be_ref, nv_ref, new_ref, x_ref, w1_ref, w3_ref, w2_ref, o_ref, w1b_ref, w3b_ref, w2b_ref):
    i = pl.program_id(0)

    @pl.when(new_ref[i] == 1)
    def _():
        w1b_ref[...] = w1_ref[0, 0].astype(BF16)
        w3b_ref[...] = w3_ref[0, 0].astype(BF16)
        w2b_ref[...] = w2_ref[0, 0].astype(BF16)

    @pl.when(i < nv_ref[0])
    def _():
        x = x_ref[...]
        h1 = _dot(x, w1b_ref[...])
        hid = (h1 * jax.nn.sigmoid(h1)) * _dot(x, w3b_ref[...])
        o_ref[...] = _dot(hid.astype(BF16), w2b_ref[...]).astype(o_ref.dtype)

    @pl.when(i >= nv_ref[0])
    def _():
        o_ref[...] = jnp.zeros_like(o_ref)


def _experts(xb, block_e, n_valid, layer, w1, w3, w2):
    n_blocks = xb.shape[0] // MOE_BM
    is_new = jnp.concatenate([jnp.ones((1,), jnp.int32), (block_e[1:] != block_e[:-1]).astype(jnp.int32)])
    wspec = lambda shape, nbuf=1: pl.BlockSpec((1, 1) + shape, lambda i, be, nv, nw: (layer, be[i], 0, 0),
                                               pipeline_mode=pl.Buffered(nbuf))
    return pl.pallas_call(
        _expert_body,
        grid_spec=pltpu.PrefetchScalarGridSpec(
            num_scalar_prefetch=3,
            grid=(n_blocks,),
            in_specs=[pl.BlockSpec((MOE_BM, D_MODEL), lambda i, be, nv, nw: (i, 0)),
                      wspec((D_MODEL, D_EXPERT)), wspec((D_MODEL, D_EXPERT)), wspec((D_EXPERT, D_MODEL), 2)],
            out_specs=pl.BlockSpec((MOE_BM, D_MODEL), lambda i, be, nv, nw: (i, 0)),
            scratch_shapes=[pltpu.VMEM((D_MODEL, D_EXPERT), BF16), pltpu.VMEM((D_MODEL, D_EXPERT), BF16),
                            pltpu.VMEM((D_EXPERT, D_MODEL), BF16)]),
        out_shape=jax.ShapeDtypeStruct((xb.shape[0], D_MODEL), BF16),
        compiler_params=_cparams(("arbitrary",)),
        name="moe_experts",
    )(block_e, n_valid, is_new, xb, w1, w3, w2)


def _combine_body(x_ref, y0_ref, y1_ref, rt_ref, mod_ref, o_ref):
    rt = rt_ref[...]
    mo = rt[:, RT_W:RT_W + 1] * y0_ref[...].astype(F32) + rt[:, RT_W + 1:RT_W + 2] * y1_ref[...].astype(F32)
    o_ref[...] = x_ref[...] + mod_ref[0, 5:6, :] * mo


def _combine(x, y0, y1, route, mod3, n_b, n_seq, n_out_rows):
    big = pl.BlockSpec((ROW_TILE, D_MODEL), lambda i: (i, 0))
    return pl.pallas_call(
        _combine_body,
        grid=(n_out_rows // ROW_TILE,),
        in_specs=[big, big, big, pl.BlockSpec((ROW_TILE, LANES), lambda i: (i, 0)),
                  pl.BlockSpec((1, 6, D_MODEL), lambda i: (_mod_row(i, ROW_TILE, n_b, n_seq), 0, 0))],
        out_specs=big,
        out_shape=jax.ShapeDtypeStruct((n_out_rows, D_MODEL), F32),
        compiler_params=_cparams(("arbitrary",)),
        name="moe_combine",
    )(x, y0, y1, route, mod3)


RT_E, RT_RANK, RT_W = 0, 2, 4
LOGIT_E0 = N_EGROUPS


def _route_body(lg_ref, rt_ref, cnt_ref, run_ref):
    tm = lg_ref.shape[0]

    @pl.when(pl.program_id(0) == 0)
    def _():
        run_ref[...] = jnp.zeros_like(run_ref)

    x = lg_ref[...]
    lane = lax.broadcasted_iota(jnp.int32, (tm, LANES), 1).astype(F32)
    neg = jnp.float32(-jnp.inf)

    def first_max(v):
        m = jnp.max(v, axis=-1, keepdims=True)
        return m, jnp.min(jnp.where(v == m, lane, float(LANES)), axis=-1, keepdims=True)

    g = jnp.where(lane < N_EGROUPS, x, neg)
    mg, gsel = first_max(g)
    gate_g = 1.0 / jnp.sum(jnp.exp(g - mg), axis=-1, keepdims=True)
    lo = LOGIT_E0 + EXP_PER_GROUP * gsel
    e = jnp.where((lane >= lo) & (lane < lo + EXP_PER_GROUP), x, neg)
    m1, i1 = first_max(e)
    m2, i2 = first_max(jnp.where(lane == i1, neg, e))
    e2 = jnp.exp(m2 - m1)
    w1 = gate_g / (1.0 + e2)
    w2 = gate_g * e2 / (1.0 + e2)
    oh1 = lane == i1
    oh2 = lane == i2
    oh = (oh1 | oh2).astype(BF16)
    rr = lax.broadcasted_iota(jnp.int32, (tm, tm), 0)
    cc = lax.broadcasted_iota(jnp.int32, (tm, tm), 1)
    before = _dot((rr > cc).astype(BF16), oh) + run_ref[0:1, :]
    r1 = jnp.sum(jnp.where(oh1, before, 0.0), axis=-1, keepdims=True)
    r2 = jnp.sum(jnp.where(oh2, before, 0.0), axis=-1, keepdims=True)
    run_ref[...] = run_ref[...] + jnp.sum(oh.astype(F32), axis=0, keepdims=True)
    cnt_ref[...] = run_ref[...]
    rec = jnp.where(lane == RT_E, i1 - LOGIT_E0, 0.0)
    rec = jnp.where(lane == RT_E + 1, i2 - LOGIT_E0, rec)
    rec = jnp.where(lane == RT_RANK, r1, rec)
    rec = jnp.where(lane == RT_RANK + 1, r2, rec)
    rec = jnp.where(lane == RT_W, w1, rec)
    rt_ref[...] = jnp.where(lane == RT_W + 1, w2, rec)


def _route(logits):
    rows = logits.shape[0]
    blk = pl.BlockSpec((ROW_TILE, LANES), lambda i: (i, 0))
    return pl.pallas_call(
        _route_body,
        grid=(rows // ROW_TILE,),
        in_specs=[blk],
        out_specs=[blk, pl.BlockSpec((SUBLANES, LANES), lambda i: (0, 0))],
        out_shape=[jax.ShapeDtypeStruct((rows, LANES), F32), jax.ShapeDtypeStruct((SUBLANES, LANES), F32)],
        scratch_shapes=[pltpu.VMEM((SUBLANES, LANES), F32)],
        compiler_params=_cparams(("arbitrary",)),
        name="moe_route",
    )(logits)


def _moe(x, h2, logits, mod3, layer, w1, w3, w2, n_b, n_seq, n_out_rows):
    rows = x.shape[0]
    route, cnt = _route(logits)
    counts = cnt[0, LOGIT_E0:LOGIT_E0 + N_EXPERTS].astype(jnp.int32)
    pcounts = ((counts + MOE_BM - 1) // MOE_BM) * MOE_BM
    pend = jnp.cumsum(pcounts)
    pstart = pend - pcounts
    eid = route[:, RT_E:RT_E + 2].astype(jnp.int32)
    dest = pstart[eid] + route[:, RT_RANK:RT_RANK + 2].astype(jnp.int32)
    n_blocks = -(-(2 * rows) // MOE_BM) + N_EXPERTS
    tok = jnp.broadcast_to(jnp.arange(rows, dtype=jnp.int32)[:, None], (rows, 2))
    slot_tok = (jnp.arange(n_blocks * MOE_BM, dtype=jnp.int32) % rows).at[dest.reshape(-1)].set(tok.reshape(-1))
    take_rows = lambda a, idx: a.at[idx].get(mode="promise_in_bounds")
    xb = take_rows(h2, slot_tok)
    blk_start = jnp.arange(n_blocks) * MOE_BM
    block_e = jnp.sum(blk_start[:, None] >= pend[None, :], axis=1)
    block_e = jnp.minimum(block_e, N_EXPERTS - 1).astype(jnp.int32)
    n_valid = (pend[-1] // MOE_BM).astype(jnp.int32).reshape(1)
    yb = _experts(xb, block_e, n_valid, layer, w1, w3, w2)
    y0 = take_rows(yb, dest[:n_out_rows, 0])
    y1 = take_rows(yb, dest[:n_out_rows, 1])
    return _combine(x, y0, y1, route, mod3, n_b, n_seq, n_out_rows)


def _pad_rows(w, lo, n):
    pad = [(0, 0)] * (w.ndim - 2) + [(lo, n - lo - w.shape[-2]), (0, 0)]
    return jnp.pad(w.astype(F32), pad).astype(BF16)


def kernel(x, c, ctx, c_ctx, w_ada, b_ada, g_norm1, g_norm2, w_in, w_out, s5_lam_re, s5_lam_im, s5_log_step, s5_b_re, s5_b_im, s5_c_re, s5_c_im, s5_d, s5_w_glu, s5_b_glu, lru_conv_w, lru_conv_b, lru_lam, lru_wa, lru_ba, lru_wx, lru_bx, diff_gq, diff_gk, diff_lq1, diff_lk1, diff_lq2, diff_lk2, diff_subln, rw_mu, rw_w0, rw_w2, rw_a0, rw_a2, rw_g2, rw_kk, rw_ka, rw_rk, rw_lnx_g, rw_lnx_b, moe_w_rg, moe_b_rg, moe_w_re, moe_b_re, moe_w1, moe_w3, moe_w2):
    n_b, n_seq, _ = x.shape
    n_ctx = ctx.shape[1]
    depth = w_ada.shape[0]
    assert n_seq % SEQ_TILE == 0 and n_ctx % SEQ_TILE == 0 and n_seq % n_ctx == 0
    assert (n_b * n_ctx) % ROW_TILE == 0 and n_seq % ROW_TILE == 0 and n_seq % ATT_TK == 0
    xs = jnp.concatenate([x.reshape(n_b * n_seq, D_MODEL), ctx.reshape(n_b * n_ctx, D_MODEL)], axis=0)
    rows = xs.shape[0]
    cvec = jnp.zeros((SUBLANES, D_MODEL), F32).at[0].set(c_ctx).at[1:1 + n_b].set(c)
    rope = _rope_tables(n_seq)
    seg = jnp.arange(W_GROUP) // RWKV_N
    e_seg = (seg[:, None] == seg[None, :]).astype(BF16)
    nl_s, nc_s = n_seq // SEQ_TILE, n_ctx // SEQ_TILE
    nl_r, nc_r = n_seq // RW_CHUNK, n_ctx // RW_CHUNK

    for l in range(depth):
        lam_init = 0.8 - 0.6 * math.exp(-0.3 * l)
        mod = _modulation(cvec, l, w_ada, b_ada[l])
        mod3 = mod[:1 + n_b].reshape(1 + n_b, 6, D_MODEL)
        w_in_b = jnp.pad(w_in[l].astype(BF16), ((0, 0), (0, D_IN_PAD - D_IN)))
        za = _inproj(xs, mod3, g_norm1[l], w_in_b, n_b, n_seq)
        s5p = _s5_prepare(s5_lam_re[l], s5_lam_im[l], s5_log_step[l], s5_b_re[l], s5_b_im[l],
                          s5_c_re[l], s5_c_im[l])
        y5 = [_s5_scan(za, s5p, d, n_b, nl_s, nc_s) for d in range(2)]
        ya = _s5_finish(y5[0], y5[1], za, s5_d[l], s5_w_glu[l], s5_b_glu[l])
        lrp = dict(cw=lru_conv_w[l], cb=lru_conv_b[l].reshape(1, W_GROUP),
                   wa=_blockdiag(lru_wa[l]).astype(BF16), wx=_blockdiag(lru_wx[l]).astype(BF16),
                   ba=lru_ba[l].reshape(2, 1, W_GROUP), bx=lru_bx[l].reshape(2, 1, W_GROUP),
                   sp=jax.nn.softplus(-lru_lam[l]).reshape(2, 1, W_GROUP))
        hl = [_lru_scan(za, lrp, d, n_b, nl_s, nc_s) for d in range(2)]
        yb_ = _lru_finish(hl[0], hl[1], za)
        lam = jnp.exp(jnp.sum(diff_lq1[l] * diff_lk1[l])) - jnp.exp(jnp.sum(diff_lq2[l] * diff_lk2[l])) + lam_init
        lam_vec = jnp.full((1, LANES), lam, F32)
        qn, kn, vn = _qkv_prep(za, rope, diff_gq[l], diff_gk[l], e_seg, n_b, n_seq)
        o_lat, o_ctx = _attention(qn, kn, vn, lam_vec, diff_subln[l], lam_init, n_b, n_seq, n_ctx)
        yc = jnp.concatenate([o_lat, o_ctx], axis=0)
        lo_w, lo_a = RANK_G, RANK_G + RANK_W
        rwp = dict(mu=rw_mu[l].reshape(1, RWKV_COLS),
                   w2=_pad_rows(rw_w2[l], lo_w, LANES), a2=_pad_rows(rw_a2[l], lo_a, LANES),
                   g2=_pad_rows(rw_g2[l], 0, LANES), w0=rw_w0[l], a0=rw_a0[l],
                   kk=rw_kk[l].reshape(1, W_GROUP), ka=rw_ka[l].reshape(1, W_GROUP),
                   rk=rw_rk[l].reshape(1, W_GROUP))
        pre = _rwkv_prep(za, rwp, e_seg, n_b, nl_s, nc_s)
        yr = _rwkv_scan(pre, nl_r, nc_r)
        yd = _rwkv_finish(yr[0], yr[1], pre[9], pre[10], rw_lnx_g[l], rw_lnx_b[l], e_seg, nl_s, nc_s)
        wr = jnp.zeros((D_MODEL, LANES), F32).at[:, :N_EGROUPS].set(moe_w_rg[l])
        wr = wr.at[:, N_EGROUPS:N_EGROUPS + N_EXPERTS].set(moe_w_re[l])
        wr_hi, wr_lo = _split2(wr)
        br = jnp.zeros((1, LANES), F32).at[0, :N_EGROUPS].set(moe_b_rg[l])
        br = br.at[0, N_EGROUPS:N_EGROUPS + N_EXPERTS].set(moe_b_re[l])
        xs, h2, logits = _outproj(xs, (ya, yb_, yc, yd), w_out[l].astype(BF16), mod3, g_norm2[l],
                                  wr_hi, wr_lo, br, n_b, n_seq)
        n_out_rows = rows if l < depth - 1 else n_b * n_seq
        xs = _moe(xs, h2, logits, mod3, l, moe_w1, moe_w3, moe_w2, n_b, n_seq, n_out_rows)
    return xs.reshape(n_b, n_seq, D_MODEL)
```

```python
import functools
import math

import jax
import jax.numpy as jnp
from jax import lax
from jax.experimental import pallas as pl
from jax.experimental.pallas import tpu as pltpu

F32 = jnp.float32
BF16 = jnp.bfloat16

D_MODEL = 2048
W_GROUP = 512
S5_CH, S5_GROUPS, S5_STATE = 16, 32, 64
S5_NS = S5_GROUPS * S5_STATE
LRU_C = 8.0
DIFF_HEADS, DIFF_D = 4, 64
ROPE_AX = 16
ROPE_BASE = 10000.0
RWKV_N, RWKV_HEADS = 64, 8
RANK_G, RANK_W, RANK_A = 64, 32, 32
RWKV_COLS = 3 * W_GROUP + RANK_G + RANK_W + RANK_A
N_EGROUPS, EXP_PER_GROUP, N_EXPERTS = 4, 8, 32
D_EXPERT = 1024
NORM_EPS = 1e-6
GN_EPS = 64e-5
GRID_W = 64

SUBLANES = 8
LANES = 128
ROW_TILE = 512
SEQ_TILE = 256
RW_CHUNK = 64
ATT_TQ = 256
ATT_UNROLL = 32
ATT_TK = 512
ATT_RB = 128
MOE_BM = 256
OUT_TILE = 256
IN_TN = 1024
D_IN = 6 * W_GROUP + RWKV_COLS
D_IN_PAD = -(-D_IN // IN_TN) * IN_TN
COL_R, COL_K, COL_V = 6, 7, 8
COL_LOW = (9 * W_GROUP) // LANES
VMEM_LIMIT = 56 * 1024 * 1024


def _cparams(sem):
    return pltpu.CompilerParams(dimension_semantics=sem, vmem_limit_bytes=VMEM_LIMIT)


def _split2(x):
    hi = x.astype(BF16)
    lo = (x - hi.astype(F32)).astype(BF16)
    return hi, lo


def _dot(a, b):
    return jnp.dot(a, b, preferred_element_type=F32)


def _dot_nt(a, b):
    return lax.dot_general(a, b, (((1,), (1,)), ((), ())), preferred_element_type=F32)


def _dot_tn(a, b):
    return lax.dot_general(a, b, (((0,), (0,)), ((), ())), preferred_element_type=F32)


def _dot3(a, b_hi, b_lo):
    a_hi, a_lo = _split2(a)
    return _dot(a_hi, b_hi) + _dot(a_hi, b_lo) + _dot(a_lo, b_hi)


def _segsum(x, e_ref):
    hi, lo = _split2(x)
    e = e_ref[...]
    return _dot(hi, e) + _dot(lo, e)


def _seq_block(rev, b, i, n_c, n_l, n_b):
    if rev:
        ctx = n_b * n_l + b * n_c + (n_c - 1 - i)
        lat = b * n_l + (n_l - 1 - (i - n_c))
    else:
        ctx = n_b * n_l + b * n_c + i
        lat = b * n_l + (i - n_c)
    return jnp.where(i < n_c, ctx, lat)


def _seg_pos(t, n_b, n_l, n_c):
    is_lat = t < n_b * n_l
    p = jnp.where(is_lat, t % n_l, (t - n_b * n_l) % n_c)
    n = jnp.where(is_lat, n_l, n_c)
    return p, n


def _mod_body(c_ref, w_ref, b_ref, o_ref):
    c = c_ref[...]
    s = c * jax.nn.sigmoid(c)
    w_hi, w_lo = _split2(w_ref[0])
    o_ref[...] = _dot3(s, w_hi, w_lo) + b_ref[...]


def _modulation(cvec, layer, w_ada, b_ada):
    d6 = w_ada.shape[2]
    tn = 1536
    return pl.pallas_call(
        _mod_body,
        grid=(d6 // tn,),
        in_specs=[pl.BlockSpec((SUBLANES, D_MODEL), lambda j: (0, 0)),
                  pl.BlockSpec((1, D_MODEL, tn), lambda j: (layer, 0, j)),
                  pl.BlockSpec((1, tn), lambda j: (0, j))],
        out_specs=pl.BlockSpec((SUBLANES, tn), lambda j: (0, j)),
        out_shape=jax.ShapeDtypeStruct((SUBLANES, d6), F32),
        compiler_params=_cparams(("arbitrary",)),
        name="adaln_mod",
    )(cvec, w_ada, b_ada.reshape(1, d6))


def _mod_row(i, tile, n_b, n_seq):
    return jnp.where(i < n_b * n_seq // tile, 1 + i // (n_seq // tile), 0)


def _inproj_body(x_ref, mod_ref, g_ref, w_ref, z_ref, h_ref):
    @pl.when(pl.program_id(1) == 0)
    def _():
        x = x_ref[...]
        ms = jnp.mean(x * x, axis=-1, keepdims=True)
        xn = x * lax.rsqrt(ms + NORM_EPS) * g_ref[...]
        h_ref[...] = (xn * (1.0 + mod_ref[0, 1:2, :]) + mod_ref[0, 0:1, :]).astype(BF16)

    z_ref[...] = _dot(h_ref[...], w_ref[...]).astype(z_ref.dtype)


def _inproj(x, mod3, g1, w, n_b, n_seq):
    rows = x.shape[0]
    tn = IN_TN
    return pl.pallas_call(
        _inproj_body,
        grid=(rows // ROW_TILE, w.shape[1] // tn),
        in_specs=[pl.BlockSpec((ROW_TILE, D_MODEL), lambda i, j: (i, 0)),
                  pl.BlockSpec((1, 6, D_MODEL), lambda i, j: (_mod_row(i, ROW_TILE, n_b, n_seq), 0, 0)),
                  pl.BlockSpec((1, D_MODEL), lambda i, j: (0, 0)),
                  pl.BlockSpec((D_MODEL, tn), lambda i, j: (0, j))],
        out_specs=pl.BlockSpec((ROW_TILE, tn), lambda i, j: (i, j)),
        out_shape=jax.ShapeDtypeStruct((rows, w.shape[1]), BF16),
        scratch_shapes=[pltpu.VMEM((ROW_TILE, D_MODEL), BF16)],
        compiler_params=_cparams(("arbitrary", "arbitrary")),
        name="norm_inproj",
    )(x, mod3, g1.reshape(1, D_MODEL), w)


S5_SEG = SEQ_TILE // SUBLANES
S5_HALF = 2


def _s5_body(rev, n_b, *refs):
    u_refs = refs[:n_b]
    (bre_ref, bim_ref, dec_ref, dseg_ref, t3_ref, pw_ref, cre_ref, cim_ref,
     y_ref, sr_ref, si_ref, car_ref, loc_ref) = refs[n_b:]
    n_k = S5_SEG
    hw, hs = W_GROUP // S5_HALF, S5_NS // S5_HALF
    batches = range(n_b)

    @pl.when(pl.program_id(0) == 0)
    def _():
        car_ref[...] = jnp.zeros_like(car_ref)

    ri = lax.broadcasted_iota(jnp.int32, (SEQ_TILE, SEQ_TILE), 0)
    ci = lax.broadcasted_iota(jnp.int32, (SEQ_TILE, SEQ_TILE), 1)
    lg8, lgk = int(math.log2(SUBLANES)), int(math.log2(n_k))
    perm = (ci == ((ri & (SUBLANES - 1)) << lgk) + (ri >> lg8)).astype(BF16)
    for b in batches:
        up = _dot(perm, u_refs[b][...].astype(BF16)).astype(BF16)
        for h in range(S5_HALF):
            uh = up[:, h * hw:(h + 1) * hw]
            sr_ref[b, :, :, h * hs:(h + 1) * hs] = _dot(uh, bre_ref[0, h]).reshape(n_k, SUBLANES, hs)
            si_ref[b, :, :, h * hs:(h + 1) * hs] = _dot(uh, bim_ref[0, h]).reshape(n_k, SUBLANES, hs)

    loc_ref[...] = jnp.zeros_like(loc_ref)
    ar, ai = dec_ref[0, 0], dec_ref[0, 1]

    def step(k, _):
        kk = (n_k - 1 - k) if rev else k
        for b in batches:
            cr, ci_ = loc_ref[b, 0], loc_ref[b, 1]
            nr = ar * cr - ai * ci_ + sr_ref[b, kk]
            ni = ar * ci_ + ai * cr + si_ref[b, kk]
            sr_ref[b, kk] = nr
            si_ref[b, kk] = ni
            loc_ref[b, 0] = nr
            loc_ref[b, 1] = ni
        return 0

    lax.fori_loop(0, n_k, step, 0)

    sub = lax.broadcasted_iota(jnp.int32, (SUBLANES, S5_NS), 0)
    first = (SUBLANES - 1) if rev else 0
    sh1 = (SUBLANES - 1) if rev else 1
    last = 0 if rev else SUBLANES - 1
    dr, di = dseg_ref[0, 0], dseg_ref[0, 1]
    enter = []
    for b in batches:
        lr, li = loc_ref[b, 0], loc_ref[b, 1]
        er = jnp.where(sub == first, car_ref[b, 0], pltpu.roll(lr, sh1, axis=0))
        ei = jnp.where(sub == first, car_ref[b, 1], pltpu.roll(li, sh1, axis=0))
        for lvl, s in enumerate((1, 2, 4)):
            sh = (SUBLANES - s) if rev else s
            pr, pi = pltpu.roll(er, sh, axis=0), pltpu.roll(ei, sh, axis=0)
            tr, ti = t3_ref[0, 0, lvl], t3_ref[0, 1, lvl]
            er, ei = er + tr * pr - ti * pi, ei + tr * pi + ti * pr
        fr = lr + dr * er - di * ei
        fi = li + dr * ei + di * er
        car_ref[b, 0] = jnp.broadcast_to(fr[last:last + 1, :], (SUBLANES, S5_NS))
        car_ref[b, 1] = jnp.broadcast_to(fi[last:last + 1, :], (SUBLANES, S5_NS))
        enter.append((er, ei))

    pr, pi = pw_ref[0, 0], pw_ref[0, 1]
    inv = (ci == ((ri & (n_k - 1)) << lg8) + (ri >> lgk)).astype(BF16)
    for b in batches:
        er, ei = enter[b]
        s_r = (sr_ref[b] + pr * er[None] - pi * ei[None]).reshape(SEQ_TILE, S5_NS).astype(BF16)
        s_i = (si_ref[b] + pr * ei[None] + pi * er[None]).reshape(SEQ_TILE, S5_NS).astype(BF16)
        ys = []
        for h in range(S5_HALF):
            cols = slice(h * hs, (h + 1) * hs)
            ys.append(_dot(s_r[:, cols], cre_ref[0, h]) - _dot(s_i[:, cols], cim_ref[0, h]))
        y_hi, y_lo = _split2(jnp.concatenate(ys, axis=1))
        y_ref[b] = _dot(inv, y_hi) + _dot(inv, y_lo)


def _per_batch_tile(t, n_b, n_l, n_c):
    tc = t - n_b * n_l
    is_lat = t < n_b * n_l
    return (jnp.where(is_lat, t // n_l, tc // n_c), jnp.where(is_lat, t % n_l, n_l + tc % n_c), 0)


def _s5_scan(za, p, d, n_b, n_l, n_c):
    rev = d == 1
    cst = lambda nd: (lambda i: (d,) + (0,) * (nd - 1))
    hw, hs = W_GROUP // S5_HALF, S5_NS // S5_HALF
    state = pltpu.VMEM((n_b, S5_SEG, SUBLANES, S5_NS), F32)
    pair = pltpu.VMEM((n_b, 2, SUBLANES, S5_NS), F32)

    def local(i):
        ctx = n_l + ((n_c - 1 - i) if rev else i)
        lat = (n_l - 1 - (i - n_c)) if rev else (i - n_c)
        return jnp.where(i < n_c, ctx, lat)

    u_specs = [pl.BlockSpec((SEQ_TILE, W_GROUP), lambda i, b=b: (_seq_block(rev, b, i, n_c, n_l, n_b), 0))
               for b in range(n_b)]
    return pl.pallas_call(
        functools.partial(_s5_body, rev, n_b),
        grid=(n_c + n_l,),
        in_specs=u_specs + [
                  pl.BlockSpec((1, S5_HALF, hw, hs), cst(4)),
                  pl.BlockSpec((1, S5_HALF, hw, hs), cst(4)),
                  pl.BlockSpec((1, 2, SUBLANES, S5_NS), cst(4)),
                  pl.BlockSpec((1, 2, SUBLANES, S5_NS), cst(4)),
                  pl.BlockSpec((1, 2, 3, SUBLANES, S5_NS), cst(5)),
                  pl.BlockSpec((1, 2, S5_SEG, SUBLANES, S5_NS), cst(5)),
                  pl.BlockSpec((1, S5_HALF, hs, hw), cst(4)),
                  pl.BlockSpec((1, S5_HALF, hs, hw), cst(4))],
        out_specs=pl.BlockSpec((n_b, SEQ_TILE, W_GROUP), lambda i: (0, local(i), 0)),
        out_shape=jax.ShapeDtypeStruct((n_b, (n_l + n_c) * SEQ_TILE, W_GROUP), F32),
        scratch_shapes=[state, state, pair, pair],
        compiler_params=_cparams(("arbitrary",)),
        name="s5_scan_rev" if rev else "s5_scan_fwd",
    )(*([za] * n_b), p["bre"], p["bim"], p["dec"], p["dseg"], p["t3"], p["pw"], p["cre"], p["cim"])


def _s5_prepare(lam_re, lam_im, log_step, b_re, b_im, c_re, c_im):
    lr, li = lam_re.astype(F32), lam_im.astype(F32)
    dt = jnp.exp(log_step.astype(F32))[..., None]
    mag = jnp.exp(lr * dt)
    ar, ai = mag * jnp.cos(li * dt), mag * jnp.sin(li * dt)
    den = lr * lr + li * li
    cr = ((ar - 1.0) * lr + ai * li) / den
    ci = (ai * lr - (ar - 1.0) * li) / den
    bbr = cr[..., None] * b_re - ci[..., None] * b_im
    bbi = cr[..., None] * b_im + ci[..., None] * b_re
    gh = S5_GROUPS // S5_HALF
    eye = jnp.eye(gh, dtype=F32)

    def bd(t):
        t = t.reshape(2, S5_HALF, gh, S5_STATE, S5_CH)
        return jnp.einsum('dxgph,gk->dxghkp', t, eye).reshape(2, S5_HALF, gh * S5_CH, gh * S5_STATE).astype(BF16)

    def cd(t):
        t = t.astype(F32).reshape(2, S5_HALF, gh, S5_CH, S5_STATE)
        return jnp.einsum('dxghp,gk->dxgpkh', t, eye).reshape(2, S5_HALF, gh * S5_STATE, gh * S5_CH).astype(BF16)

    cmul = lambda x, y: (x[0] * y[0] - x[1] * y[1], x[0] * y[1] + x[1] * y[0])
    dec = (ar.reshape(2, S5_NS), ai.reshape(2, S5_NS))
    pw = [dec]
    for _ in range(S5_SEG - 1):
        pw.append(cmul(pw[-1], dec))
    seg = [pw[-1]]
    for _ in range(2):
        seg.append(cmul(seg[-1], seg[-1]))
    wide = lambda t: jnp.broadcast_to(t[:, None, :], (2, SUBLANES, S5_NS))
    pair = lambda z: jnp.stack([wide(z[0]), wide(z[1])], axis=1)
    k = jnp.arange(SUBLANES)
    t3 = []
    for j, s in enumerate((1, 2, 4)):
        m = jnp.stack([k >= s, k + s <= SUBLANES - 1])
        t3.append(jnp.where(m[:, None, :, None], pair(seg[j]), 0.0))
    t3 = jnp.stack(t3, axis=2)
    pw_f = jnp.stack([jnp.stack([wide(p[0])[0], wide(p[1])[0]]) for p in pw], axis=1)
    pw_b = jnp.stack([jnp.stack([wide(p[0])[1], wide(p[1])[1]]) for p in reversed(pw)], axis=1)
    return dict(bre=bd(bbr), bim=bd(bbi), cre=cd(c_re), cim=cd(c_im), dec=pair(dec), dseg=pair(seg[0]),
                t3=t3, pw=jnp.stack([pw_f, pw_b]))


def _s5_finish_body(yf_ref, yb_ref, u_ref, d_ref, w_ref, b_ref, o_ref):
    y = jax.nn.gelu(yf_ref[0] + yb_ref[0] + d_ref[...] * u_ref[...].astype(F32))
    gate = jax.nn.sigmoid(_dot(y.astype(BF16), w_ref[...]) + b_ref[...])
    o_ref[...] = (y * gate).astype(BF16)


def _s5_finish(yf, yb, za, d_skip, w_glu, b_glu, n_l, n_c):
    rows = za.shape[0]
    n_b = yf.shape[0]
    blk = pl.BlockSpec((SEQ_TILE, W_GROUP), lambda i: (i, 0))
    yblk = pl.BlockSpec((1, SEQ_TILE, W_GROUP), lambda i: _per_batch_tile(i, n_b, n_l, n_c))
    vec = pl.BlockSpec((1, W_GROUP), lambda i: (0, 0))
    return pl.pallas_call(
        _s5_finish_body,
        grid=(rows // SEQ_TILE,),
        in_specs=[yblk, yblk, blk, vec, pl.BlockSpec((W_GROUP, W_GROUP), lambda i: (0, 0)), vec],
        out_specs=blk,
        out_shape=jax.ShapeDtypeStruct((rows, W_GROUP), BF16),
        compiler_params=_cparams(("arbitrary",)),
        name="s5_finish",
    )(yf, yb, za, d_skip.reshape(1, W_GROUP), w_glu.astype(BF16), b_glu.reshape(1, W_GROUP))


HALO = 2 * SUBLANES


def _halo_maps(tile_fn, rows, col):
    per = SEQ_TILE // HALO
    last = rows // HALO - 1
    prv = lambda *ids: (jnp.maximum(tile_fn(*ids) * per - 1, 0), col)
    nxt = lambda *ids: (jnp.minimum((tile_fn(*ids) + 1) * per, last), col)
    return prv, nxt


def _lru_body(rev, n_b, n_l, n_c, xp_ref, xc_ref, xn_ref, cw_ref, cb_ref, wa_ref, ba_ref, wx_ref, bx_ref,
              sp_ref, h_ref, buf_ref, a_ref, b_ref, car_ref):
    g_n = SEQ_TILE // SUBLANES
    b = pl.program_id(0)
    i = pl.program_id(1)

    @pl.when(i == 0)
    def _():
        car_ref[...] = jnp.zeros_like(car_ref)

    p, n = _seg_pos(_seq_block(rev, b, i, n_c, n_l, n_b), n_b, n_l, n_c)
    prev_ok = (p > 0).astype(F32)
    next_ok = (p < n - 1).astype(F32)
    xc = xc_ref[...].astype(F32)
    buf_ref[0:HALO, :] = xp_ref[...].astype(F32) * prev_ok
    buf_ref[HALO:HALO + SEQ_TILE, :] = xc
    buf_ref[HALO + SEQ_TILE:, :] = xn_ref[...].astype(F32) * next_ok
    v = cb_ref[...] + cw_ref[2:3, :] * xc
    v = v + cw_ref[0:1, :] * buf_ref[HALO - 2:HALO - 2 + SEQ_TILE, :]
    v = v + cw_ref[1:2, :] * buf_ref[HALO - 1:HALO - 1 + SEQ_TILE, :]
    v = v + cw_ref[3:4, :] * buf_ref[HALO + 1:HALO + 1 + SEQ_TILE, :]
    vb = v.astype(BF16)
    r = jax.nn.sigmoid(_dot(vb, wa_ref[0]) + ba_ref[0])
    ig = jax.nn.sigmoid(_dot(vb, wx_ref[0]) + bx_ref[0])
    a = jnp.exp(-LRU_C * r * sp_ref[0])
    bb = jnp.sqrt(1.0 - a * a) * (ig * v)
    a3 = a.reshape(g_n, SUBLANES, W_GROUP)
    b3 = bb.reshape(g_n, SUBLANES, W_GROUP)
    k = lax.broadcasted_iota(jnp.int32, (g_n, SUBLANES, W_GROUP), 1)
    for s in (1, 2, 4):
        sh = (SUBLANES - s) if rev else s
        m = (k + s <= SUBLANES - 1) if rev else (k >= s)
        a_s = pltpu.roll(a3, sh, axis=1)
        b_s = pltpu.roll(b3, sh, axis=1)
        b3 = jnp.where(m, a3 * b_s + b3, b3)
        a3 = jnp.where(m, a3 * a_s, a3)
    a_ref[...] = a3
    b_ref[...] = b3
    row = 0 if rev else SUBLANES - 1

    def step(g, _):
        gg = (g_n - 1 - g) if rev else g
        hh = b_ref[gg] + a_ref[gg] * car_ref[...]
        b_ref[gg] = hh
        car_ref[...] = jnp.broadcast_to(hh[row:row + 1, :], (SUBLANES, W_GROUP))
        return 0

    lax.fori_loop(0, g_n, step, 0)
    h_ref[...] = b_ref[...].reshape(SEQ_TILE, W_GROUP)


def _lru_scan(za, p, d, n_b, n_l, n_c):
    rev = d == 1
    rows = za.shape[0]
    tile = lambda b, i: _seq_block(rev, b, i, n_c, n_l, n_b)
    cur = lambda b, i: (tile(b, i), 1)
    prv, nxt = _halo_maps(tile, rows, 1)
    out = lambda b, i: (tile(b, i), 0)
    vec = pl.BlockSpec((1, W_GROUP), lambda b, i: (0, 0))
    dvec = pl.BlockSpec((1, 1, W_GROUP), lambda b, i: (d, 0, 0))
    dmat = pl.BlockSpec((1, W_GROUP, W_GROUP), lambda b, i: (d, 0, 0))
    blk = lambda f: pl.BlockSpec((SEQ_TILE, W_GROUP), f)
    halo = lambda f: pl.BlockSpec((HALO, W_GROUP), f)
    g_n = SEQ_TILE // SUBLANES
    return pl.pallas_call(
        functools.partial(_lru_body, rev, n_b, n_l, n_c),
        grid=(n_b, n_c + n_l),
        in_specs=[halo(prv), blk(cur), halo(nxt),
                  pl.BlockSpec((4, W_GROUP), lambda b, i: (0, 0)), vec,
                  dmat, dvec, dmat, dvec, dvec],
        out_specs=blk(out),
        out_shape=jax.ShapeDtypeStruct((rows, W_GROUP), F32),
        scratch_shapes=[pltpu.VMEM((SEQ_TILE + 2 * HALO, W_GROUP), F32),
                        pltpu.VMEM((g_n, SUBLANES, W_GROUP), F32),
                        pltpu.VMEM((g_n, SUBLANES, W_GROUP), F32),
                        pltpu.VMEM((SUBLANES, W_GROUP), F32)],
        compiler_params=_cparams(("arbitrary", "arbitrary")),
        name="lru_scan_rev" if rev else "lru_scan_fwd",
    )(za, za, za, p["cw"], p["cb"], p["wa"], p["ba"], p["wx"], p["bx"], p["sp"])


def _blockdiag(w):
    nb, c = w.shape[1], w.shape[2]
    eye = jnp.eye(nb, dtype=F32)
    return jnp.einsum('dncf,nm->dncmf', w.astype(F32), eye).reshape(2, nb * c, nb * c)


def _lru_finish_body(hf_ref, hb_ref, g_ref, o_ref):
    o_ref[...] = ((hf_ref[...] + hb_ref[...]) * jax.nn.gelu(g_ref[...].astype(F32))).astype(BF16)


def _lru_finish(hf, hb, za):
    rows = hf.shape[0]
    blk = pl.BlockSpec((ROW_TILE, W_GROUP), lambda i: (i, 0))
    return pl.pallas_call(
        _lru_finish_body,
        grid=(rows // ROW_TILE,),
        in_specs=[blk, blk, pl.BlockSpec((ROW_TILE, W_GROUP), lambda i: (i, 2))],
        out_specs=blk,
        out_shape=jax.ShapeDtypeStruct((rows, W_GROUP), BF16),
        compiler_params=_cparams(("arbitrary",)),
        name="lru_finish",
    )(hf, hb, za)


def _qkv_prep_body(n_lat_tiles, q_ref, k_ref, v_ref, cos_ref, s1_ref, s2_ref, gq_ref, gk_ref, e_ref,
                   qo_ref, ko_ref, vo_ref):
    reps = W_GROUP // LANES
    is_ctx = pl.program_id(0) >= n_lat_tiles
    cos = jnp.concatenate([jnp.where(is_ctx, 1.0, cos_ref[...])] * reps, axis=1)
    s1 = jnp.concatenate([jnp.where(is_ctx, 0.0, s1_ref[...])] * reps, axis=1)
    s2 = jnp.concatenate([jnp.where(is_ctx, 0.0, s2_ref[...])] * reps, axis=1)

    def prep(x, g):
        ms = _segsum(x * x, e_ref) * (1.0 / DIFF_D)
        x = x * lax.rsqrt(ms + NORM_EPS) * g
        return x * cos + pltpu.roll(x, ROPE_AX, axis=1) * s1 + pltpu.roll(x, W_GROUP - ROPE_AX, axis=1) * s2

    qo_ref[...] = (prep(q_ref[...].astype(F32), gq_ref[...]) * (DIFF_D ** -0.5)).astype(BF16)
    ko_ref[...] = prep(k_ref[...].astype(F32), gk_ref[...]).astype(BF16)
    vo_ref[...] = v_ref[...].astype(BF16)


def _qkv_prep(za, rope, gq, gk, e_seg, n_b, n_seq):
    rows = za.shape[0]
    per_batch = n_seq // ROW_TILE
    col = lambda c: pl.BlockSpec((ROW_TILE, W_GROUP), lambda i: (i, c))
    tab = pl.BlockSpec((ROW_TILE, LANES), lambda i: (i % per_batch, 0))
    vec = pl.BlockSpec((1, W_GROUP), lambda i: (0, 0))
    out = pl.BlockSpec((ROW_TILE, W_GROUP), lambda i: (i, 0))
    reps = W_GROUP // DIFF_D
    return pl.pallas_call(
        functools.partial(_qkv_prep_body, n_b * per_batch),
        grid=(rows // ROW_TILE,),
        in_specs=[col(3), col(4), col(5), tab, tab, tab, vec, vec,
                  pl.BlockSpec((W_GROUP, W_GROUP), lambda i: (0, 0))],
        out_specs=[out, out, out],
        out_shape=[jax.ShapeDtypeStruct((rows, W_GROUP), BF16)] * 3,
        compiler_params=_cparams(("arbitrary",)),
        name="attn_qkv_prep",
    )(za, za, za, rope[0], rope[1], rope[2], jnp.tile(gq, reps).reshape(1, W_GROUP),
      jnp.tile(gk, reps).reshape(1, W_GROUP), e_seg)


def _attn_body(seg_lens, out_scale, lam_ref, sub_ref, q_ref, *rest):
    o_ref, q2_ref, m_ref, acc_ref, sa_ref, sb_ref = rest[-6:]
    kv = rest[:-6]
    tq = q_ref.shape[0]
    q = q_ref[...]
    lane = lax.broadcasted_iota(jnp.int32, (tq, LANES), 1)
    zero = jnp.zeros_like(q)
    q2 = jnp.concatenate([jnp.where(lane < DIFF_D, q, zero), jnp.where(lane < DIFF_D, zero, q)], axis=0)
    q2_ref[...] = q2
    m_ref[...] = jnp.full_like(m_ref, -jnp.inf)
    acc_ref[...] = jnp.zeros_like(acc_ref)
    n_rb = (2 * tq) // ATT_RB

    rows = [slice(rb * ATT_RB, (rb + 1) * ATT_RB) for rb in range(n_rb)]

    def scores(kc, s_ref):
        for rs in rows:
            s_ref[rs, 0:kc.shape[0]] = _dot_nt(q2_ref[rs, :], kc)

    def softmax_pv(s_ref, vc):
        tk = vc.shape[0]
        v_ext = jnp.concatenate([vc, jnp.ones_like(vc)], axis=1)
        for rs in rows:
            s = s_ref[rs, 0:tk]
            m_old = m_ref[rs, :]
            m_new = jnp.maximum(m_old, jnp.max(s, axis=-1, keepdims=True))
            alpha = jnp.exp(m_old - m_new)
            pm = jnp.exp((s - jnp.concatenate([m_new] * (tk // LANES), axis=1)).astype(BF16))
            acc_ref[rs, :] = jnp.concatenate([alpha, alpha], axis=1) * acc_ref[rs, :] + _dot(pm, v_ext)
            m_ref[rs, :] = m_new

    for si, n_k in enumerate(seg_lens):
        k_ref, v_ref = kv[2 * si], kv[2 * si + 1]
        tk = min(ATT_TK, n_k)
        n_ch = n_k // tk
        if n_ch < 2:
            for j in range(n_ch):
                scores(k_ref[j * tk:(j + 1) * tk, :], sa_ref)
                softmax_pv(sa_ref, v_ref[j * tk:(j + 1) * tk, :])
            continue
        unroll = min(ATT_UNROLL, n_ch)
        assert unroll % 2 == 0 and n_ch % unroll == 0
        chunk = lambda ref, j, tk=tk: ref[pl.ds(pl.multiple_of(j * tk, tk), tk), :]
        scores(chunk(k_ref, 0), sa_ref)

        def body(jj, _, k_ref=k_ref, v_ref=v_ref, n_ch=n_ch, unroll=unroll):
            for u in range(unroll):
                j = unroll * jj + u
                cur, nxt = (sa_ref, sb_ref) if u % 2 == 0 else (sb_ref, sa_ref)
                scores(chunk(k_ref, jnp.minimum(j + 1, n_ch - 1)), nxt)
                softmax_pv(cur, chunk(v_ref, j))
            return 0

        lax.fori_loop(0, n_ch // unroll, body, 0)

    o = acc_ref[:, 0:LANES] / acc_ref[:, LANES:]
    o = o[:tq] - lam_ref[...] * o[tq:]
    ms = jnp.mean(o * o, axis=-1, keepdims=True)
    o_ref[...] = (o * lax.rsqrt(ms + NORM_EPS) * sub_ref[...] * out_scale).astype(BF16)


def _attention(qn, kn, vn, lam_vec, subln, lam_init, n_b, n_seq, n_ctx):
    vec = lambda nd: pl.BlockSpec((1, LANES), (lambda b, h, i: (0, 0)) if nd == 3 else (lambda b, h: (0, 0)))
    scratch = lambda tq: [pltpu.VMEM((2 * tq, LANES), BF16), pltpu.VMEM((2 * tq, LANES), F32),
                          pltpu.VMEM((2 * tq, 2 * LANES), F32),
                          pltpu.VMEM((2 * tq, ATT_TK), F32), pltpu.VMEM((2 * tq, ATT_TK), F32)]
    sub = subln.reshape(1, LANES)
    ctx_blk0 = n_b * n_seq // n_ctx
    qpb = n_seq // ATT_TQ
    lat_kv = pl.BlockSpec((n_seq, LANES), lambda b, h, i: (b, h))
    ctx_kv = pl.BlockSpec((n_ctx, LANES), lambda b, h, i: (ctx_blk0 + b, h))
    o_lat = pl.pallas_call(
        functools.partial(_attn_body, (n_seq, n_ctx), 1.0 - lam_init),
        grid=(n_b, DIFF_HEADS, qpb),
        in_specs=[vec(3), vec(3), pl.BlockSpec((ATT_TQ, LANES), lambda b, h, i: (b * qpb + i, h)),
                  lat_kv, lat_kv, ctx_kv, ctx_kv],
        out_specs=pl.BlockSpec((ATT_TQ, LANES), lambda b, h, i: (b * qpb + i, h)),
        out_shape=jax.ShapeDtypeStruct((n_b * n_seq, W_GROUP), BF16),
        scratch_shapes=scratch(ATT_TQ),
        compiler_params=_cparams(("arbitrary", "arbitrary", "arbitrary")),
        name="diff_attn_latent",
    )(lam_vec, sub, qn, kn, vn, kn, vn)
    ckv = pl.BlockSpec((n_ctx, LANES), lambda b, h: (ctx_blk0 + b, h))
    o_ctx = pl.pallas_call(
        functools.partial(_attn_body, (n_ctx,), 1.0 - lam_init),
        grid=(n_b, DIFF_HEADS),
        in_specs=[vec(2), vec(2), ckv, ckv, ckv],
        out_specs=pl.BlockSpec((n_ctx, LANES), lambda b, h: (b, h)),
        out_shape=jax.ShapeDtypeStruct((n_b * n_ctx, W_GROUP), BF16),
        scratch_shapes=scratch(n_ctx),
        compiler_params=_cparams(("arbitrary", "arbitrary")),
        name="diff_attn_context",
    )(lam_vec, sub, qn, kn, vn)
    return o_lat, o_ctx


def _rope_tables(n_seq):
    pos = jnp.arange(n_seq)
    row = (pos // GRID_W).astype(F32)
    col = (pos % GRID_W).astype(F32)
    inv_freq = ROPE_BASE ** (-jnp.arange(ROPE_AX, dtype=F32) / ROPE_AX)
    ang_r, ang_c = row[:, None] * inv_freq, col[:, None] * inv_freq
    z = jnp.zeros_like(ang_r)
    cos = jnp.concatenate([jnp.cos(ang_r)] * 2 + [jnp.cos(ang_c)] * 2, axis=1)
    s1 = jnp.concatenate([z, jnp.sin(ang_r), z, jnp.sin(ang_c)], axis=1)
    s2 = jnp.concatenate([-jnp.sin(ang_r), z, -jnp.sin(ang_c), z], axis=1)

    wide = lambda t: jnp.tile(t, (1, LANES // DIFF_D))
    return wide(cos), wide(s1), wide(s2)


def _rwkv_prep_body(n_b, n_l, n_c,
                    rp_ref, rc_ref, rn_ref, kp_ref, kc_ref, kn_ref, vp_ref, vc_ref, vn_ref, lp_ref, lc_ref, ln_ref,
                    mu_ref, w2_ref, a2_ref, g2_ref, w0_ref, a0_ref, kk_w_ref, ka_ref, rk_ref, e_ref,
                    r_ref, v_ref, kk_ref, lw0_ref, kd0_ref, b0_ref, lw1_ref, kd1_ref, b1_ref, bon_ref, g_ref,
                    buf_ref, bufl_ref):
    t = pl.program_id(0)
    p, n = _seg_pos(t, n_b, n_l, n_c)
    prev_ok = (p > 0).astype(F32)
    next_ok = (p < n - 1).astype(F32)

    def shifted(zp_ref, zc_ref, zn_ref, buf, mu):
        z = zc_ref[...].astype(F32)
        buf[0:HALO, :] = zp_ref[...].astype(F32) * prev_ok
        buf[HALO:HALO + SEQ_TILE, :] = z
        buf[HALO + SEQ_TILE:, :] = zn_ref[...].astype(F32) * next_ok
        return z + mu * (0.5 * (buf[HALO - 1:HALO - 1 + SEQ_TILE, :] + buf[HALO + 1:HALO + 1 + SEQ_TILE, :]) - z)

    r = shifted(rp_ref, rc_ref, rn_ref, buf_ref, mu_ref[:, 0:W_GROUP])
    k = shifted(kp_ref, kc_ref, kn_ref, buf_ref, mu_ref[:, W_GROUP:2 * W_GROUP])
    v = shifted(vp_ref, vc_ref, vn_ref, buf_ref, mu_ref[:, 2 * W_GROUP:3 * W_GROUP])
    low = shifted(lp_ref, lc_ref, ln_ref, bufl_ref, mu_ref[:, 3 * W_GROUP:])
    kk = k * kk_w_ref[...]
    ss = _segsum(kk * kk, e_ref)
    kk = kk / jnp.maximum(jnp.sqrt(ss), 1e-12)
    tw = jnp.tanh(low).astype(BF16)
    lb = low.astype(BF16)
    r_ref[0] = r.astype(r_ref.dtype)
    v_ref[0] = v.astype(v_ref.dtype)
    kk_ref[0] = kk.astype(kk_ref.dtype)
    ksum = jnp.zeros_like(k)
    for d, (lw_ref, kd_ref, b_ref) in enumerate(((lw0_ref, kd0_ref, b0_ref), (lw1_ref, kd1_ref, b1_ref))):
        y = w0_ref[d:d + 1, :] + _dot(tw, w2_ref[d])
        y = -y
        softplus = jnp.maximum(y, 0.0) + jnp.log(1.0 + jnp.exp(-jnp.abs(y)))
        lw_ref[0] = -jnp.exp(-softplus - 0.5)
        ag = jax.nn.sigmoid(a0_ref[d:d + 1, :] + _dot(lb, a2_ref[d]))
        kd = k * (1.0 + (ag - 1.0) * ka_ref[...])
        kd_ref[0] = kd.astype(kd_ref.dtype)
        b_ref[0] = (kk * ag).astype(b_ref.dtype)
        ksum = ksum + kd
    bon_ref[...] = (_segsum(r * ksum * rk_ref[...], e_ref) * v).astype(bon_ref.dtype)
    g_ref[...] = _dot(jax.nn.sigmoid(low).astype(BF16), g2_ref[...]).astype(g_ref.dtype)


def _rwkv_prep(z, p, e_seg, n_b, n_l, n_c):
    rows = z.shape[0]
    vec = pl.BlockSpec((1, W_GROUP), lambda t: (0, 0))
    two = pl.BlockSpec((2, W_GROUP), lambda t: (0, 0))
    lowm = pl.BlockSpec((2, LANES, W_GROUP), lambda t: (0, 0, 0))
    out = pl.BlockSpec((SEQ_TILE, W_GROUP), lambda t: (t, 0))

    def per_batch(t):
        tc = t - n_b * n_l
        is_lat = t < n_b * n_l
        return (jnp.where(is_lat, t // n_l, tc // n_c), jnp.where(is_lat, t % n_l, n_l + tc % n_c), 0)

    out_b = pl.BlockSpec((1, SEQ_TILE, W_GROUP), per_batch)
    t_b = (n_l + n_c) * SEQ_TILE
    z_specs = []
    for col, width in ((COL_R, W_GROUP), (COL_K, W_GROUP), (COL_V, W_GROUP), (COL_LOW, LANES)):
        prv, nxt = _halo_maps(lambda t: t, rows, col)
        z_specs += [pl.BlockSpec((HALO, width), prv),
                    pl.BlockSpec((SEQ_TILE, width), lambda t, col=col: (t, col)),
                    pl.BlockSpec((HALO, width), nxt)]
    return pl.pallas_call(
        functools.partial(_rwkv_prep_body, n_b, n_l, n_c),
        grid=(rows // SEQ_TILE,),
        in_specs=z_specs + [pl.BlockSpec((1, RWKV_COLS), lambda t: (0, 0)), lowm, lowm,
                            pl.BlockSpec((LANES, W_GROUP), lambda t: (0, 0)), two, two, vec, vec, vec,
                            pl.BlockSpec((W_GROUP, W_GROUP), lambda t: (0, 0))],
        out_specs=[out_b] * 9 + [out] * 2,
        out_shape=[jax.ShapeDtypeStruct((n_b, t_b, W_GROUP), F32 if j in (3, 6) else BF16) for j in range(9)]
        + [jax.ShapeDtypeStruct((rows, W_GROUP), BF16)] * 2,
        scratch_shapes=[pltpu.VMEM((SEQ_TILE + 2 * HALO, W_GROUP), F32),
                        pltpu.VMEM((SEQ_TILE + 2 * HALO, LANES), F32)],
        compiler_params=_cparams(("arbitrary",)),
        name="rwkv_prep",
    )(*([z] * 12), p["mu"], p["w2"], p["a2"], p["g2"], p["w0"], p["a0"], p["kk"], p["ka"], p["rk"], e_seg)


def _rwkv_scan_body(*refs):
    c = RW_CHUNK
    in_refs, y_refs, st_ref = refs[:12], refs[12:14], refs[14]
    n_pair = RWKV_HEADS // 2
    n_b = st_ref.shape[0]

    @pl.when(pl.program_id(0) == 0)
    def _():
        st_ref[...] = jnp.zeros_like(st_ref)

    rt = lax.broadcasted_iota(jnp.int32, (c, c), 0)
    ct = lax.broadcasted_iota(jnp.int32, (c, c), 1)
    lane = lax.broadcasted_iota(jnp.int32, (c, LANES), 1)
    head0 = lane < RWKV_N
    rr = lax.broadcasted_iota(jnp.int32, (2 * c, 2 * c), 0)
    cc = lax.broadcasted_iota(jnp.int32, (2 * c, 2 * c), 1)
    same = (rr >= c) == (cc >= c)
    tt = rr & (c - 1)
    ss = cc & (c - 1)

    def stack(x):
        z = jnp.zeros_like(x)
        return jnp.concatenate([jnp.where(head0, x, z), jnp.where(head0, z, x)], axis=0).astype(BF16)

    chains = []
    for bi, d in [(bi, d) for bi in range(n_b) for d in range(2)]:
        rev = d == 1
        r_ref, v_ref, kk_ref, lw_ref, kd_ref, b_ref = in_refs[6 * d:6 * d + 6]
        tri = ((rt <= ct) if rev else (rt >= ct)).astype(BF16)
        lw = lw_ref[bi]
        hi = lw.astype(BF16)
        r1 = lw - hi.astype(F32)
        mid = r1.astype(BF16)
        lo = (r1 - mid.astype(F32)).astype(BF16)
        cum = _dot(tri, hi) + _dot(tri, mid) + _dot(tri, lo)
        last = 0 if rev else c - 1
        tot = cum[last:last + 1, :]
        g_end = jnp.exp(tot - cum)
        g_inv = jnp.exp(-cum)
        full = dict(a=-kk_ref[bi] * jnp.exp(cum - lw), r=r_ref[bi] * jnp.exp(cum), k=kd_ref[bi] * g_inv,
                    b=b_ref[bi] * g_inv, kg=kd_ref[bi] * g_end, bg=b_ref[bi] * g_end, v=v_ref[bi])
        g_tot = jnp.exp(tot)
        strict = same & ((tt < ss) if rev else (tt > ss))
        incl = same & ((tt <= ss) if rev else (tt >= ss))
        for p in range(n_pair):
            sl = slice(p * LANES, (p + 1) * LANES)
            ch = {k: stack(x[:, sl]) for k, x in full.items()}
            ch.update(bi=bi, d=d, p=p, sl=sl, strict=strict, incl=incl, g_tot=g_tot[:, sl])
            chains.append(ch)

    c2 = 2 * c
    mask2 = lambda m, x: jnp.where(jnp.concatenate([m, m], axis=1), x, 0.0).astype(BF16)
    for ch in chains:
        kb = jnp.concatenate([ch["b"], ch["k"]], axis=0)
        sc = mask2(ch["strict"], _dot_nt(ch["a"], kb))
        ch["low"], ch["a_k"] = sc[:, :c2], sc[:, c2:]
        ch["r_bk"] = mask2(ch["incl"], _dot_nt(ch["r"], kb))
    for ch in chains:
        ch["st"] = st_ref[ch["bi"], ch["d"], ch["p"]]
        ch["stb"] = ch["st"].astype(BF16)
        ch["u"] = _dot_nt(ch["a"], ch["stb"]) + _dot(ch["a_k"], ch["v"])
    n_sq = int(math.log2(c)) - 1
    for it in range(n_sq + 1):
        for ch in chains:
            if it < n_sq:
                both = _dot(ch["low"], jnp.concatenate([ch["low"], ch["u"].astype(BF16)], axis=1))
                ch["low"], ch["u"] = both[:, :c2].astype(BF16), ch["u"] + both[:, c2:]
            else:
                ch["u"] = ch["u"] + _dot(ch["low"], ch["u"].astype(BF16))
    for ch in chains:
        ub = ch["u"].astype(BF16)
        uv = jnp.concatenate([ub, ch["v"]], axis=0)
        y = _dot_nt(ch["r"], ch["stb"]) + _dot(ch["r_bk"], uv)
        y_refs[ch["d"]][ch["bi"], :, ch["sl"]] = y[:c] + y[c:]
        st_ref[ch["bi"], ch["d"], ch["p"]] = (ch["st"] * ch["g_tot"]
                                              + _dot_tn(uv, jnp.concatenate([ch["bg"], ch["kg"]], axis=0)))


def _rwkv_scan(pre, n_l, n_c):
    n_b, t_b, _ = pre[0].shape

    def chunk(rev, i):
        ctx = n_l + ((n_c - 1 - i) if rev else i)
        lat = (n_l - 1 - (i - n_c)) if rev else (i - n_c)
        return jnp.where(i < n_c, ctx, lat)

    blk = lambda rev: pl.BlockSpec((n_b, RW_CHUNK, W_GROUP), lambda i, rev=rev: (0, chunk(rev, i), 0))
    args, specs = [], []
    for d in range(2):
        args += [pre[0], pre[1], pre[2], pre[3 + 3 * d], pre[4 + 3 * d], pre[5 + 3 * d]]
        specs += [blk(d == 1)] * 6
    return pl.pallas_call(
        _rwkv_scan_body,
        grid=(n_c + n_l,),
        in_specs=specs,
        out_specs=[blk(False), blk(True)],
        out_shape=[jax.ShapeDtypeStruct((n_b, t_b, W_GROUP), F32)] * 2,
        scratch_shapes=[pltpu.VMEM((n_b, 2, RWKV_HEADS // 2, LANES, LANES), F32)],
        compiler_params=_cparams(("arbitrary",)),
        name="rwkv_scan",
    )(*args)


def _rwkv_finish_body(yf_ref, yb_ref, bon_ref, g_ref, lg_ref, lb_ref, e_ref, o_ref):
    y = yf_ref[0] + yb_ref[0]
    dlt = y - _segsum(y, e_ref) * (1.0 / RWKV_N)
    var = _segsum(dlt * dlt, e_ref) * (1.0 / RWKV_N)
    yn = dlt * lax.rsqrt(var + GN_EPS) * lg_ref[...] + lb_ref[...]
    o_ref[...] = ((yn + bon_ref[...]) * g_ref[...]).astype(BF16)


def _rwkv_finish(yf, yb, bonus, gate, lnx_g, lnx_b, e_seg, n_l, n_c):
    rows = bonus.shape[0]
    n_b = yf.shape[0]
    blk = pl.BlockSpec((SEQ_TILE, W_GROUP), lambda i: (i, 0))
    vec = pl.BlockSpec((1, W_GROUP), lambda i: (0, 0))

    def per_batch(t):
        tc = t - n_b * n_l
        is_lat = t < n_b * n_l
        return (jnp.where(is_lat, t // n_l, tc // n_c), jnp.where(is_lat, t % n_l, n_l + tc % n_c), 0)

    yblk = pl.BlockSpec((1, SEQ_TILE, W_GROUP), per_batch)
    return pl.pallas_call(
        _rwkv_finish_body,
        grid=(rows // SEQ_TILE,),
        in_specs=[yblk, yblk, blk, blk, vec, vec, pl.BlockSpec((W_GROUP, W_GROUP), lambda i: (0, 0))],
        out_specs=blk,
        out_shape=jax.ShapeDtypeStruct((rows, W_GROUP), BF16),
        compiler_params=_cparams(("arbitrary",)),
        name="rwkv_finish",
    )(yf, yb, bonus, gate, lnx_g.reshape(1, W_GROUP), lnx_b.reshape(1, W_GROUP), e_seg)


def _outproj_body(x_ref, ya_ref, yb_ref, yc_ref, yd_ref, w_ref, mod_ref, g2_ref, wrh_ref, wrl_ref, br_ref,
                  xo_ref, h2_ref, lg_ref):
    mix = _dot(ya_ref[...], w_ref[0:W_GROUP, :])
    mix = mix + _dot(yb_ref[...], w_ref[W_GROUP:2 * W_GROUP, :])
    mix = mix + _dot(yc_ref[...], w_ref[2 * W_GROUP:3 * W_GROUP, :])
    mix = mix + _dot(yd_ref[...], w_ref[3 * W_GROUP:, :])
    x = x_ref[...] + mod_ref[0, 2:3, :] * mix
    xo_ref[...] = x
    ms = jnp.mean(x * x, axis=-1, keepdims=True)
    h2 = x * lax.rsqrt(ms + NORM_EPS) * g2_ref[...]
    h2 = h2 * (1.0 + mod_ref[0, 4:5, :]) + mod_ref[0, 3:4, :]
    h2_ref[...] = h2.astype(BF16)
    lg_ref[...] = _dot3(h2, wrh_ref[...], wrl_ref[...]) + br_ref[...]


def _outproj(x, ys, w_out, mod3, g2, wr_hi, wr_lo, br, n_b, n_seq):
    rows = x.shape[0]
    big = pl.BlockSpec((OUT_TILE, D_MODEL), lambda i: (i, 0))
    yb = pl.BlockSpec((OUT_TILE, W_GROUP), lambda i: (i, 0))
    cst = lambda shape: pl.BlockSpec(shape, lambda i: (0, 0))
    return pl.pallas_call(
        _outproj_body,
        grid=(rows // OUT_TILE,),
        in_specs=[big, yb, yb, yb, yb, cst((D_MODEL, D_MODEL)),
                  pl.BlockSpec((1, 6, D_MODEL), lambda i: (_mod_row(i, OUT_TILE, n_b, n_seq), 0, 0)),
                  cst((1, D_MODEL)), cst((D_MODEL, LANES)), cst((D_MODEL, LANES)), cst((1, LANES))],
        out_specs=[big, big, pl.BlockSpec((OUT_TILE, LANES), lambda i: (i, 0))],
        out_shape=[jax.ShapeDtypeStruct((rows, D_MODEL), F32),
                   jax.ShapeDtypeStruct((rows, D_MODEL), BF16),
                   jax.ShapeDtypeStruct((rows, LANES), F32)],
        compiler_params=_cparams(("arbitrary",)),
        name="outproj_norm_router",
    )(x, ys[0], ys[1], ys[2], ys[3], w_out, mod3, g2.reshape(1, D_MODEL), wr_hi, wr_lo, br)


def _expert_body(be_ref, nv_ref, new_ref, x_ref, w1_ref, w3_ref, w2_ref, o_ref, w1b_ref, w3b_ref, w2b_ref):
    i = pl.program_id(0)

    @pl.when(new_ref[i] == 1)
    def _():
        w1b_ref[...] = w1_ref[0, 0].astype(BF16)
        w3b_ref[...] = w3_ref[0, 0].astype(BF16)
        w2b_ref[...] = w2_ref[0, 0].astype(BF16)

    @pl.when(i < nv_ref[0])
    def _():
        x = x_ref[...]
        h1 = _dot(x, w1b_ref[...])
        hid = (h1 * jax.nn.sigmoid(h1)) * _dot(x, w3b_ref[...])
        o_ref[...] = _dot(hid.astype(BF16), w2b_ref[...]).astype(o_ref.dtype)

    @pl.when(i >= nv_ref[0])
    def _():
        o_ref[...] = jnp.zeros_like(o_ref)


def _experts(xb, block_e, n_valid, layer, w1, w3, w2):
    n_blocks = xb.shape[0] // MOE_BM
    is_new = jnp.concatenate([jnp.ones((1,), jnp.int32), (block_e[1:] != block_e[:-1]).astype(jnp.int32)])
    wspec = lambda shape, nbuf=1: pl.BlockSpec((1, 1) + shape, lambda i, be, nv, nw: (layer, be[i], 0, 0),
                                               pipeline_mode=pl.Buffered(nbuf))
    return pl.pallas_call(
        _expert_body,
        grid_spec=pltpu.PrefetchScalarGridSpec(
            num_scalar_prefetch=3,
            grid=(n_blocks,),
            in_specs=[pl.BlockSpec((MOE_BM, D_MODEL), lambda i, be, nv, nw: (i, 0)),
                      wspec((D_MODEL, D_EXPERT)), wspec((D_MODEL, D_EXPERT)), wspec((D_EXPERT, D_MODEL), 2)],
            out_specs=pl.BlockSpec((MOE_BM, D_MODEL), lambda i, be, nv, nw: (i, 0)),
            scratch_shapes=[pltpu.VMEM((D_MODEL, D_EXPERT), BF16), pltpu.VMEM((D_MODEL, D_EXPERT), BF16),
                            pltpu.VMEM((D_EXPERT, D_MODEL), BF16)]),
        out_shape=jax.ShapeDtypeStruct((xb.shape[0], D_MODEL), BF16),
        compiler_params=_cparams(("arbitrary",)),
        name="moe_experts",
    )(block_e, n_valid, is_new, xb, w1, w3, w2)


def _combine_body(x_ref, y0_ref, y1_ref, rt_ref, mod_ref, o_ref):
    rt = rt_ref[...]
    mo = rt[:, RT_W:RT_W + 1] * y0_ref[...].astype(F32) + rt[:, RT_W + 1:RT_W + 2] * y1_ref[...].astype(F32)
    o_ref[...] = x_ref[...] + mod_ref[0, 5:6, :] * mo


def _combine(x, y0, y1, route, mod3, n_b, n_seq, n_out_rows):
    big = pl.BlockSpec((ROW_TILE, D_MODEL), lambda i: (i, 0))
    return pl.pallas_call(
        _combine_body,
        grid=(n_out_rows // ROW_TILE,),
        in_specs=[big, big, big, pl.BlockSpec((ROW_TILE, LANES), lambda i: (i, 0)),
                  pl.BlockSpec((1, 6, D_MODEL), lambda i: (_mod_row(i, ROW_TILE, n_b, n_seq), 0, 0))],
        out_specs=big,
        out_shape=jax.ShapeDtypeStruct((n_out_rows, D_MODEL), F32),
        compiler_params=_cparams(("arbitrary",)),
        name="moe_combine",
    )(x, y0, y1, route, mod3)


RT_E, RT_RANK, RT_W = 0, 2, 4
LOGIT_E0 = N_EGROUPS


def _route_body(lg_ref, rt_ref, cnt_ref, run_ref):
    tm = lg_ref.shape[0]

    @pl.when(pl.program_id(0) == 0)
    def _():
        run_ref[...] = jnp.zeros_like(run_ref)

    x = lg_ref[...]
    lane = lax.broadcasted_iota(jnp.int32, (tm, LANES), 1).astype(F32)
    neg = jnp.float32(-jnp.inf)

    def first_max(v):
        m = jnp.max(v, axis=-1, keepdims=True)
        return m, jnp.min(jnp.where(v == m, lane, float(LANES)), axis=-1, keepdims=True)

    g = jnp.where(lane < N_EGROUPS, x, neg)
    mg, gsel = first_max(g)
    gate_g = 1.0 / jnp.sum(jnp.exp(g - mg), axis=-1, keepdims=True)
    lo = LOGIT_E0 + EXP_PER_GROUP * gsel
    e = jnp.where((lane >= lo) & (lane < lo + EXP_PER_GROUP), x, neg)
    m1, i1 = first_max(e)
    m2, i2 = first_max(jnp.where(lane == i1, neg, e))
    e2 = jnp.exp(m2 - m1)
    w1 = gate_g / (1.0 + e2)
    w2 = gate_g * e2 / (1.0 + e2)
    oh1 = lane == i1
    oh2 = lane == i2
    oh = (oh1 | oh2).astype(BF16)
    rr = lax.broadcasted_iota(jnp.int32, (tm, tm), 0)
    cc = lax.broadcasted_iota(jnp.int32, (tm, tm), 1)
    before = _dot((rr > cc).astype(BF16), oh) + run_ref[0:1, :]
    r1 = jnp.sum(jnp.where(oh1, before, 0.0), axis=-1, keepdims=True)
    r2 = jnp.sum(jnp.where(oh2, before, 0.0), axis=-1, keepdims=True)
    run_ref[...] = run_ref[...] + jnp.sum(oh.astype(F32), axis=0, keepdims=True)
    cnt_ref[...] = run_ref[...]
    rec = jnp.where(lane == RT_E, i1 - LOGIT_E0, 0.0)
    rec = jnp.where(lane == RT_E + 1, i2 - LOGIT_E0, rec)
    rec = jnp.where(lane == RT_RANK, r1, rec)
    rec = jnp.where(lane == RT_RANK + 1, r2, rec)
    rec = jnp.where(lane == RT_W, w1, rec)
    rt_ref[...] = jnp.where(lane == RT_W + 1, w2, rec)


def _route(logits):
    rows = logits.shape[0]
    blk = pl.BlockSpec((ROW_TILE, LANES), lambda i: (i, 0))
    return pl.pallas_call(
        _route_body,
        grid=(rows // ROW_TILE,),
        in_specs=[blk],
        out_specs=[blk, pl.BlockSpec((SUBLANES, LANES), lambda i: (0, 0))],
        out_shape=[jax.ShapeDtypeStruct((rows, LANES), F32), jax.ShapeDtypeStruct((SUBLANES, LANES), F32)],
        scratch_shapes=[pltpu.VMEM((SUBLANES, LANES), F32)],
        compiler_params=_cparams(("arbitrary",)),
        name="moe_route",
    )(logits)


def _moe(x, h2, logits, mod3, layer, w1, w3, w2, n_b, n_seq, n_out_rows):
    rows = x.shape[0]
    route, cnt = _route(logits)
    counts = cnt[0, LOGIT_E0:LOGIT_E0 + N_EXPERTS].astype(jnp.int32)
    pcounts = ((counts + MOE_BM - 1) // MOE_BM) * MOE_BM
    pend = jnp.cumsum(pcounts)
    pstart = pend - pcounts
    eid = route[:, RT_E:RT_E + 2].astype(jnp.int32)
    dest = pstart[eid] + route[:, RT_RANK:RT_RANK + 2].astype(jnp.int32)
    n_blocks = -(-(2 * rows) // MOE_BM) + N_EXPERTS
    tok = jnp.broadcast_to(jnp.arange(rows, dtype=jnp.int32)[:, None], (rows, 2))
    slot_tok = (jnp.arange(n_blocks * MOE_BM, dtype=jnp.int32) % rows).at[dest.reshape(-1)].set(tok.reshape(-1))
    take_rows = lambda a, idx: a.at[idx].get(mode="promise_in_bounds")
    xb = take_rows(h2, slot_tok)
    blk_start = jnp.arange(n_blocks) * MOE_BM
    block_e = jnp.sum(blk_start[:, None] >= pend[None, :], axis=1)
    block_e = jnp.minimum(block_e, N_EXPERTS - 1).astype(jnp.int32)
    n_valid = (pend[-1] // MOE_BM).astype(jnp.int32).reshape(1)
    yb = _experts(xb, block_e, n_valid, layer, w1, w3, w2)
    y0 = take_rows(yb, dest[:n_out_rows, 0])
    y1 = take_rows(yb, dest[:n_out_rows, 1])
    return _combine(x, y0, y1, route, mod3, n_b, n_seq, n_out_rows)


def _pad_rows(w, lo, n):
    pad = [(0, 0)] * (w.ndim - 2) + [(lo, n - lo - w.shape[-2]), (0, 0)]
    return jnp.pad(w.astype(F32), pad).astype(BF16)


def kernel(x, c, ctx, c_ctx, w_ada, b_ada, g_norm1, g_norm2, w_in, w_out, s5_lam_re, s5_lam_im, s5_log_step, s5_b_re, s5_b_im, s5_c_re, s5_c_im, s5_d, s5_w_glu, s5_b_glu, lru_conv_w, lru_conv_b, lru_lam, lru_wa, lru_ba, lru_wx, lru_bx, diff_gq, diff_gk, diff_lq1, diff_lk1, diff_lq2, diff_lk2, diff_subln, rw_mu, rw_w0, rw_w2, rw_a0, rw_a2, rw_g2, rw_kk, rw_ka, rw_rk, rw_lnx_g, rw_lnx_b, moe_w_rg, moe_b_rg, moe_w_re, moe_b_re, moe_w1, moe_w3, moe_w2):
    n_b, n_seq, _ = x.shape
    n_ctx = ctx.shape[1]
    depth = w_ada.shape[0]
    assert n_seq % SEQ_TILE == 0 and n_ctx % SEQ_TILE == 0 and n_seq % n_ctx == 0
    assert (n_b * n_ctx) % ROW_TILE == 0 and n_seq % ROW_TILE == 0 and n_seq % ATT_TK == 0
    xs = jnp.concatenate([x.reshape(n_b * n_seq, D_MODEL), ctx.reshape(n_b * n_ctx, D_MODEL)], axis=0)
    rows = xs.shape[0]
    cvec = jnp.zeros((SUBLANES, D_MODEL), F32).at[0].set(c_ctx).at[1:1 + n_b].set(c)
    rope = _rope_tables(n_seq)
    seg = jnp.arange(W_GROUP) // RWKV_N
    e_seg = (seg[:, None] == seg[None, :]).astype(BF16)
    nl_s, nc_s = n_seq // SEQ_TILE, n_ctx // SEQ_TILE
    nl_r, nc_r = n_seq // RW_CHUNK, n_ctx // RW_CHUNK

    for l in range(depth):
        lam_init = 0.8 - 0.6 * math.exp(-0.3 * l)
        mod = _modulation(cvec, l, w_ada, b_ada[l])
        mod3 = mod[:1 + n_b].reshape(1 + n_b, 6, D_MODEL)
        w_in_b = jnp.pad(w_in[l].astype(BF16), ((0, 0), (0, D_IN_PAD - D_IN)))
        za = _inproj(xs, mod3, g_norm1[l], w_in_b, n_b, n_seq)
        s5p = _s5_prepare(s5_lam_re[l], s5_lam_im[l], s5_log_step[l], s5_b_re[l], s5_b_im[l],
                          s5_c_re[l], s5_c_im[l])
        y5 = [_s5_scan(za, s5p, d, n_b, nl_s, nc_s) for d in range(2)]
        ya = _s5_finish(y5[0], y5[1], za, s5_d[l], s5_w_glu[l], s5_b_glu[l], nl_s, nc_s)
        lrp = dict(cw=lru_conv_w[l], cb=lru_conv_b[l].reshape(1, W_GROUP),
                   wa=_blockdiag(lru_wa[l]).astype(BF16), wx=_blockdiag(lru_wx[l]).astype(BF16),
                   ba=lru_ba[l].reshape(2, 1, W_GROUP), bx=lru_bx[l].reshape(2, 1, W_GROUP),
                   sp=jax.nn.softplus(-lru_lam[l]).reshape(2, 1, W_GROUP))
        hl = [_lru_scan(za, lrp, d, n_b, nl_s, nc_s) for d in range(2)]
        yb_ = _lru_finish(hl[0], hl[1], za)
        lam = jnp.exp(jnp.sum(diff_lq1[l] * diff_lk1[l])) - jnp.exp(jnp.sum(diff_lq2[l] * diff_lk2[l])) + lam_init
        lam_vec = jnp.full((1, LANES), lam, F32)
        qn, kn, vn = _qkv_prep(za, rope, diff_gq[l], diff_gk[l], e_seg, n_b, n_seq)
        o_lat, o_ctx = _attention(qn, kn, vn, lam_vec, diff_subln[l], lam_init, n_b, n_seq, n_ctx)
        yc = jnp.concatenate([o_lat, o_ctx], axis=0)
        lo_w, lo_a = RANK_G, RANK_G + RANK_W
        rwp = dict(mu=rw_mu[l].reshape(1, RWKV_COLS),
                   w2=_pad_rows(rw_w2[l], lo_w, LANES), a2=_pad_rows(rw_a2[l], lo_a, LANES),
                   g2=_pad_rows(rw_g2[l], 0, LANES), w0=rw_w0[l], a0=rw_a0[l],
                   kk=rw_kk[l].reshape(1, W_GROUP), ka=rw_ka[l].reshape(1, W_GROUP),
                   rk=rw_rk[l].reshape(1, W_GROUP))
        pre = _rwkv_prep(za, rwp, e_seg, n_b, nl_s, nc_s)
        yr = _rwkv_scan(pre, nl_r, nc_r)
        yd = _rwkv_finish(yr[0], yr[1], pre[9], pre[10], rw_lnx_g[l], rw_lnx_b[l], e_seg, nl_s, nc_s)
        wr = jnp.zeros((D_MODEL, LANES), F32).at[:, :N_EGROUPS].set(moe_w_rg[l])
        wr = wr.at[:, N_EGROUPS:N_EGROUPS + N_EXPERTS].set(moe_w_re[l])
        wr_hi, wr_lo = _split2(wr)
        br = jnp.zeros((1, LANES), F32).at[0, :N_EGROUPS].set(moe_b_rg[l])
        br = br.at[0, N_EGROUPS:N_EGROUPS + N_EXPERTS].set(moe_b_re[l])
        xs, h2, logits = _outproj(xs, (ya, yb_, yc, yd), w_out[l].astype(BF16), mod3, g_norm2[l],
                                  wr_hi, wr_lo, br, n_b, n_seq)
        n_out_rows = rows if l < depth - 1 else n_b * n_seq
        xs = _moe(xs, h2, logits, mod3, l, moe_w1, moe_w3, moe_w2, n_b, n_seq, n_out_rows)
    return xs.reshape(n_b, n_seq, D_MODEL)
```
